```python
import math
import jax
import jax.numpy as jnp
from jax import lax
import numpy as np

D_MODEL = 1024
BATCH = 32
SEQ = 256
DEPTH = 2
DEC_BATCH = 4
DEC_SEQ = 2048
PAST_LEN = 512

GRID_W = 64
MIX_DIM = D_MODEL
V_DIM = 128
QK_DIM = V_DIM // 2
N_ATT_HEADS = (MIX_DIM // 2) // V_DIM
CONV_DIM = MIX_DIM // 4
N_CONV_GROUPS = 4
FOURIER_DIM = MIX_DIM // 4
N_FOURIER_GROUPS = 4
FOURIER_GROUP_DIM = FOURIER_DIM // N_FOURIER_GROUPS
ATT_QK = N_ATT_HEADS * 2 * QK_DIM
ATT_V = N_ATT_HEADS * V_DIM
IN_DIM = 2 * ATT_QK + ATT_V + 3 * CONV_DIM + FOURIER_DIM
SPLITS = (ATT_QK, 2 * ATT_QK, 2 * ATT_QK + ATT_V,
          2 * ATT_QK + ATT_V + CONV_DIM,
          2 * ATT_QK + ATT_V + 2 * CONV_DIM,
          2 * ATT_QK + ATT_V + 3 * CONV_DIM)
D_FF = ((8 * D_MODEL // 3 + 127) // 128) * 128
CONV_WIDTH = 3
N_MOD = 9
ROT_AXIS_DIM = QK_DIM // 2
ROT_FREQS = ROT_AXIS_DIM // 2
ROPE_THETA = 10000.0
Q_BLOCK = 128
FFN_RESIDUAL_WEIGHT = 0.5
EPS = 1e-6

kernel_name = "hybrid_diffattn_conv_fourier_dit_step"


def _lambda_init(layer_idx):
    return 0.8 - 0.6 * math.exp(-0.3 * layer_idx)


def _rmsnorm(x, g):
    xf = x.astype(jnp.float32)
    y = xf * lax.rsqrt(jnp.mean(xf * xf, axis=-1, keepdims=True) + EPS)
    return (y * g.astype(jnp.float32)).astype(x.dtype)


def _modulation(cvec, w_mod_l, b_mod_l):
    m = jax.nn.silu(cvec) @ w_mod_l + b_mod_l
    return m.reshape(-1, 1, N_MOD, D_MODEL)


def _axial_angles(n):
    rows = n // GRID_W
    t = jnp.arange(rows * GRID_W)
    row = (t // GRID_W).astype(jnp.float32)
    col = (t % GRID_W).astype(jnp.float32)
    inv = 1.0 / (ROPE_THETA ** (jnp.arange(ROT_FREQS, dtype=jnp.float32) / ROT_FREQS))
    return row[:, None] * inv, col[:, None] * inv


def _rot(x, ang):
    cos = jnp.cos(ang)[None, :, None, None, :].astype(x.dtype)
    sin = jnp.sin(ang)[None, :, None, None, :].astype(x.dtype)
    x1, x2 = x[..., :ROT_FREQS], x[..., ROT_FREQS:]
    return jnp.concatenate([x1 * cos - x2 * sin, x2 * cos + x1 * sin], axis=-1)


def _rope_2d(x, ang_r, ang_c):
    return jnp.concatenate([_rot(x[..., :ROT_AXIS_DIM], ang_r),
                            _rot(x[..., ROT_AXIS_DIM:], ang_c)], axis=-1)


def _diff_attention(q, k, v, lam, lambda_init, subln_g):
    b, lq, h, _, dk = q.shape
    nb = lq // Q_BLOCK
    qb = q.reshape(b, nb, Q_BLOCK, h, 2, dk).transpose(1, 0, 2, 3, 4, 5)
    scale = dk ** -0.5

    def one_block(qblk):
        s = jnp.einsum('bqhmd,bkhmd->bhmqk', qblk, k).astype(jnp.float32) * scale
        p = jax.nn.softmax(s, axis=-1)
        a = p[:, :, 0] - lam * p[:, :, 1]
        o = jnp.einsum('bhqk,bkhe->bqhe', a.astype(v.dtype), v)
        return _rmsnorm(o, subln_g) * (1.0 - lambda_init)

    ob = lax.map(one_block, qb)
    return ob.transpose(1, 0, 2, 3, 4).reshape(b, lq, h * v.shape[-1])


def _short_conv(u, w):
    up = jnp.pad(u, ((0, 0), (1, 1), (0, 0)))
    return up[:, :-2] * w[0] + up[:, 1:-1] * w[1] + up[:, 2:] * w[2]


def _fourier_mix(f):
    b, n, _ = f.shape
    ff = f.astype(jnp.float32).reshape(b, n, N_FOURIER_GROUPS, FOURIER_GROUP_DIM)
    out = jnp.fft.fft2(ff, axes=(1, 3), norm="ortho").real
    return out.reshape(b, n, FOURIER_DIM).astype(f.dtype)


def _ffn_sub(x, shift, scale, gate, g_pre, g_post, w_gu, w_down):
    h = _rmsnorm(x, g_pre) * (1 + scale) + shift
    g, u = jnp.split(h @ w_gu, 2, axis=-1)
    y = (jax.nn.silu(g) * u) @ w_down
    return x + FFN_RESIDUAL_WEIGHT * gate * _rmsnorm(y, g_post)


def _mixer_sub(x, shift, scale, gate, g_pre, g_post, w_in_l, conv_w_l, w_out_l,
               lam, lambda_init, subln_g_l, ctx_k, ctx_v):
    b, n, _ = x.shape
    h = _rmsnorm(x, g_pre) * (1 + scale) + shift
    p = h @ w_in_l
    q, k, v, gb, gc, hc, f = jnp.split(p, SPLITS, axis=-1)
    q = q.reshape(b, n, N_ATT_HEADS, 2, QK_DIM)
    k = k.reshape(b, n, N_ATT_HEADS, 2, QK_DIM)
    v = v.reshape(b, n, N_ATT_HEADS, V_DIM)
    if ctx_k is None:
        keys, vals = k, v
    else:
        ang_r, ang_c = _axial_angles(n)
        q = _rope_2d(q, ang_r, ang_c)
        keys = jnp.concatenate([_rope_2d(k, ang_r, ang_c), ctx_k.astype(k.dtype)], axis=1)
        vals = jnp.concatenate([v, ctx_v.astype(v.dtype)], axis=1)
    att = _diff_attention(q, keys, vals, lam, lambda_init, subln_g_l)
    conv = gb * _short_conv(gc * hc, conv_w_l)
    four = _fourier_mix(f)
    y = jnp.concatenate([att, conv, four], axis=-1) @ w_out_l
    return x + gate * _rmsnorm(y, g_post), k, v


def _layer(x, l, cvec, w_mod, b_mod, norm_g, w_ffn_gu, w_ffn_down, w_in, w_out,
           conv_w, lam_qk, subln_g, ctx_k, ctx_v):
    mod = _modulation(cvec, w_mod[l], b_mod[l])
    g = norm_g[l]
    lambda_init = _lambda_init(l)
    lq = lam_qk[l].astype(jnp.float32)
    lam = jnp.exp(jnp.sum(lq[0] * lq[1])) - jnp.exp(jnp.sum(lq[2] * lq[3])) + lambda_init
    x = _ffn_sub(x, mod[:, :, 0], mod[:, :, 1], mod[:, :, 2], g[0], g[1],
                 w_ffn_gu[l, 0], w_ffn_down[l, 0])
    x, k, v = _mixer_sub(x, mod[:, :, 3], mod[:, :, 4], mod[:, :, 5], g[2], g[3],
                         w_in[l], conv_w[l], w_out[l], lam, lambda_init, subln_g[l],
                         ctx_k, ctx_v)
    x = _ffn_sub(x, mod[:, :, 6], mod[:, :, 7], mod[:, :, 8], g[4], g[5],
                 w_ffn_gu[l, 1], w_ffn_down[l, 1])
    return x, k, v


def setup_inputs(seed: int = 0) -> dict:
    key = jax.random.key(seed)
    ks = jax.random.split(key, 16)
    f32 = jnp.float32
    x_prompt = jax.random.normal(ks[0], (BATCH, SEQ, D_MODEL), f32)
    x_sample = jax.random.normal(ks[1], (DEC_BATCH, DEC_SEQ, D_MODEL), f32)
    cache_k = jax.random.normal(ks[2], (DEC_BATCH, DEPTH, PAST_LEN, N_ATT_HEADS, 2, QK_DIM), f32)
    cache_v = jax.random.normal(ks[3], (DEC_BATCH, DEPTH, PAST_LEN, N_ATT_HEADS, V_DIM), f32)
    c = jax.random.normal(ks[4], (DEC_BATCH, D_MODEL), f32)
    c_ctx = jax.random.normal(ks[5], (D_MODEL,), f32)
    w_mod = jax.random.normal(ks[6], (DEPTH, D_MODEL, N_MOD * D_MODEL), f32) * (0.5 * D_MODEL ** -0.5)
    b_mod = jax.random.normal(ks[7], (DEPTH, N_MOD * D_MODEL), f32) * 0.02
    norm_g = 1.0 + 0.02 * jax.random.normal(ks[8], (DEPTH, 6, D_MODEL), f32)
    w_ffn_gu = jax.random.normal(ks[9], (DEPTH, 2, D_MODEL, 2 * D_FF), f32) * D_MODEL ** -0.5
    w_ffn_down = jax.random.normal(ks[10], (DEPTH, 2, D_FF, D_MODEL), f32) * D_FF ** -0.5
    w_in = jax.random.normal(ks[11], (DEPTH, D_MODEL, IN_DIM), f32) * D_MODEL ** -0.5
    w_out = jax.random.normal(ks[12], (DEPTH, MIX_DIM, D_MODEL), f32) * MIX_DIM ** -0.5
    conv_w = jax.random.normal(ks[13], (DEPTH, CONV_WIDTH, CONV_DIM), f32) * CONV_WIDTH ** -0.5
    lam_qk = jax.random.normal(ks[14], (DEPTH, 4, QK_DIM), f32) * 0.1
    subln_g = 1.0 + 0.02 * jax.random.normal(ks[15], (DEPTH, V_DIM), f32)
    return {"x_prompt": x_prompt, "x_sample": x_sample, "cache_k": cache_k, "cache_v": cache_v,
            "c": c, "c_ctx": c_ctx, "w_mod": w_mod, "b_mod": b_mod, "norm_g": norm_g,
            "w_ffn_gu": w_ffn_gu, "w_ffn_down": w_ffn_down, "w_in": w_in, "w_out": w_out,
            "conv_w": conv_w, "lam_qk": lam_qk, "subln_g": subln_g}


def reference(x_prompt, x_sample, cache_k, cache_v, c, c_ctx, w_mod, b_mod, norm_g,
              w_ffn_gu, w_ffn_down, w_in, w_out, conv_w, lam_qk, subln_g):
    y_prompt = x_prompt
    ks, vs = [], []
    for l in range(DEPTH):
        y_prompt, k, v = _layer(y_prompt, l, c_ctx, w_mod, b_mod, norm_g, w_ffn_gu, w_ffn_down,
                                w_in, w_out, conv_w, lam_qk, subln_g, None, None)
        ks.append(k)
        vs.append(v)
    new_cache_k = jnp.stack(ks, axis=1)
    new_cache_v = jnp.stack(vs, axis=1)
    y_sample = x_sample
    for l in range(DEPTH):
        y_sample, _, _ = _layer(y_sample, l, c, w_mod, b_mod, norm_g, w_ffn_gu, w_ffn_down,
                                w_in, w_out, conv_w, lam_qk, subln_g,
                                cache_k[:, l], cache_v[:, l])
    return (y_prompt, y_sample, new_cache_k, new_cache_v)
```

```python
import functools
import math

import jax
import jax.numpy as jnp
import numpy as np
from jax import lax
from jax.experimental import pallas as pl
from jax.experimental.pallas import tpu as pltpu

D_MODEL = 1024
DEPTH = 2
GRID_W = 64
V_DIM = 128
QK_DIM = V_DIM // 2
N_HEADS = 4
ATT_DIM = N_HEADS * V_DIM
CONV_DIM = 256
FOURIER_DIM = 256
FOURIER_GROUP_DIM = 64
IN_DIM = 3 * ATT_DIM + 3 * CONV_DIM + FOURIER_DIM
D_FF = 2816
N_MOD = 9
ROT_FREQS = 16
ROPE_THETA = 10000.0
EPS = 1e-6
FFN_RESIDUAL_WEIGHT = 0.5

FF_CHUNK = 256
N_FF_CHUNKS = D_FF // FF_CHUNK
MOD_ROWS = 8
SUBLANES = 8
LANES = 128
VMEM_LIMIT = 56 * 1024 * 1024

F32 = jnp.float32
BF16 = jnp.bfloat16


def _lambda_init(layer_idx):
    return 0.8 - 0.6 * math.exp(-0.3 * layer_idx)


def _rms(x, g):
    return x * lax.rsqrt(jnp.mean(x * x, axis=-1, keepdims=True) + EPS) * g


def _dot(a, b):
    return jnp.dot(a, b, preferred_element_type=F32)


def _params(semantics):
    return pltpu.CompilerParams(dimension_semantics=semantics, vmem_limit_bytes=VMEM_LIMIT)


def _resident(block_shape, index_map):
    return pl.BlockSpec(block_shape, index_map, pipeline_mode=pl.Buffered(1))


def _mod_kernel(c_ref, w_ref, b_ref, o_ref):
    c = c_ref[...]
    s = (c / (1.0 + jnp.exp(-c))).astype(BF16)
    o_ref[...] = _dot(s, w_ref[...].astype(BF16)) + b_ref[...]


def _modulation(cvecs, w_mod, b_mod):
    col = D_MODEL
    out = pl.pallas_call(
        _mod_kernel,
        out_shape=jax.ShapeDtypeStruct((DEPTH, MOD_ROWS, N_MOD * D_MODEL), F32),
        grid=(DEPTH, N_MOD * D_MODEL // col),
        in_specs=[
            pl.BlockSpec((MOD_ROWS, D_MODEL), lambda l, j: (0, 0)),
            pl.BlockSpec((None, D_MODEL, col), lambda l, j: (l, 0, j)),
            pl.BlockSpec((None, 1, col), lambda l, j: (l, 0, j)),
        ],
        out_specs=pl.BlockSpec((None, MOD_ROWS, col), lambda l, j: (l, 0, j)),
        compiler_params=_params(("arbitrary", "arbitrary")),
        name="modulation",
    )(cvecs, w_mod, b_mod.reshape(DEPTH, 1, N_MOD * D_MODEL))
    return out.reshape(DEPTH, MOD_ROWS, N_MOD, D_MODEL)


def _ffn_kernel(x_ref, mod_ref, g_ref, wgu_ref, wd_ref, o_ref, acc_ref, *, sub, g_row):
    x = x_ref[...]
    shift = mod_ref[3 * sub:3 * sub + 1, :]
    scale = mod_ref[3 * sub + 1:3 * sub + 2, :]
    gate = mod_ref[3 * sub + 2:3 * sub + 3, :]
    h = _rms(x, g_ref[g_row:g_row + 1, :]) * (1.0 + scale) + shift
    hb = h.astype(BF16)
    acc_ref[...] = jnp.zeros_like(acc_ref)

    def chunk(c, carry):
        gu = _dot(hb, wgu_ref[c])
        g = gu[:, :FF_CHUNK]
        u = gu[:, FF_CHUNK:]
        a = (g / (1.0 + jnp.exp(-g)) * u).astype(BF16)
        acc_ref[...] += _dot(a, wd_ref[c])
        return carry

    lax.fori_loop(0, N_FF_CHUNKS, chunk, 0)
    y = _rms(acc_ref[...], g_ref[g_row + 1:g_row + 2, :])
    o_ref[...] = x + FFN_RESIDUAL_WEIGHT * gate * y


def _ffn(x, mods, norm_g, wgu, wd, *, layer, half, mod_row_of_tile, tm):
    n = x.shape[0]
    sub = 2 * half
    g_row = 4 * half
    return pl.pallas_call(
        functools.partial(_ffn_kernel, sub=sub, g_row=g_row),
        out_shape=jax.ShapeDtypeStruct((n, D_MODEL), F32),
        grid=(n // tm,),
        in_specs=[
            pl.BlockSpec((tm, D_MODEL), lambda i: (i, 0)),
            pl.BlockSpec((None, None, N_MOD, D_MODEL), lambda i: (layer, mod_row_of_tile(i), 0, 0)),
            pl.BlockSpec((None, 6, D_MODEL), lambda i: (layer, 0, 0)),
            _resident((None, None, N_FF_CHUNKS, D_MODEL, 2 * FF_CHUNK), lambda i: (layer, half, 0, 0, 0)),
            _resident((None, None, N_FF_CHUNKS, FF_CHUNK, D_MODEL), lambda i: (layer, half, 0, 0, 0)),
        ],
        out_specs=pl.BlockSpec((tm, D_MODEL), lambda i: (i, 0)),
        scratch_shapes=[pltpu.VMEM((tm, D_MODEL), F32)],
        compiler_params=_params(("arbitrary",)),
        name="ffn",
    )(x, mods, norm_g, wgu, wd)


def _rope(x, cos, sin_signed, lo_half):
    partner = jnp.where(lo_half, pltpu.roll(x, LANES - ROT_FREQS, 1), pltpu.roll(x, ROT_FREQS, 1))
    return x * cos + partner * sin_signed


def _mixer_in_kernel(*refs, rotary):
    if rotary:
        (x_ref, mod_ref, g_ref, win_ref, cdft_ref, cos_ref, sin_ref,
         q_ref, k_ref, v_ref, gb_ref, u_ref, xcs_ref) = refs
    else:
        (x_ref, mod_ref, g_ref, win_ref, cdft_ref,
         q_ref, k_ref, v_ref, gb_ref, u_ref, xcs_ref) = refs
    x = x_ref[...]
    shift = mod_ref[3:4, :]
    scale = mod_ref[4:5, :]
    h = _rms(x, g_ref[2:3, :]) * (1.0 + scale) + shift
    hb = h.astype(BF16)

    q = _dot(hb, win_ref[:, 0:ATT_DIM]) * (QK_DIM ** -0.5)
    k = _dot(hb, win_ref[:, ATT_DIM:2 * ATT_DIM])
    if rotary:
        cos = cos_ref[...]
        sin = sin_ref[...]
        lane = lax.broadcasted_iota(jnp.int32, (1, LANES), 1)
        lo_half = (lane % (2 * ROT_FREQS)) < ROT_FREQS
        for hd in range(N_HEADS):
            sl = slice(hd * V_DIM, (hd + 1) * V_DIM)
            q_ref[:, sl] = _rope(q[:, sl], cos, sin, lo_half).astype(q_ref.dtype)
            k_ref[:, sl] = _rope(k[:, sl], cos, sin, lo_half).astype(k_ref.dtype)
    else:
        q_ref[...] = q.astype(q_ref.dtype)
        k_ref[...] = k.astype(k_ref.dtype)
    v_ref[...] = _dot(hb, win_ref[:, 2 * ATT_DIM:3 * ATT_DIM]).astype(v_ref.dtype)

    c0 = 3 * ATT_DIM
    gates = _dot(hb, win_ref[:, c0:c0 + 3 * CONV_DIM])
    gb_ref[...] = gates[:, 0:CONV_DIM]
    u_ref[...] = gates[:, CONV_DIM:2 * CONV_DIM] * gates[:, 2 * CONV_DIM:3 * CONV_DIM]
    f = _dot(hb, win_ref[:, c0 + 3 * CONV_DIM:IN_DIM])
    xcs_ref[...] = _dot(f.astype(BF16), cdft_ref[...]).astype(BF16)


def _mixer_in(x, mods, norm_g, win, cdft, rope_tabs, *, layer, mod_row_of_tile, tm, kv_dtype):
    n = x.shape[0]
    rotary = rope_tabs is not None
    in_specs = [
        pl.BlockSpec((tm, D_MODEL), lambda i: (i, 0)),
        pl.BlockSpec((None, None, N_MOD, D_MODEL), lambda i: (layer, mod_row_of_tile(i), 0, 0)),
        pl.BlockSpec((None, 6, D_MODEL), lambda i: (layer, 0, 0)),
        _resident((None, D_MODEL, IN_DIM), lambda i: (layer, 0, 0)),
        _resident((FOURIER_DIM, 2 * FOURIER_DIM), lambda i: (0, 0)),
    ]
    args = [x, mods, norm_g, win, cdft]
    if rotary:
        tiles_per_seq = rope_tabs[0].shape[0] // tm
        for tab in rope_tabs:
            in_specs.append(pl.BlockSpec((tm, LANES), lambda i: (i % tiles_per_seq, 0)))
            args.append(tab)

    def rows(width):
        return pl.BlockSpec((tm, width), lambda i: (i, 0))

    return pl.pallas_call(
        functools.partial(_mixer_in_kernel, rotary=rotary),
        out_shape=[
            jax.ShapeDtypeStruct((n, ATT_DIM), BF16),
            jax.ShapeDtypeStruct((n, ATT_DIM), kv_dtype),
            jax.ShapeDtypeStruct((n, ATT_DIM), kv_dtype),
            jax.ShapeDtypeStruct((n, CONV_DIM), F32),
            jax.ShapeDtypeStruct((n, CONV_DIM), F32),
            jax.ShapeDtypeStruct((n, 2 * FOURIER_DIM), BF16),
        ],
        grid=(n // tm,),
        in_specs=in_specs,
        out_specs=[rows(ATT_DIM), rows(ATT_DIM), rows(ATT_DIM), rows(CONV_DIM), rows(CONV_DIM),
                   rows(2 * FOURIER_DIM)],
        compiler_params=_params(("arbitrary",)),
        name="mixer_in",
    )(*args)


def _lambda(lam_ref, lambda_init):
    lq = lam_ref[...]
    s01 = jnp.sum(lq[0:1, :] * lq[1:2, :], axis=-1, keepdims=True)
    s23 = jnp.sum(lq[2:3, :] * lq[3:4, :], axis=-1, keepdims=True)
    return jnp.exp(s01) - jnp.exp(s23) + lambda_init


def _attention_kernel(*refs, has_ctx, lambda_init):
    if has_ctx:
        q_ref, k_ref, v_ref, ck_ref, cv_ref, lam_ref, sg_ref, o_ref = refs
    else:
        q_ref, k_ref, v_ref, lam_ref, sg_ref, o_ref = refs
    lam = _lambda(lam_ref, lambda_init)
    lane = lax.broadcasted_iota(jnp.int32, (1, V_DIM), 1)
    first_map = lane < QK_DIM
    sub_g = sg_ref[...] * (1.0 - lambda_init)
    nt = (((1,), (1,)), ((), ()))
    for hd in range(N_HEADS):
        sl = slice(hd * V_DIM, (hd + 1) * V_DIM)
        q = q_ref[:, sl]
        zero = jnp.zeros_like(q)
        k_sets = [k_ref[:, sl].astype(BF16)]
        v_sets = [v_ref[:, sl].astype(BF16)]
        if has_ctx:
            k_sets.append(ck_ref[:, sl].astype(BF16))
            v_sets.append(cv_ref[:, sl].astype(BF16))
        probs = []
        for keep in (first_map, jnp.logical_not(first_map)):
            qm = jnp.where(keep, q, zero)
            s = [lax.dot_general(qm, ks, nt, preferred_element_type=F32) for ks in k_sets]
            m = functools.reduce(jnp.maximum, [jnp.max(t, axis=-1, keepdims=True) for t in s])
            e = [jnp.exp(t - m) for t in s]
            den = functools.reduce(jnp.add, [jnp.sum(t, axis=-1, keepdims=True) for t in e])
            probs.append((e, 1.0 / den))
        (e0, r0), (e1, r1) = probs
        r1 = lam * r1
        o = functools.reduce(jnp.add, [
            _dot((a * r0 - b * r1).astype(BF16), vs) for a, b, vs in zip(e0, e1, v_sets)])
        o_ref[:, sl] = (_rms(o, sub_g)).astype(o_ref.dtype)


def _attention(q, k, v, ctx_k, ctx_v, lam_qk, subln_g, *, layer, tq):
    b, n, _ = q.shape
    has_ctx = ctx_k is not None
    in_specs = [
        pl.BlockSpec((None, tq, ATT_DIM), lambda i, j: (i, j, 0)),
        pl.BlockSpec((None, n, ATT_DIM), lambda i, j: (i, 0, 0)),
        pl.BlockSpec((None, n, ATT_DIM), lambda i, j: (i, 0, 0)),
    ]
    args = [q, k, v]
    if has_ctx:
        past = ctx_k.shape[2]
        for arr in (ctx_k, ctx_v):
            in_specs.append(pl.BlockSpec((None, None, past, ATT_DIM), lambda i, j: (i, layer, 0, 0)))
            args.append(arr)
    in_specs += [
        pl.BlockSpec((None, 4, QK_DIM), lambda i, j: (layer, 0, 0)),
        pl.BlockSpec((None, 1, V_DIM), lambda i, j: (layer, 0, 0)),
    ]
    args += [lam_qk, subln_g.reshape(DEPTH, 1, V_DIM)]
    return pl.pallas_call(
        functools.partial(_attention_kernel, has_ctx=has_ctx, lambda_init=_lambda_init(layer)),
        out_shape=jax.ShapeDtypeStruct((b, n, ATT_DIM), BF16),
        grid=(b, n // tq),
        in_specs=in_specs,
        out_specs=pl.BlockSpec((None, tq, ATT_DIM), lambda i, j: (i, j, 0)),
        compiler_params=_params(("arbitrary", "arbitrary")),
        name="attention",
    )(*args)


def _mixer_out_kernel(x_ref, att_ref, gb_ref, u_ref, xcs_ref, dft_ref, cw_ref, mod_ref, g_ref, wout_ref,
                      o_ref, *, seqs, tr, n):
    r0 = pl.multiple_of(pl.program_id(1) * tr, tr)
    gate = mod_ref[5:6, :]
    cw = cw_ref[...]
    row = lax.broadcasted_iota(jnp.int32, (tr, 1), 0)
    four_scale = (n * FOURIER_GROUP_DIM) ** -0.5
    for s in range(seqs):
        four = (_dot(dft_ref[:, 0:n], xcs_ref[s, :, 0:FOURIER_DIM])
                + _dot(dft_ref[:, n:2 * n], xcs_ref[s, :, FOURIER_DIM:2 * FOURIER_DIM])) * four_scale
        u = u_ref[s, pl.ds(r0, tr), :]
        before_start = pl.multiple_of(jnp.maximum(r0 - SUBLANES, 0), SUBLANES)
        after_start = pl.multiple_of(jnp.minimum(r0 + tr, n - SUBLANES), SUBLANES)
        before = u_ref[s, pl.ds(before_start, SUBLANES), :][SUBLANES - 1:SUBLANES, :]
        after = u_ref[s, pl.ds(after_start, SUBLANES), :][0:1, :]
        before = jnp.where(r0 > 0, before, 0.0)
        after = jnp.where(r0 + tr < n, after, 0.0)
        prev = jnp.where(row == 0, before, pltpu.roll(u, 1, 0))
        nxt = jnp.where(row == tr - 1, after, pltpu.roll(u, tr - 1, 0))
        conv = gb_ref[s] * (prev * cw[0:1, :] + u * cw[1:2, :] + nxt * cw[2:3, :])
        y = (_dot(att_ref[s], wout_ref[0:ATT_DIM, :])
             + _dot(conv.astype(BF16), wout_ref[ATT_DIM:ATT_DIM + CONV_DIM, :])
             + _dot(four.astype(BF16), wout_ref[ATT_DIM + CONV_DIM:D_MODEL, :]))
        o_ref[s] = x_ref[s] + gate * _rms(y, g_ref[3:4, :])


def _mixer_out(x, att, gb, u, xcs, dft, conv_w, mods, norm_g, wout, *, layer, mod_row_of_batch, seqs, tr):
    b, n, _ = x.shape

    def tile(width):
        return pl.BlockSpec((seqs, tr, width), lambda i, j: (i, j, 0))

    def whole(width):
        return pl.BlockSpec((seqs, n, width), lambda i, j: (i, 0, 0))

    return pl.pallas_call(
        functools.partial(_mixer_out_kernel, seqs=seqs, tr=tr, n=n),
        out_shape=jax.ShapeDtypeStruct((b, n, D_MODEL), F32),
        grid=(b // seqs, n // tr),
        in_specs=[
            tile(D_MODEL), tile(ATT_DIM), tile(CONV_DIM), whole(CONV_DIM), whole(2 * FOURIER_DIM),
            pl.BlockSpec((tr, 2 * n), lambda i, j: (j, 0)),
            pl.BlockSpec((None, 3, CONV_DIM), lambda i, j: (layer, 0, 0)),
            pl.BlockSpec((None, None, N_MOD, D_MODEL), lambda i, j: (layer, mod_row_of_batch(i), 0, 0)),
            pl.BlockSpec((None, 6, D_MODEL), lambda i, j: (layer, 0, 0)),
            _resident((None, D_MODEL, D_MODEL), lambda i, j: (layer, 0, 0)),
        ],
        out_specs=tile(D_MODEL),
        compiler_params=_params(("arbitrary", "arbitrary")),
        name="mixer_out",
    )(x, att, gb, u, xcs, dft, conv_w, mods, norm_g, wout)


def _position_dft(n):
    jk = (np.arange(n, dtype=np.int64)[:, None] * np.arange(n, dtype=np.int64)[None, :]) % n
    ang = 2.0 * np.pi * jk.astype(np.float64) / n
    return np.concatenate([np.cos(ang), -np.sin(ang)], axis=1).astype(np.float32)


def _channel_dft():
    g = FOURIER_GROUP_DIM
    jk = (np.arange(g)[:, None] * np.arange(g)[None, :]) % g
    ang = 2.0 * np.pi * jk.astype(np.float64) / g
    eye = np.eye(FOURIER_DIM // g)
    return np.concatenate([np.kron(eye, np.cos(ang)), np.kron(eye, np.sin(ang))], axis=1).astype(np.float32)


def _rope_tables(n):
    t = jnp.arange(n)
    row = (t // GRID_W).astype(F32)
    col = (t % GRID_W).astype(F32)
    inv = 1.0 / (ROPE_THETA ** (jnp.arange(ROT_FREQS, dtype=F32) / ROT_FREQS))
    ang_r = row[:, None] * inv
    ang_c = col[:, None] * inv
    cos = jnp.concatenate([jnp.cos(ang_r)] * 2 + [jnp.cos(ang_c)] * 2, axis=-1)
    sin = jnp.concatenate([-jnp.sin(ang_r), jnp.sin(ang_r), -jnp.sin(ang_c), jnp.sin(ang_c)], axis=-1)
    return jnp.tile(cos, (1, 2)), jnp.tile(sin, (1, 2))


def _run_path(x, mods, mod_row, weights, ctx, rope_tabs, dft, *, tm, tq, seqs, tr, kv_dtype):
    norm_g, wgu, wd, win, wout, cdft, conv_w, lam_qk, subln_g = weights
    b, n, _ = x.shape
    tiles_per_seq = max(n // tm, 1)
    seqs_per_tile = max(tm // n, 1)

    def mod_row_of_tile(i):
        return mod_row(i * seqs_per_tile // tiles_per_seq)

    ks, vs = [], []
    for l in range(DEPTH):
        flat = x.reshape(b * n, D_MODEL)
        flat = _ffn(flat, mods, norm_g, wgu, wd, layer=l, half=0, mod_row_of_tile=mod_row_of_tile, tm=tm)
        q, k, v, gb, u, xcs = _mixer_in(flat, mods, norm_g, win, cdft, rope_tabs, layer=l,
                                        mod_row_of_tile=mod_row_of_tile, tm=tm, kv_dtype=kv_dtype)
        shape3 = lambda a: a.reshape(b, n, a.shape[-1])
        ctx_k, ctx_v = ctx if ctx is not None else (None, None)
        att = _attention(shape3(q), shape3(k), shape3(v), ctx_k, ctx_v, lam_qk, subln_g, layer=l, tq=tq)
        x = _mixer_out(shape3(flat), att, shape3(gb), shape3(u), shape3(xcs), dft, conv_w, mods, norm_g,
                       wout, layer=l, mod_row_of_batch=lambda i: mod_row(i * seqs), seqs=seqs, tr=tr)
        flat = _ffn(x.reshape(b * n, D_MODEL), mods, norm_g, wgu, wd, layer=l, half=1,
                    mod_row_of_tile=mod_row_of_tile, tm=tm)
        x = flat.reshape(b, n, D_MODEL)
        ks.append(shape3(k))
        vs.append(shape3(v))
    return x, ks, vs


def kernel(x_prompt, x_sample, cache_k, cache_v, c, c_ctx, w_mod, b_mod, norm_g, w_ffn_gu, w_ffn_down,
           w_in, w_out, conv_w, lam_qk, subln_g):
    batch, seq, _ = x_prompt.shape
    dec_batch, dec_seq, _ = x_sample.shape
    past = cache_k.shape[2]

    cvecs = jnp.zeros((MOD_ROWS, D_MODEL), F32).at[0].set(c_ctx).at[1:1 + dec_batch].set(c)
    mods = _modulation(cvecs, w_mod, b_mod)

    wgu = w_ffn_gu.reshape(DEPTH, 2, D_MODEL, 2, N_FF_CHUNKS, FF_CHUNK)
    wgu = wgu.transpose(0, 1, 4, 2, 3, 5).reshape(DEPTH, 2, N_FF_CHUNKS, D_MODEL, 2 * FF_CHUNK).astype(BF16)
    wd = w_ffn_down.reshape(DEPTH, 2, N_FF_CHUNKS, FF_CHUNK, D_MODEL).astype(BF16)
    win = w_in.astype(BF16)
    wout = w_out.astype(BF16)
    cdft = jnp.asarray(_channel_dft()).astype(BF16)
    weights = (norm_g, wgu, wd, win, wout, cdft, conv_w, lam_qk, subln_g)

    dft_ctx = jnp.asarray(_position_dft(seq)).astype(BF16)
    y_prompt, ks, vs = _run_path(x_prompt, mods, lambda i: 0, weights, None, None, dft_ctx,
                                 tm=1024, tq=seq, seqs=4, tr=seq, kv_dtype=F32)
    new_cache_k = jnp.stack(ks, axis=1).reshape(batch, DEPTH, seq, N_HEADS, 2, QK_DIM)
    new_cache_v = jnp.stack(vs, axis=1).reshape(batch, DEPTH, seq, N_HEADS, V_DIM)

    dft_lat = jnp.asarray(_position_dft(dec_seq)).astype(BF16)
    ctx = (cache_k.reshape(dec_batch, DEPTH, past, ATT_DIM), cache_v.reshape(dec_batch, DEPTH, past, ATT_DIM))
    y_sample, _, _ = _run_path(x_sample, mods, lambda i: 1 + i, weights, ctx, _rope_tables(dec_seq), dft_lat,
                               tm=1024, tq=256, seqs=1, tr=512, kv_dtype=BF16)
    return (y_prompt, y_sample, new_cache_k, new_cache_v)
```

```python
import functools
import math

import jax
import jax.numpy as jnp
import numpy as np
from jax import lax
from jax.experimental import pallas as pl
from jax.experimental.pallas import tpu as pltpu

D_MODEL = 1024
DEPTH = 2
GRID_W = 64
V_DIM = 128
QK_DIM = V_DIM // 2
N_HEADS = 4
ATT_DIM = N_HEADS * V_DIM
CONV_DIM = 256
FOURIER_DIM = 256
FOURIER_GROUP_DIM = 64
IN_DIM = 3 * ATT_DIM + 3 * CONV_DIM + FOURIER_DIM
D_FF = 2816
N_MOD = 9
ROT_FREQS = 16
ROPE_THETA = 10000.0
EPS = 1e-6
FFN_RESIDUAL_WEIGHT = 0.5

FF_CHUNK = 256
N_FF_CHUNKS = D_FF // FF_CHUNK
MOD_ROWS = 8
SUBLANES = 8
LANES = 128
VMEM_LIMIT = 56 * 1024 * 1024

F32 = jnp.float32
BF16 = jnp.bfloat16


def _lambda_init(layer_idx):
    return 0.8 - 0.6 * math.exp(-0.3 * layer_idx)


def _rms(x, g):
    return x * lax.rsqrt(jnp.mean(x * x, axis=-1, keepdims=True) + EPS) * g


def _dot(a, b):
    return jnp.dot(a, b, preferred_element_type=F32)


def _params(semantics):
    return pltpu.CompilerParams(dimension_semantics=semantics, vmem_limit_bytes=VMEM_LIMIT)


def _resident(block_shape, index_map):
    return pl.BlockSpec(block_shape, index_map, pipeline_mode=pl.Buffered(1))


def _cast_kernel(x_ref, o_ref):
    o_ref[...] = x_ref[...].astype(o_ref.dtype)


def _cast_bf16(x, block_rows):
    rows, cols = x.shape
    return pl.pallas_call(
        _cast_kernel,
        out_shape=jax.ShapeDtypeStruct((rows, cols), BF16),
        grid=(rows // block_rows,),
        in_specs=[pl.BlockSpec((block_rows, cols), lambda i: (i, 0))],
        out_specs=pl.BlockSpec((block_rows, cols), lambda i: (i, 0)),
        compiler_params=_params(("arbitrary",)),
        name="cast_bf16",
    )(x)


def _regroup_gu_kernel(g_ref, u_ref, o_ref):
    for c in range(N_FF_CHUNKS):
        cols = slice(c * FF_CHUNK, (c + 1) * FF_CHUNK)
        o_ref[c, :, 0:FF_CHUNK] = g_ref[:, cols].astype(BF16)
        o_ref[c, :, FF_CHUNK:2 * FF_CHUNK] = u_ref[:, cols].astype(BF16)


def _regroup_gu(w_gu, block_rows=256):
    def half(part):
        return pl.BlockSpec((None, None, block_rows, D_FF), lambda l, s, r: (l, s, r, part))

    return pl.pallas_call(
        _regroup_gu_kernel,
        out_shape=jax.ShapeDtypeStruct((DEPTH, 2, N_FF_CHUNKS, D_MODEL, 2 * FF_CHUNK), BF16),
        grid=(DEPTH, 2, D_MODEL // block_rows),
        in_specs=[half(0), half(1)],
        out_specs=pl.BlockSpec((None, None, N_FF_CHUNKS, block_rows, 2 * FF_CHUNK),
                               lambda l, s, r: (l, s, 0, r, 0)),
        compiler_params=_params(("arbitrary",) * 3),
        name="regroup_gu",
    )(w_gu, w_gu)


def _mod_kernel(c_ref, w_ref, b_ref, o_ref):
    c = c_ref[...]
    s = (c / (1.0 + jnp.exp(-c))).astype(BF16)
    o_ref[...] = _dot(s, w_ref[...].astype(BF16)) + b_ref[...]


def _modulation(cvecs, w_mod, b_mod):
    col = D_MODEL
    out = pl.pallas_call(
        _mod_kernel,
        out_shape=jax.ShapeDtypeStruct((DEPTH, MOD_ROWS, N_MOD * D_MODEL), F32),
        grid=(DEPTH, N_MOD * D_MODEL // col),
        in_specs=[
            pl.BlockSpec((MOD_ROWS, D_MODEL), lambda l, j: (0, 0)),
            pl.BlockSpec((None, D_MODEL, col), lambda l, j: (l, 0, j)),
            pl.BlockSpec((None, 1, col), lambda l, j: (l, 0, j)),
        ],
        out_specs=pl.BlockSpec((None, MOD_ROWS, col), lambda l, j: (l, 0, j)),
        compiler_params=_params(("arbitrary", "arbitrary")),
        name="modulation",
    )(cvecs, w_mod, b_mod.reshape(DEPTH, 1, N_MOD * D_MODEL))
    return out.reshape(DEPTH, MOD_ROWS, N_MOD, D_MODEL)


def _ffn_kernel(x_ref, mod_ref, g_ref, wgu_ref, wd_ref, o_ref, acc_ref, *, sub, g_row):
    x = x_ref[...]
    shift = mod_ref[3 * sub:3 * sub + 1, :]
    scale = mod_ref[3 * sub + 1:3 * sub + 2, :]
    gate = mod_ref[3 * sub + 2:3 * sub + 3, :]
    h = _rms(x, g_ref[g_row:g_row + 1, :]) * (1.0 + scale) + shift
    hb = h.astype(BF16)
    acc_ref[...] = jnp.zeros_like(acc_ref)

    def chunk(c, carry):
        gu = _dot(hb, wgu_ref[c])
        g = gu[:, :FF_CHUNK]
        u = gu[:, FF_CHUNK:]
        a = (g / (1.0 + jnp.exp(-g)) * u).astype(BF16)
        acc_ref[...] += _dot(a, wd_ref[c])
        return carry

    lax.fori_loop(0, N_FF_CHUNKS, chunk, 0)
    y = _rms(acc_ref[...], g_ref[g_row + 1:g_row + 2, :])
    o_ref[...] = x + FFN_RESIDUAL_WEIGHT * gate * y


def _ffn(x, mods, norm_g, wgu, wd, *, layer, half, mod_row_of_tile, tm):
    n = x.shape[0]
    sub = 2 * half
    g_row = 4 * half
    return pl.pallas_call(
        functools.partial(_ffn_kernel, sub=sub, g_row=g_row),
        out_shape=jax.ShapeDtypeStruct((n, D_MODEL), F32),
        grid=(n // tm,),
        in_specs=[
            pl.BlockSpec((tm, D_MODEL), lambda i: (i, 0)),
            pl.BlockSpec((None, None, N_MOD, D_MODEL), lambda i: (layer, mod_row_of_tile(i), 0, 0)),
            pl.BlockSpec((None, 6, D_MODEL), lambda i: (layer, 0, 0)),
            _resident((None, None, N_FF_CHUNKS, D_MODEL, 2 * FF_CHUNK), lambda i: (layer, half, 0, 0, 0)),
            _resident((None, None, N_FF_CHUNKS, FF_CHUNK, D_MODEL), lambda i: (layer, half, 0, 0, 0)),
        ],
        out_specs=pl.BlockSpec((tm, D_MODEL), lambda i: (i, 0)),
        scratch_shapes=[pltpu.VMEM((tm, D_MODEL), F32)],
        compiler_params=_params(("arbitrary",)),
        name="ffn",
    )(x, mods, norm_g, wgu, wd)


def _rope(x, cos, sin_signed, lo_half):
    partner = jnp.where(lo_half, pltpu.roll(x, LANES - ROT_FREQS, 1), pltpu.roll(x, ROT_FREQS, 1))
    return x * cos + partner * sin_signed


def _mixer_in_kernel(*refs, rotary, seq):
    x_ref, mod_ref, g_ref, win_ref, cdft_ref = refs[:5]
    if rotary:
        cos_ref, sin_ref = refs[5:7]
        q_ref, k_ref, v_ref, gb_ref, u_ref, xcs_ref = refs[7:]
    else:
        q_ref, k_ref, v_ref, gb_ref, u_ref, xcs_ref, kc_ref, vc_ref = refs[-8:]
    x = x_ref[...]
    shift = mod_ref[3:4, :]
    scale = mod_ref[4:5, :]
    h = _rms(x, g_ref[2:3, :]) * (1.0 + scale) + shift
    hb = h.astype(BF16)

    q = _dot(hb, win_ref[:, 0:ATT_DIM]) * (QK_DIM ** -0.5)
    k = _dot(hb, win_ref[:, ATT_DIM:2 * ATT_DIM])
    if rotary:
        cos = cos_ref[...]
        sin = sin_ref[...]
        lane = lax.broadcasted_iota(jnp.int32, (1, LANES), 1)
        lo_half = (lane % (2 * ROT_FREQS)) < ROT_FREQS
        for hd in range(N_HEADS):
            sl = slice(hd * V_DIM, (hd + 1) * V_DIM)
            q_ref[:, sl] = _rope(q[:, sl], cos, sin, lo_half).astype(q_ref.dtype)
            k_ref[:, sl] = _rope(k[:, sl], cos, sin, lo_half).astype(k_ref.dtype)
    else:
        q_ref[...] = q.astype(q_ref.dtype)
        k_ref[...] = k.astype(k_ref.dtype)
    v = _dot(hb, win_ref[:, 2 * ATT_DIM:3 * ATT_DIM])
    v_ref[...] = v.astype(v_ref.dtype)
    if not rotary:
        for s in range(x.shape[0] // seq):
            rows = slice(s * seq, (s + 1) * seq)
            kc_ref[s] = k[rows, :].T
            for hd in range(N_HEADS):
                vc_ref[s, pl.ds(hd, seq, stride=N_HEADS), :] = v[rows, hd * V_DIM:(hd + 1) * V_DIM]

    c0 = 3 * ATT_DIM
    gates = _dot(hb, win_ref[:, c0:c0 + 3 * CONV_DIM])
    gb_ref[...] = gates[:, 0:CONV_DIM]
    u_ref[...] = gates[:, CONV_DIM:2 * CONV_DIM] * gates[:, 2 * CONV_DIM:3 * CONV_DIM]
    f = _dot(hb, win_ref[:, c0 + 3 * CONV_DIM:IN_DIM])
    xcs_ref[...] = _dot(f.astype(BF16), cdft_ref[...]).astype(BF16)


def _mixer_in(x, mods, norm_g, win, cdft, rope_tabs, caches, *, layer, mod_row_of_tile, tm, seq):
    n = x.shape[0]
    rotary = rope_tabs is not None
    in_specs = [
        pl.BlockSpec((tm, D_MODEL), lambda i: (i, 0)),
        pl.BlockSpec((None, None, N_MOD, D_MODEL), lambda i: (layer, mod_row_of_tile(i), 0, 0)),
        pl.BlockSpec((None, 6, D_MODEL), lambda i: (layer, 0, 0)),
        _resident((None, D_MODEL, IN_DIM), lambda i: (layer, 0, 0)),
        _resident((FOURIER_DIM, 2 * FOURIER_DIM), lambda i: (0, 0)),
    ]
    args = [x, mods, norm_g, win, cdft]
    if rotary:
        tiles_per_seq = rope_tabs[0].shape[0] // tm
        for tab in rope_tabs:
            in_specs.append(pl.BlockSpec((tm, LANES), lambda i: (i % tiles_per_seq, 0)))
            args.append(tab)

    def rows(width):
        return pl.BlockSpec((tm, width), lambda i: (i, 0))

    out_shape = [
        jax.ShapeDtypeStruct((n, ATT_DIM), BF16),
        jax.ShapeDtypeStruct((n, ATT_DIM), BF16),
        jax.ShapeDtypeStruct((n, ATT_DIM), BF16),
        jax.ShapeDtypeStruct((n, CONV_DIM), F32),
        jax.ShapeDtypeStruct((n, CONV_DIM), F32),
        jax.ShapeDtypeStruct((n, 2 * FOURIER_DIM), BF16),
    ]
    out_specs = [rows(ATT_DIM), rows(ATT_DIM), rows(ATT_DIM), rows(CONV_DIM), rows(CONV_DIM),
                 rows(2 * FOURIER_DIM)]
    aliases = {}
    if not rotary:
        batch, seqs = n // seq, tm // seq
        out_shape += [jax.ShapeDtypeStruct((batch, DEPTH, ATT_DIM, seq), F32),
                      jax.ShapeDtypeStruct((batch, DEPTH, N_HEADS * seq, V_DIM), F32)]
        out_specs += [pl.BlockSpec((seqs, None, ATT_DIM, seq), lambda i: (i, layer, 0, 0)),
                      pl.BlockSpec((seqs, None, N_HEADS * seq, V_DIM), lambda i: (i, layer, 0, 0))]
        if caches is not None:
            first_cache_out = len(out_shape) - 2
            for offset, arr in enumerate(caches):
                aliases[len(args)] = first_cache_out + offset
                in_specs.append(pl.BlockSpec(memory_space=pl.ANY))
                args.append(arr)

    return pl.pallas_call(
        functools.partial(_mixer_in_kernel, rotary=rotary, seq=seq),
        out_shape=out_shape,
        grid=(n // tm,),
        in_specs=in_specs,
        out_specs=out_specs,
        input_output_aliases=aliases,
        compiler_params=_params(("arbitrary",)),
        name="mixer_in",
    )(*args)


def _lambda(lam_ref, lambda_init):
    lq = lam_ref[...]
    s01 = jnp.sum(lq[0:1, :] * lq[1:2, :], axis=-1, keepdims=True)
    s23 = jnp.sum(lq[2:3, :] * lq[3:4, :], axis=-1, keepdims=True)
    return jnp.exp(s01) - jnp.exp(s23) + lambda_init


def _attention_kernel(*refs, has_ctx, lambda_init):
    if has_ctx:
        q_ref, k_ref, v_ref, ck_ref, cv_ref, lam_ref, sg_ref, o_ref = refs
    else:
        q_ref, k_ref, v_ref, lam_ref, sg_ref, o_ref = refs
    lam = _lambda(lam_ref, lambda_init)
    lane = lax.broadcasted_iota(jnp.int32, (1, V_DIM), 1)
    first_map = lane < QK_DIM
    sub_g = sg_ref[...] * (1.0 - lambda_init)
    nt = (((1,), (1,)), ((), ()))
    for hd in range(N_HEADS):
        sl = slice(hd * V_DIM, (hd + 1) * V_DIM)
        q = q_ref[:, sl]
        zero = jnp.zeros_like(q)
        k_own = k_ref[:, sl]
        v_sets = [v_ref[:, sl]]
        if has_ctx:
            past = ck_ref.shape[1]
            k_ctx_t = ck_ref[sl, :].astype(BF16)
            v_sets.append(cv_ref[pl.ds(hd, past, stride=N_HEADS), :].astype(BF16))
        probs = []
        for keep in (first_map, jnp.logical_not(first_map)):
            qm = jnp.where(keep, q, zero)
            s = [lax.dot_general(qm, k_own, nt, preferred_element_type=F32)]
            if has_ctx:
                s.append(_dot(qm, k_ctx_t))
            m = functools.reduce(jnp.maximum, [jnp.max(t, axis=-1, keepdims=True) for t in s])
            e = [jnp.exp(t - m) for t in s]
            den = functools.reduce(jnp.add, [jnp.sum(t, axis=-1, keepdims=True) for t in e])
            probs.append((e, 1.0 / den))
        (e0, r0), (e1, r1) = probs
        r1 = lam * r1
        o = functools.reduce(jnp.add, [
            _dot((a * r0 - b * r1).astype(BF16), vs) for a, b, vs in zip(e0, e1, v_sets)])
        o_ref[:, sl] = (_rms(o, sub_g)).astype(o_ref.dtype)


def _attention(q, k, v, ctx_k, ctx_v, lam_qk, subln_g, *, layer, tq):
    b, n, _ = q.shape
    has_ctx = ctx_k is not None
    in_specs = [
        pl.BlockSpec((None, tq, ATT_DIM), lambda i, j: (i, j, 0)),
        pl.BlockSpec((None, n, ATT_DIM), lambda i, j: (i, 0, 0)),
        pl.BlockSpec((None, n, ATT_DIM), lambda i, j: (i, 0, 0)),
    ]
    args = [q, k, v]
    if has_ctx:
        for arr in (ctx_k, ctx_v):
            in_specs.append(pl.BlockSpec((None, None) + arr.shape[2:], lambda i, j: (i, layer, 0, 0)))
            args.append(arr)
    in_specs += [
        pl.BlockSpec((None, 4, QK_DIM), lambda i, j: (layer, 0, 0)),
        pl.BlockSpec((None, 1, V_DIM), lambda i, j: (layer, 0, 0)),
    ]
    args += [lam_qk, subln_g.reshape(DEPTH, 1, V_DIM)]
    return pl.pallas_call(
        functools.partial(_attention_kernel, has_ctx=has_ctx, lambda_init=_lambda_init(layer)),
        out_shape=jax.ShapeDtypeStruct((b, n, ATT_DIM), BF16),
        grid=(b, n // tq),
        in_specs=in_specs,
        out_specs=pl.BlockSpec((None, tq, ATT_DIM), lambda i, j: (i, j, 0)),
        compiler_params=_params(("arbitrary", "arbitrary")),
        name="attention",
    )(*args)


def _mixer_out_kernel(x_ref, att_ref, gb_ref, u_ref, xcs_ref, dft_ref, cw_ref, mod_ref, g_ref, wout_ref,
                      o_ref, *, seqs, tr, n):
    r0 = pl.multiple_of(pl.program_id(1) * tr, tr)
    gate = mod_ref[5:6, :]
    cw = cw_ref[...]
    row = lax.broadcasted_iota(jnp.int32, (tr, 1), 0)
    four_scale = (n * FOURIER_GROUP_DIM) ** -0.5
    for s in range(seqs):
        four = (_dot(dft_ref[:, 0:n], xcs_ref[s, :, 0:FOURIER_DIM])
                + _dot(dft_ref[:, n:2 * n], xcs_ref[s, :, FOURIER_DIM:2 * FOURIER_DIM])) * four_scale
        u = u_ref[s, pl.ds(r0, tr), :]
        before_start = pl.multiple_of(jnp.maximum(r0 - SUBLANES, 0), SUBLANES)
        after_start = pl.multiple_of(jnp.minimum(r0 + tr, n - SUBLANES), SUBLANES)
        before = u_ref[s, pl.ds(before_start, SUBLANES), :][SUBLANES - 1:SUBLANES, :]
        after = u_ref[s, pl.ds(after_start, SUBLANES), :][0:1, :]
        before = jnp.where(r0 > 0, before, 0.0)
        after = jnp.where(r0 + tr < n, after, 0.0)
        prev = jnp.where(row == 0, before, pltpu.roll(u, 1, 0))
        nxt = jnp.where(row == tr - 1, after, pltpu.roll(u, tr - 1, 0))
        conv = gb_ref[s] * (prev * cw[0:1, :] + u * cw[1:2, :] + nxt * cw[2:3, :])
        y = (_dot(att_ref[s], wout_ref[0:ATT_DIM, :])
             + _dot(conv.astype(BF16), wout_ref[ATT_DIM:ATT_DIM + CONV_DIM, :])
             + _dot(four.astype(BF16), wout_ref[ATT_DIM + CONV_DIM:D_MODEL, :]))
        o_ref[s] = x_ref[s] + gate * _rms(y, g_ref[3:4, :])


def _mixer_out(x, att, gb, u, xcs, dft, conv_w, mods, norm_g, wout, *, layer, mod_row_of_batch, seqs, tr):
    b, n, _ = x.shape

    def tile(width):
        return pl.BlockSpec((seqs, tr, width), lambda i, j: (i, j, 0))

    def whole(width):
        return pl.BlockSpec((seqs, n, width), lambda i, j: (i, 0, 0))

    return pl.pallas_call(
        functools.partial(_mixer_out_kernel, seqs=seqs, tr=tr, n=n),
        out_shape=jax.ShapeDtypeStruct((b, n, D_MODEL), F32),
        grid=(b // seqs, n // tr),
        in_specs=[
            tile(D_MODEL), tile(ATT_DIM), tile(CONV_DIM), whole(CONV_DIM), whole(2 * FOURIER_DIM),
            pl.BlockSpec((tr, 2 * n), lambda i, j: (j, 0)),
            pl.BlockSpec((None, 3, CONV_DIM), lambda i, j: (layer, 0, 0)),
            pl.BlockSpec((None, None, N_MOD, D_MODEL), lambda i, j: (layer, mod_row_of_batch(i), 0, 0)),
            pl.BlockSpec((None, 6, D_MODEL), lambda i, j: (layer, 0, 0)),
            _resident((None, D_MODEL, D_MODEL), lambda i, j: (layer, 0, 0)),
        ],
        out_specs=tile(D_MODEL),
        compiler_params=_params(("arbitrary", "arbitrary")),
        name="mixer_out",
    )(x, att, gb, u, xcs, dft, conv_w, mods, norm_g, wout)


def _position_dft(n):
    jk = (np.arange(n, dtype=np.int64)[:, None] * np.arange(n, dtype=np.int64)[None, :]) % n
    ang = 2.0 * np.pi * jk.astype(np.float64) / n
    return np.concatenate([np.cos(ang), -np.sin(ang)], axis=1).astype(np.float32)


def _channel_dft():
    g = FOURIER_GROUP_DIM
    jk = (np.arange(g)[:, None] * np.arange(g)[None, :]) % g
    ang = 2.0 * np.pi * jk.astype(np.float64) / g
    eye = np.eye(FOURIER_DIM // g)
    return np.concatenate([np.kron(eye, np.cos(ang)), np.kron(eye, np.sin(ang))], axis=1).astype(np.float32)


def _rope_tables(n):
    t = jnp.arange(n)
    row = (t // GRID_W).astype(F32)
    col = (t % GRID_W).astype(F32)
    inv = 1.0 / (ROPE_THETA ** (jnp.arange(ROT_FREQS, dtype=F32) / ROT_FREQS))
    ang_r = row[:, None] * inv
    ang_c = col[:, None] * inv
    cos = jnp.concatenate([jnp.cos(ang_r)] * 2 + [jnp.cos(ang_c)] * 2, axis=-1)
    sin = jnp.concatenate([-jnp.sin(ang_r), jnp.sin(ang_r), -jnp.sin(ang_c), jnp.sin(ang_c)], axis=-1)
    return jnp.tile(cos, (1, 2)), jnp.tile(sin, (1, 2))


def _run_path(x, mods, mod_row, weights, ctx, rope_tabs, dft, *, tm, tq, seqs, tr):
    norm_g, wgu, wd, win, wout, cdft, conv_w, lam_qk, subln_g = weights
    b, n, _ = x.shape
    tiles_per_seq = max(n // tm, 1)
    seqs_per_tile = max(tm // n, 1)

    def mod_row_of_tile(i):
        return mod_row(i * seqs_per_tile // tiles_per_seq)

    def shape3(a):
        return a.reshape(b, n, a.shape[-1])

    ctx_k, ctx_v = ctx if ctx is not None else (None, None)
    caches = None
    for l in range(DEPTH):
        flat = x.reshape(b * n, D_MODEL)
        flat = _ffn(flat, mods, norm_g, wgu, wd, layer=l, half=0, mod_row_of_tile=mod_row_of_tile, tm=tm)
        q, k, v, gb, u, xcs, *caches = _mixer_in(flat, mods, norm_g, win, cdft, rope_tabs, caches, layer=l,
                                                 mod_row_of_tile=mod_row_of_tile, tm=tm, seq=n)
        caches = caches or None
        att = _attention(shape3(q), shape3(k), shape3(v), ctx_k, ctx_v, lam_qk, subln_g, layer=l, tq=tq)
        x = _mixer_out(shape3(flat), att, shape3(gb), shape3(u), shape3(xcs), dft, conv_w, mods, norm_g,
                       wout, layer=l, mod_row_of_batch=lambda i: mod_row(i * seqs), seqs=seqs, tr=tr)
        flat = _ffn(x.reshape(b * n, D_MODEL), mods, norm_g, wgu, wd, layer=l, half=1,
                    mod_row_of_tile=mod_row_of_tile, tm=tm)
        x = flat.reshape(b, n, D_MODEL)
    return x, caches


def kernel(x_prompt, x_sample, cache_k, cache_v, c, c_ctx, w_mod, b_mod, norm_g, w_ffn_gu, w_ffn_down,
           w_in, w_out, conv_w, lam_qk, subln_g):
    batch, seq, _ = x_prompt.shape
    dec_batch, dec_seq, _ = x_sample.shape
    past = cache_k.shape[2]

    cvecs = jnp.zeros((MOD_ROWS, D_MODEL), F32).at[0].set(c_ctx).at[1:1 + dec_batch].set(c)
    mods = _modulation(cvecs, w_mod, b_mod)

    wgu = _regroup_gu(w_ffn_gu)
    wd = _cast_bf16(w_ffn_down.reshape(DEPTH * 2 * D_FF, D_MODEL), D_FF // 2)
    wd = wd.reshape(DEPTH, 2, N_FF_CHUNKS, FF_CHUNK, D_MODEL)
    win = _cast_bf16(w_in.reshape(DEPTH * D_MODEL, IN_DIM), D_MODEL // 2).reshape(DEPTH, D_MODEL, IN_DIM)
    wout = _cast_bf16(w_out.reshape(DEPTH * D_MODEL, D_MODEL), D_MODEL).reshape(DEPTH, D_MODEL, D_MODEL)
    cdft = jnp.asarray(_channel_dft()).astype(BF16)
    weights = (norm_g, wgu, wd, win, wout, cdft, conv_w, lam_qk, subln_g)

    dft_ctx = jnp.asarray(_position_dft(seq)).astype(BF16)
    y_prompt, (kc, vc) = _run_path(x_prompt, mods, lambda i: 0, weights, None, None, dft_ctx,
                                   tm=1024, tq=seq, seqs=4, tr=seq)
    new_cache_k = kc.reshape(batch, DEPTH, N_HEADS, 2, QK_DIM, seq).transpose(0, 1, 5, 2, 3, 4)
    new_cache_v = vc.reshape(batch, DEPTH, seq, N_HEADS, V_DIM)

    dft_lat = jnp.asarray(_position_dft(dec_seq)).astype(BF16)
    ctx = (cache_k.transpose(0, 1, 3, 4, 5, 2).reshape(dec_batch, DEPTH, ATT_DIM, past),
           cache_v.reshape(dec_batch, DEPTH, past * N_HEADS, V_DIM))
    y_sample, _ = _run_path(x_sample, mods, lambda i: 1 + i, weights, ctx, _rope_tables(dec_seq), dft_lat,
                            tm=1024, tq=256, seqs=1, tr=512)
    return (y_prompt, y_sample, new_cache_k, new_cache_v)
```

```python
import functools
import math

import jax
import jax.numpy as jnp
import numpy as np
from jax import lax
from jax.experimental import pallas as pl
from jax.experimental.pallas import tpu as pltpu

D_MODEL = 1024
DEPTH = 2
GRID_W = 64
V_DIM = 128
QK_DIM = V_DIM // 2
N_HEADS = 4
ATT_DIM = N_HEADS * V_DIM
CONV_DIM = 256
FOURIER_DIM = 256
FOURIER_GROUP_DIM = 64
IN_DIM = 3 * ATT_DIM + 3 * CONV_DIM + FOURIER_DIM
D_FF = 2816
N_MOD = 9
ROT_FREQS = 16
ROPE_THETA = 10000.0
EPS = 1e-6
FFN_RESIDUAL_WEIGHT = 0.5
LOG2_E = math.log2(math.e)

FF_CHUNK = 256
N_FF_CHUNKS = D_FF // FF_CHUNK
MOD_ROWS = 8
SUBLANES = 8
LANES = 128
VMEM_LIMIT = 56 * 1024 * 1024

F32 = jnp.float32
BF16 = jnp.bfloat16


def _lambda_init(layer_idx):
    return 0.8 - 0.6 * math.exp(-0.3 * layer_idx)


def _rms(x, g):
    return x * lax.rsqrt(jnp.mean(x * x, axis=-1, keepdims=True) + EPS) * g


def _dot(a, b):
    return jnp.dot(a, b, preferred_element_type=F32)


def _params(semantics):
    return pltpu.CompilerParams(dimension_semantics=semantics, vmem_limit_bytes=VMEM_LIMIT)


def _resident(block_shape, index_map):
    return pl.BlockSpec(block_shape, index_map, pipeline_mode=pl.Buffered(1))


def _cast_kernel(x_ref, o_ref):
    o_ref[...] = x_ref[...].astype(o_ref.dtype)


def _cast_bf16(x, block_rows):
    rows, cols = x.shape
    return pl.pallas_call(
        _cast_kernel,
        out_shape=jax.ShapeDtypeStruct((rows, cols), BF16),
        grid=(rows // block_rows,),
        in_specs=[pl.BlockSpec((block_rows, cols), lambda i: (i, 0))],
        out_specs=pl.BlockSpec((block_rows, cols), lambda i: (i, 0)),
        compiler_params=_params(("arbitrary",)),
        name="cast_bf16",
    )(x)


def _regroup_gu_kernel(g_ref, u_ref, o_ref):
    for c in range(N_FF_CHUNKS):
        cols = slice(c * FF_CHUNK, (c + 1) * FF_CHUNK)
        o_ref[c, :, 0:FF_CHUNK] = g_ref[:, cols].astype(BF16)
        o_ref[c, :, FF_CHUNK:2 * FF_CHUNK] = u_ref[:, cols].astype(BF16)


def _regroup_gu(w_gu, block_rows=256):
    def half(part):
        return pl.BlockSpec((None, None, block_rows, D_FF), lambda l, s, r: (l, s, r, part))

    return pl.pallas_call(
        _regroup_gu_kernel,
        out_shape=jax.ShapeDtypeStruct((DEPTH, 2, N_FF_CHUNKS, D_MODEL, 2 * FF_CHUNK), BF16),
        grid=(DEPTH, 2, D_MODEL // block_rows),
        in_specs=[half(0), half(1)],
        out_specs=pl.BlockSpec((None, None, N_FF_CHUNKS, block_rows, 2 * FF_CHUNK),
                               lambda l, s, r: (l, s, 0, r, 0)),
        compiler_params=_params(("arbitrary",) * 3),
        name="regroup_gu",
    )(w_gu, w_gu)


def _mod_kernel(c_ref, w_ref, b_ref, o_ref):
    c = c_ref[...]
    s = (c / (1.0 + jnp.exp(-c))).astype(BF16)
    o_ref[...] = _dot(s, w_ref[...].astype(BF16)) + b_ref[...]


def _modulation(cvecs, w_mod, b_mod):
    col = D_MODEL
    out = pl.pallas_call(
        _mod_kernel,
        out_shape=jax.ShapeDtypeStruct((DEPTH, MOD_ROWS, N_MOD * D_MODEL), F32),
        grid=(DEPTH, N_MOD * D_MODEL // col),
        in_specs=[
            pl.BlockSpec((MOD_ROWS, D_MODEL), lambda l, j: (0, 0)),
            pl.BlockSpec((None, D_MODEL, col), lambda l, j: (l, 0, j)),
            pl.BlockSpec((None, 1, col), lambda l, j: (l, 0, j)),
        ],
        out_specs=pl.BlockSpec((None, MOD_ROWS, col), lambda l, j: (l, 0, j)),
        compiler_params=_params(("arbitrary", "arbitrary")),
        name="modulation",
    )(cvecs, w_mod, b_mod.reshape(DEPTH, 1, N_MOD * D_MODEL))
    return out.reshape(DEPTH, MOD_ROWS, N_MOD, D_MODEL)


def _ffn_kernel(x_ref, mod_ref, g_ref, wgu_ref, wd_ref, o_ref, acc_ref, *, sub, g_row):
    x = x_ref[...]
    shift = mod_ref[3 * sub:3 * sub + 1, :]
    scale = mod_ref[3 * sub + 1:3 * sub + 2, :]
    gate = mod_ref[3 * sub + 2:3 * sub + 3, :]
    h = _rms(x, g_ref[g_row:g_row + 1, :]) * (1.0 + scale) + shift
    hb = h.astype(BF16)
    acc_ref[...] = jnp.zeros_like(acc_ref)

    def chunk(c, carry):
        gu = _dot(hb, wgu_ref[c])
        g = gu[:, :FF_CHUNK]
        u = gu[:, FF_CHUNK:]
        a = (g / (1.0 + jnp.exp(-g)) * u).astype(BF16)
        acc_ref[...] += _dot(a, wd_ref[c])
        return carry

    lax.fori_loop(0, N_FF_CHUNKS, chunk, 0)
    y = _rms(acc_ref[...], g_ref[g_row + 1:g_row + 2, :])
    o_ref[...] = x + FFN_RESIDUAL_WEIGHT * gate * y


def _ffn(x, mods, norm_g, wgu, wd, *, layer, half, mod_row_of_tile, tm):
    n = x.shape[0]
    sub = 2 * half
    g_row = 4 * half
    return pl.pallas_call(
        functools.partial(_ffn_kernel, sub=sub, g_row=g_row),
        out_shape=jax.ShapeDtypeStruct((n, D_MODEL), F32),
        grid=(n // tm,),
        in_specs=[
            pl.BlockSpec((tm, D_MODEL), lambda i: (i, 0)),
            pl.BlockSpec((None, None, N_MOD, D_MODEL), lambda i: (layer, mod_row_of_tile(i), 0, 0)),
            pl.BlockSpec((None, 6, D_MODEL), lambda i: (layer, 0, 0)),
            _resident((None, None, N_FF_CHUNKS, D_MODEL, 2 * FF_CHUNK), lambda i: (layer, half, 0, 0, 0)),
            _resident((None, None, N_FF_CHUNKS, FF_CHUNK, D_MODEL), lambda i: (layer, half, 0, 0, 0)),
        ],
        out_specs=pl.BlockSpec((tm, D_MODEL), lambda i: (i, 0)),
        scratch_shapes=[pltpu.VMEM((tm, D_MODEL), F32)],
        compiler_params=_params(("arbitrary",)),
        name="ffn",
    )(x, mods, norm_g, wgu, wd)


def _rope(x, cos, sin_signed, lo_half):
    partner = jnp.where(lo_half, pltpu.roll(x, LANES - ROT_FREQS, 1), pltpu.roll(x, ROT_FREQS, 1))
    return x * cos + partner * sin_signed


def _mixer_in_kernel(*refs, rotary, seq):
    x_ref, mod_ref, g_ref, win_ref, cdft_ref = refs[:5]
    if rotary:
        cos_ref, sin_ref = refs[5:7]
        q_ref, kt_ref, v_ref, gb_ref, u_ref, xcs_ref = refs[7:]
    else:
        q_ref, kt_ref, v_ref, gb_ref, u_ref, xcs_ref, kc_ref, vc_ref = refs[-8:]
    x = x_ref[...]
    shift = mod_ref[3:4, :]
    scale = mod_ref[4:5, :]
    h = _rms(x, g_ref[2:3, :]) * (1.0 + scale) + shift
    hb = h.astype(BF16)

    q = _dot(hb, win_ref[:, 0:ATT_DIM]) * (QK_DIM ** -0.5 * LOG2_E)
    k = _dot(hb, win_ref[:, ATT_DIM:2 * ATT_DIM])
    v = _dot(hb, win_ref[:, 2 * ATT_DIM:3 * ATT_DIM])
    v_ref[...] = v.astype(v_ref.dtype)
    if rotary:
        cos = cos_ref[...]
        sin = sin_ref[...]
        lane = lax.broadcasted_iota(jnp.int32, (1, LANES), 1)
        lo_half = (lane % (2 * ROT_FREQS)) < ROT_FREQS
        for hd in range(N_HEADS):
            sl = slice(hd * V_DIM, (hd + 1) * V_DIM)
            q_ref[:, sl] = _rope(q[:, sl], cos, sin, lo_half).astype(q_ref.dtype)
            kt_ref[sl, :] = _rope(k[:, sl], cos, sin, lo_half).T.astype(kt_ref.dtype)
    else:
        q_ref[...] = q.astype(q_ref.dtype)
        for s in range(x.shape[0] // seq):
            rows = slice(s * seq, (s + 1) * seq)
            kt = k[rows, :].T
            kc_ref[s] = kt
            kt_ref[s] = kt.astype(kt_ref.dtype)
            for hd in range(N_HEADS):
                vc_ref[s, pl.ds(hd, seq, stride=N_HEADS), :] = v[rows, hd * V_DIM:(hd + 1) * V_DIM]

    c0 = 3 * ATT_DIM
    gates = _dot(hb, win_ref[:, c0:c0 + 3 * CONV_DIM])
    gb_ref[...] = gates[:, 0:CONV_DIM]
    u_ref[...] = gates[:, CONV_DIM:2 * CONV_DIM] * gates[:, 2 * CONV_DIM:3 * CONV_DIM]
    f = _dot(hb, win_ref[:, c0 + 3 * CONV_DIM:IN_DIM])
    xcs_ref[...] = _dot(f.astype(BF16), cdft_ref[...]).astype(BF16)


def _mixer_in(x, mods, norm_g, win, cdft, rope_tabs, caches, *, layer, mod_row_of_tile, tm, seq):
    n = x.shape[0]
    rotary = rope_tabs is not None
    in_specs = [
        pl.BlockSpec((tm, D_MODEL), lambda i: (i, 0)),
        pl.BlockSpec((None, None, N_MOD, D_MODEL), lambda i: (layer, mod_row_of_tile(i), 0, 0)),
        pl.BlockSpec((None, 6, D_MODEL), lambda i: (layer, 0, 0)),
        _resident((None, D_MODEL, IN_DIM), lambda i: (layer, 0, 0)),
        _resident((FOURIER_DIM, 2 * FOURIER_DIM), lambda i: (0, 0)),
    ]
    args = [x, mods, norm_g, win, cdft]
    if rotary:
        tiles_per_seq = rope_tabs[0].shape[0] // tm
        for tab in rope_tabs:
            in_specs.append(pl.BlockSpec((tm, LANES), lambda i: (i % tiles_per_seq, 0)))
            args.append(tab)

    def rows(width):
        return pl.BlockSpec((tm, width), lambda i: (i, 0))

    batch = n // seq
    if tm <= seq:
        tiles_per_seq = seq // tm
        kt_spec = pl.BlockSpec((None, ATT_DIM, tm), lambda i: (i // tiles_per_seq, 0, i % tiles_per_seq))
    else:
        kt_spec = pl.BlockSpec((tm // seq, ATT_DIM, seq), lambda i: (i, 0, 0))
    out_shape = [
        jax.ShapeDtypeStruct((n, ATT_DIM), BF16),
        jax.ShapeDtypeStruct((batch, ATT_DIM, seq), BF16),
        jax.ShapeDtypeStruct((n, ATT_DIM), BF16),
        jax.ShapeDtypeStruct((n, CONV_DIM), F32),
        jax.ShapeDtypeStruct((n, CONV_DIM), F32),
        jax.ShapeDtypeStruct((n, 2 * FOURIER_DIM), BF16),
    ]
    out_specs = [rows(ATT_DIM), kt_spec, rows(ATT_DIM), rows(CONV_DIM), rows(CONV_DIM),
                 rows(2 * FOURIER_DIM)]
    aliases = {}
    if not rotary:
        seqs = tm // seq
        out_shape += [jax.ShapeDtypeStruct((batch, DEPTH, ATT_DIM, seq), F32),
                      jax.ShapeDtypeStruct((batch, DEPTH, N_HEADS * seq, V_DIM), F32)]
        out_specs += [pl.BlockSpec((seqs, None, ATT_DIM, seq), lambda i: (i, layer, 0, 0)),
                      pl.BlockSpec((seqs, None, N_HEADS * seq, V_DIM), lambda i: (i, layer, 0, 0))]
        if caches is not None:
            first_cache_out = len(out_shape) - 2
            for offset, arr in enumerate(caches):
                aliases[len(args)] = first_cache_out + offset
                in_specs.append(pl.BlockSpec(memory_space=pl.ANY))
                args.append(arr)

    return pl.pallas_call(
        functools.partial(_mixer_in_kernel, rotary=rotary, seq=seq),
        out_shape=out_shape,
        grid=(n // tm,),
        in_specs=in_specs,
        out_specs=out_specs,
        input_output_aliases=aliases,
        compiler_params=_params(("arbitrary",)),
        name="mixer_in",
    )(*args)


def _lambda(lam_ref, lambda_init):
    lq = lam_ref[...]
    s01 = jnp.sum(lq[0:1, :] * lq[1:2, :], axis=-1, keepdims=True)
    s23 = jnp.sum(lq[2:3, :] * lq[3:4, :], axis=-1, keepdims=True)
    return jnp.exp(s01) - jnp.exp(s23) + lambda_init


def _attention_kernel(*refs, has_ctx, lambda_init, nb):
    if has_ctx:
        q_ref, kt_ref, v_ref, ckt_ref, cv_ref, lam_ref, sg_ref, o_ref = refs
    else:
        q_ref, kt_ref, v_ref, lam_ref, sg_ref, o_ref = refs
    lam = _lambda(lam_ref, lambda_init)
    lane = lax.broadcasted_iota(jnp.int32, (1, V_DIM), 1)
    first_map = lane < QK_DIM
    sub_g = sg_ref[...] * (1.0 - lambda_init)

    def scores(b, hd):
        sl = slice(hd * V_DIM, (hd + 1) * V_DIM)
        q = q_ref[b, :, sl]
        zero = jnp.zeros_like(q)
        kt_sets = [kt_ref[b, sl, :]]
        if has_ctx:
            kt_sets.append(ckt_ref[sl, :].astype(BF16))
        out = []
        for keep in (first_map, jnp.logical_not(first_map)):
            qm = jnp.where(keep, q, zero)
            out.append([_dot(qm, kt) for kt in kt_sets])
        return out

    def finish(b, hd, s_maps):
        sl = slice(hd * V_DIM, (hd + 1) * V_DIM)
        v_sets = [v_ref[b, :, sl]]
        if has_ctx:
            past = ckt_ref.shape[1]
            v_sets.append(cv_ref[pl.ds(hd, past, stride=N_HEADS), :].astype(BF16))
        exps, sums = [], []
        for s in s_maps:
            m = functools.reduce(jnp.maximum, [jnp.max(t, axis=-1, keepdims=True) for t in s])
            e = [jnp.exp2(t - m) for t in s]
            exps.append(e)
            sums.append(functools.reduce(jnp.add, [jnp.sum(t, axis=-1, keepdims=True) for t in e]))
        ratio = lam * sums[0] / sums[1]
        o = functools.reduce(jnp.add, [
            _dot((e0 - ratio * e1).astype(BF16), vs) for e0, e1, vs in zip(exps[0], exps[1], v_sets)])
        o = o * (1.0 / sums[0])
        o_ref[b, :, sl] = (_rms(o, sub_g)).astype(o_ref.dtype)

    items = [(b, hd) for b in range(nb) for hd in range(N_HEADS)]
    pending = scores(*items[0])
    for idx, item in enumerate(items):
        upcoming = scores(*items[idx + 1]) if idx + 1 < len(items) else None
        finish(*item, pending)
        pending = upcoming


def _attention(q, kt, v, ctx_kt, ctx_v, lam_qk, subln_g, *, layer, tq, nb):
    b, n, _ = q.shape
    has_ctx = ctx_kt is not None
    assert not has_ctx or nb == 1
    in_specs = [
        pl.BlockSpec((nb, tq, ATT_DIM), lambda i, j: (i, j, 0)),
        pl.BlockSpec((nb, ATT_DIM, n), lambda i, j: (i, 0, 0)),
        pl.BlockSpec((nb, n, ATT_DIM), lambda i, j: (i, 0, 0)),
    ]
    args = [q, kt, v]
    if has_ctx:
        for arr in (ctx_kt, ctx_v):
            in_specs.append(pl.BlockSpec((None, None) + arr.shape[2:], lambda i, j: (i, layer, 0, 0)))
            args.append(arr)
    in_specs += [
        pl.BlockSpec((None, 4, QK_DIM), lambda i, j: (layer, 0, 0)),
        pl.BlockSpec((None, 1, V_DIM), lambda i, j: (layer, 0, 0)),
    ]
    args += [lam_qk, subln_g.reshape(DEPTH, 1, V_DIM)]
    return pl.pallas_call(
        functools.partial(_attention_kernel, has_ctx=has_ctx, lambda_init=_lambda_init(layer), nb=nb),
        out_shape=jax.ShapeDtypeStruct((b, n, ATT_DIM), BF16),
        grid=(b // nb, n // tq),
        in_specs=in_specs,
        out_specs=pl.BlockSpec((nb, tq, ATT_DIM), lambda i, j: (i, j, 0)),
        compiler_params=_params(("arbitrary", "arbitrary")),
        name="attention",
    )(*args)


def _mixer_out_kernel(x_ref, att_ref, gb_ref, u_ref, xcs_ref, dft_ref, cw_ref, mod_ref, g_ref, wout_ref,
                      o_ref, *, seqs, tr, n):
    r0 = pl.multiple_of(pl.program_id(1) * tr, tr)
    gate = mod_ref[5:6, :]
    cw = cw_ref[...]
    row = lax.broadcasted_iota(jnp.int32, (tr, 1), 0)
    four_scale = (n * FOURIER_GROUP_DIM) ** -0.5
    for s in range(seqs):
        four = (_dot(dft_ref[:, 0:n], xcs_ref[s, :, 0:FOURIER_DIM])
                + _dot(dft_ref[:, n:2 * n], xcs_ref[s, :, FOURIER_DIM:2 * FOURIER_DIM])) * four_scale
        u = u_ref[s, pl.ds(r0, tr), :]
        before_start = pl.multiple_of(jnp.maximum(r0 - SUBLANES, 0), SUBLANES)
        after_start = pl.multiple_of(jnp.minimum(r0 + tr, n - SUBLANES), SUBLANES)
        before = u_ref[s, pl.ds(before_start, SUBLANES), :][SUBLANES - 1:SUBLANES, :]
        after = u_ref[s, pl.ds(after_start, SUBLANES), :][0:1, :]
        before = jnp.where(r0 > 0, before, 0.0)
        after = jnp.where(r0 + tr < n, after, 0.0)
        prev = jnp.where(row == 0, before, pltpu.roll(u, 1, 0))
        nxt = jnp.where(row == tr - 1, after, pltpu.roll(u, tr - 1, 0))
        conv = gb_ref[s] * (prev * cw[0:1, :] + u * cw[1:2, :] + nxt * cw[2:3, :])
        y = (_dot(att_ref[s], wout_ref[0:ATT_DIM, :])
             + _dot(conv.astype(BF16), wout_ref[ATT_DIM:ATT_DIM + CONV_DIM, :])
             + _dot(four.astype(BF16), wout_ref[ATT_DIM + CONV_DIM:D_MODEL, :]))
        o_ref[s] = x_ref[s] + gate * _rms(y, g_ref[3:4, :])


def _mixer_out(x, att, gb, u, xcs, dft, conv_w, mods, norm_g, wout, *, layer, mod_row_of_batch, seqs, tr):
    b, n, _ = x.shape

    def tile(width):
        return pl.BlockSpec((seqs, tr, width), lambda i, j: (i, j, 0))

    def whole(width):
        return pl.BlockSpec((seqs, n, width), lambda i, j: (i, 0, 0))

    return pl.pallas_call(
        functools.partial(_mixer_out_kernel, seqs=seqs, tr=tr, n=n),
        out_shape=jax.ShapeDtypeStruct((b, n, D_MODEL), F32),
        grid=(b // seqs, n // tr),
        in_specs=[
            tile(D_MODEL), tile(ATT_DIM), tile(CONV_DIM), whole(CONV_DIM), whole(2 * FOURIER_DIM),
            pl.BlockSpec((tr, 2 * n), lambda i, j: (j, 0)),
            pl.BlockSpec((None, 3, CONV_DIM), lambda i, j: (layer, 0, 0)),
            pl.BlockSpec((None, None, N_MOD, D_MODEL), lambda i, j: (layer, mod_row_of_batch(i), 0, 0)),
            pl.BlockSpec((None, 6, D_MODEL), lambda i, j: (layer, 0, 0)),
            _resident((None, D_MODEL, D_MODEL), lambda i, j: (layer, 0, 0)),
        ],
        out_specs=tile(D_MODEL),
        compiler_params=_params(("arbitrary", "arbitrary")),
        name="mixer_out",
    )(x, att, gb, u, xcs, dft, conv_w, mods, norm_g, wout)


def _position_dft(n):
    jk = (np.arange(n, dtype=np.int64)[:, None] * np.arange(n, dtype=np.int64)[None, :]) % n
    ang = 2.0 * np.pi * jk.astype(np.float64) / n
    return np.concatenate([np.cos(ang), -np.sin(ang)], axis=1).astype(np.float32)


def _channel_dft():
    g = FOURIER_GROUP_DIM
    jk = (np.arange(g)[:, None] * np.arange(g)[None, :]) % g
    ang = 2.0 * np.pi * jk.astype(np.float64) / g
    eye = np.eye(FOURIER_DIM // g)
    return np.concatenate([np.kron(eye, np.cos(ang)), np.kron(eye, np.sin(ang))], axis=1).astype(np.float32)


def _rope_tables(n):
    t = jnp.arange(n)
    row = (t // GRID_W).astype(F32)
    col = (t % GRID_W).astype(F32)
    inv = 1.0 / (ROPE_THETA ** (jnp.arange(ROT_FREQS, dtype=F32) / ROT_FREQS))
    ang_r = row[:, None] * inv
    ang_c = col[:, None] * inv
    cos = jnp.concatenate([jnp.cos(ang_r)] * 2 + [jnp.cos(ang_c)] * 2, axis=-1)
    sin = jnp.concatenate([-jnp.sin(ang_r), jnp.sin(ang_r), -jnp.sin(ang_c), jnp.sin(ang_c)], axis=-1)
    return jnp.tile(cos, (1, 2)), jnp.tile(sin, (1, 2))


def _run_path(x, mods, mod_row, weights, ctx, rope_tabs, dft, *, tm, tq, seqs, tr):
    norm_g, wgu, wd, win, wout, cdft, conv_w, lam_qk, subln_g = weights
    b, n, _ = x.shape
    tiles_per_seq = max(n // tm, 1)
    seqs_per_tile = max(tm // n, 1)

    def mod_row_of_tile(i):
        return mod_row(i * seqs_per_tile // tiles_per_seq)

    def shape3(a):
        return a.reshape(b, n, a.shape[-1])

    ctx_k, ctx_v = ctx if ctx is not None else (None, None)
    caches = None
    for l in range(DEPTH):
        flat = x.reshape(b * n, D_MODEL)
        flat = _ffn(flat, mods, norm_g, wgu, wd, layer=l, half=0, mod_row_of_tile=mod_row_of_tile, tm=tm)
        q, kt, v, gb, u, xcs, *caches = _mixer_in(flat, mods, norm_g, win, cdft, rope_tabs, caches, layer=l,
                                                  mod_row_of_tile=mod_row_of_tile, tm=tm, seq=n)
        caches = caches or None
        att = _attention(shape3(q), kt, shape3(v), ctx_k, ctx_v, lam_qk, subln_g, layer=l, tq=tq, nb=seqs)
        x = _mixer_out(shape3(flat), att, shape3(gb), shape3(u), shape3(xcs), dft, conv_w, mods, norm_g,
                       wout, layer=l, mod_row_of_batch=lambda i: mod_row(i * seqs), seqs=seqs, tr=tr)
        flat = _ffn(x.reshape(b * n, D_MODEL), mods, norm_g, wgu, wd, layer=l, half=1,
                    mod_row_of_tile=mod_row_of_tile, tm=tm)
        x = flat.reshape(b, n, D_MODEL)
    return x, caches


def kernel(x_prompt, x_sample, cache_k, cache_v, c, c_ctx, w_mod, b_mod, norm_g, w_ffn_gu, w_ffn_down,
           w_in, w_out, conv_w, lam_qk, subln_g):
    batch, seq, _ = x_prompt.shape
    dec_batch, dec_seq, _ = x_sample.shape
    past = cache_k.shape[2]

    cvecs = jnp.zeros((MOD_ROWS, D_MODEL), F32).at[0].set(c_ctx).at[1:1 + dec_batch].set(c)
    mods = _modulation(cvecs, w_mod, b_mod)

    wgu = _regroup_gu(w_ffn_gu)
    wd = _cast_bf16(w_ffn_down.reshape(DEPTH * 2 * D_FF, D_MODEL), D_FF // 2)
    wd = wd.reshape(DEPTH, 2, N_FF_CHUNKS, FF_CHUNK, D_MODEL)
    win = _cast_bf16(w_in.reshape(DEPTH * D_MODEL, IN_DIM), D_MODEL // 2).reshape(DEPTH, D_MODEL, IN_DIM)
    wout = _cast_bf16(w_out.reshape(DEPTH * D_MODEL, D_MODEL), D_MODEL).reshape(DEPTH, D_MODEL, D_MODEL)
    cdft = jnp.asarray(_channel_dft()).astype(BF16)
    weights = (norm_g, wgu, wd, win, wout, cdft, conv_w, lam_qk, subln_g)

    dft_ctx = jnp.asarray(_position_dft(seq)).astype(BF16)
    y_prompt, (kc, vc) = _run_path(x_prompt, mods, lambda i: 0, weights, None, None, dft_ctx,
                                   tm=1024, tq=seq, seqs=4, tr=seq)
    new_cache_k = kc.reshape(batch, DEPTH, N_HEADS, 2, QK_DIM, seq).transpose(0, 1, 5, 2, 3, 4)
    new_cache_v = vc.reshape(batch, DEPTH, seq, N_HEADS, V_DIM)

    dft_lat = jnp.asarray(_position_dft(dec_seq)).astype(BF16)
    ctx = (cache_k.transpose(0, 1, 3, 4, 5, 2).reshape(dec_batch, DEPTH, ATT_DIM, past),
           cache_v.reshape(dec_batch, DEPTH, past * N_HEADS, V_DIM))
    y_sample, _ = _run_path(x_sample, mods, lambda i: 1 + i, weights, ctx, _rope_tables(dec_seq), dft_lat,
                            tm=1024, tq=256, seqs=1, tr=512)
    return (y_prompt, y_sample, new_cache_k, new_cache_v)
```

```python
import functools
import math

import jax
import jax.numpy as jnp
import numpy as np
from jax import lax
from jax.experimental import pallas as pl
from jax.experimental.pallas import tpu as pltpu

D_MODEL = 1024
DEPTH = 2
GRID_W = 64
V_DIM = 128
QK_DIM = V_DIM // 2
N_HEADS = 4
ATT_DIM = N_HEADS * V_DIM
CONV_DIM = 256
FOURIER_DIM = 256
FOURIER_GROUP_DIM = 64
IN_DIM = 3 * ATT_DIM + 3 * CONV_DIM + FOURIER_DIM
D_FF = 2816
N_MOD = 9
ROT_FREQS = 16
ROPE_THETA = 10000.0
EPS = 1e-6
FFN_RESIDUAL_WEIGHT = 0.5
LOG2_E = math.log2(math.e)

FF_CHUNK = 256
N_FF_CHUNKS = D_FF // FF_CHUNK
FF_GROUP = 3
assert (N_FF_CHUNKS - 2) % FF_GROUP == 0
MOD_ROWS = 8
SUBLANES = 8
LANES = 128
VMEM_LIMIT = 56 * 1024 * 1024

F32 = jnp.float32
BF16 = jnp.bfloat16


def _lambda_init(layer_idx):
    return 0.8 - 0.6 * math.exp(-0.3 * layer_idx)


def _rms(x, g):
    return x * lax.rsqrt(jnp.mean(x * x, axis=-1, keepdims=True) + EPS) * g


def _dot(a, b):
    return jnp.dot(a, b, preferred_element_type=F32)


def _params(semantics):
    return pltpu.CompilerParams(dimension_semantics=semantics, vmem_limit_bytes=VMEM_LIMIT)


def _resident(block_shape, index_map):
    return pl.BlockSpec(block_shape, index_map, pipeline_mode=pl.Buffered(1))


def _cast_kernel(x_ref, o_ref):
    o_ref[...] = x_ref[...].astype(o_ref.dtype)


def _cast_bf16(x, block_rows):
    rows, cols = x.shape
    return pl.pallas_call(
        _cast_kernel,
        out_shape=jax.ShapeDtypeStruct((rows, cols), BF16),
        grid=(rows // block_rows,),
        in_specs=[pl.BlockSpec((block_rows, cols), lambda i: (i, 0))],
        out_specs=pl.BlockSpec((block_rows, cols), lambda i: (i, 0)),
        compiler_params=_params(("arbitrary",)),
        name="cast_bf16",
    )(x)


def _regroup_gu_kernel(g_ref, u_ref, o_ref):
    for c in range(N_FF_CHUNKS):
        cols = slice(c * FF_CHUNK, (c + 1) * FF_CHUNK)
        o_ref[c, :, 0:FF_CHUNK] = g_ref[:, cols].astype(BF16)
        o_ref[c, :, FF_CHUNK:2 * FF_CHUNK] = u_ref[:, cols].astype(BF16)


def _regroup_gu(w_gu, block_rows=256):
    def half(part):
        return pl.BlockSpec((None, None, block_rows, D_FF), lambda l, s, r: (l, s, r, part))

    return pl.pallas_call(
        _regroup_gu_kernel,
        out_shape=jax.ShapeDtypeStruct((DEPTH, 2, N_FF_CHUNKS, D_MODEL, 2 * FF_CHUNK), BF16),
        grid=(DEPTH, 2, D_MODEL // block_rows),
        in_specs=[half(0), half(1)],
        out_specs=pl.BlockSpec((None, None, N_FF_CHUNKS, block_rows, 2 * FF_CHUNK),
                               lambda l, s, r: (l, s, 0, r, 0)),
        compiler_params=_params(("arbitrary",) * 3),
        name="regroup_gu",
    )(w_gu, w_gu)


def _mod_kernel(c_ref, w_ref, b_ref, o_ref):
    c = c_ref[...]
    s = (c / (1.0 + jnp.exp(-c))).astype(BF16)
    o_ref[...] = _dot(s, w_ref[...].astype(BF16)) + b_ref[...]


def _modulation(cvecs, w_mod, b_mod):
    col = D_MODEL
    out = pl.pallas_call(
        _mod_kernel,
        out_shape=jax.ShapeDtypeStruct((DEPTH, MOD_ROWS, N_MOD * D_MODEL), F32),
        grid=(DEPTH, N_MOD * D_MODEL // col),
        in_specs=[
            pl.BlockSpec((MOD_ROWS, D_MODEL), lambda l, j: (0, 0)),
            pl.BlockSpec((None, D_MODEL, col), lambda l, j: (l, 0, j)),
            pl.BlockSpec((None, 1, col), lambda l, j: (l, 0, j)),
        ],
        out_specs=pl.BlockSpec((None, MOD_ROWS, col), lambda l, j: (l, 0, j)),
        compiler_params=_params(("arbitrary", "arbitrary")),
        name="modulation",
    )(cvecs, w_mod, b_mod.reshape(DEPTH, 1, N_MOD * D_MODEL))
    return out.reshape(DEPTH, MOD_ROWS, N_MOD, D_MODEL)


def _ffn_kernel(x_ref, mod_ref, g_ref, wgu_ref, wd_ref, o_ref, acc_ref, *, sub, g_row):
    x = x_ref[...]
    shift = mod_ref[3 * sub:3 * sub + 1, :]
    scale = mod_ref[3 * sub + 1:3 * sub + 2, :]
    gate = mod_ref[3 * sub + 2:3 * sub + 3, :]
    h = _rms(x, g_ref[g_row:g_row + 1, :] * (1.0 + scale)) + shift
    hb = h.astype(BF16)

    def hidden(c):
        gu = _dot(hb, wgu_ref[c])
        g = gu[:, :FF_CHUNK]
        u = gu[:, FF_CHUNK:]
        return (g / (1.0 + jnp.exp(-g)) * u).astype(BF16)

    acc_ref[...] = _dot(hidden(0), wd_ref[0])

    def group(i, carry):
        c0 = 1 + i * FF_GROUP
        a = jnp.concatenate([hidden(c0 + k) for k in range(FF_GROUP)], axis=-1)
        wd = wd_ref[pl.ds(c0, FF_GROUP)].reshape(FF_GROUP * FF_CHUNK, D_MODEL)
        acc_ref[...] += _dot(a, wd)
        return carry

    lax.fori_loop(0, (N_FF_CHUNKS - 2) // FF_GROUP, group, 0)
    last = N_FF_CHUNKS - 1
    y = acc_ref[...] + _dot(hidden(last), wd_ref[last])
    o_ref[...] = x + _rms(y, g_ref[g_row + 1:g_row + 2, :] * (FFN_RESIDUAL_WEIGHT * gate))


def _ffn(x, mods, norm_g, wgu, wd, *, layer, half, mod_row_of_tile, tm):
    n = x.shape[0]
    sub = 2 * half
    g_row = 4 * half
    return pl.pallas_call(
        functools.partial(_ffn_kernel, sub=sub, g_row=g_row),
        out_shape=jax.ShapeDtypeStruct((n, D_MODEL), F32),
        grid=(n // tm,),
        in_specs=[
            pl.BlockSpec((tm, D_MODEL), lambda i: (i, 0)),
            pl.BlockSpec((None, None, N_MOD, D_MODEL), lambda i: (layer, mod_row_of_tile(i), 0, 0)),
            pl.BlockSpec((None, 6, D_MODEL), lambda i: (layer, 0, 0)),
            _resident((None, None, N_FF_CHUNKS, D_MODEL, 2 * FF_CHUNK), lambda i: (layer, half, 0, 0, 0)),
            _resident((None, None, N_FF_CHUNKS, FF_CHUNK, D_MODEL), lambda i: (layer, half, 0, 0, 0)),
        ],
        out_specs=pl.BlockSpec((tm, D_MODEL), lambda i: (i, 0)),
        scratch_shapes=[pltpu.VMEM((tm, D_MODEL), F32)],
        compiler_params=_params(("arbitrary",)),
        name="ffn",
    )(x, mods, norm_g, wgu, wd)


def _rope(x, cos, sin_signed, lo_half):
    partner = jnp.where(lo_half, pltpu.roll(x, LANES - ROT_FREQS, 1), pltpu.roll(x, ROT_FREQS, 1))
    return x * cos + partner * sin_signed


def _mixer_in_kernel(*refs, rotary, seq):
    x_ref, mod_ref, g_ref, win_ref, cdft_ref = refs[:5]
    if rotary:
        cos_ref, sin_ref = refs[5:7]
        q_ref, kt_ref, v_ref, gb_ref, u_ref, xcs_ref = refs[7:]
    else:
        q_ref, kt_ref, v_ref, gb_ref, u_ref, xcs_ref, kc_ref, vc_ref = refs[-8:]
    x = x_ref[...]
    shift = mod_ref[3:4, :]
    scale = mod_ref[4:5, :]
    h = _rms(x, g_ref[2:3, :]) * (1.0 + scale) + shift
    hb = h.astype(BF16)

    q = _dot(hb, win_ref[:, 0:ATT_DIM]) * (QK_DIM ** -0.5 * LOG2_E)
    k = _dot(hb, win_ref[:, ATT_DIM:2 * ATT_DIM])
    v = _dot(hb, win_ref[:, 2 * ATT_DIM:3 * ATT_DIM])
    v_ref[...] = v.astype(v_ref.dtype)
    if rotary:
        cos = cos_ref[...]
        sin = sin_ref[...]
        lane = lax.broadcasted_iota(jnp.int32, (1, LANES), 1)
        lo_half = (lane % (2 * ROT_FREQS)) < ROT_FREQS
        for hd in range(N_HEADS):
            sl = slice(hd * V_DIM, (hd + 1) * V_DIM)
            q_ref[:, sl] = _rope(q[:, sl], cos, sin, lo_half).astype(q_ref.dtype)
            kt_ref[sl, :] = _rope(k[:, sl], cos, sin, lo_half).T.astype(kt_ref.dtype)
    else:
        q_ref[...] = q.astype(q_ref.dtype)
        for s in range(x.shape[0] // seq):
            rows = slice(s * seq, (s + 1) * seq)
            kt = k[rows, :].T
            kc_ref[s] = kt
            kt_ref[s] = kt.astype(kt_ref.dtype)
            for hd in range(N_HEADS):
                vc_ref[s, pl.ds(hd, seq, stride=N_HEADS), :] = v[rows, hd * V_DIM:(hd + 1) * V_DIM]

    c0 = 3 * ATT_DIM
    gates = _dot(hb, win_ref[:, c0:c0 + 3 * CONV_DIM])
    gb_ref[...] = gates[:, 0:CONV_DIM]
    u_ref[...] = gates[:, CONV_DIM:2 * CONV_DIM] * gates[:, 2 * CONV_DIM:3 * CONV_DIM]
    f = _dot(hb, win_ref[:, c0 + 3 * CONV_DIM:IN_DIM])
    xcs_ref[...] = _dot(f.astype(BF16), cdft_ref[...]).astype(BF16)


def _mixer_in(x, mods, norm_g, win, cdft, rope_tabs, caches, *, layer, mod_row_of_tile, tm, seq):
    n = x.shape[0]
    rotary = rope_tabs is not None
    in_specs = [
        pl.BlockSpec((tm, D_MODEL), lambda i: (i, 0)),
        pl.BlockSpec((None, None, N_MOD, D_MODEL), lambda i: (layer, mod_row_of_tile(i), 0, 0)),
        pl.BlockSpec((None, 6, D_MODEL), lambda i: (layer, 0, 0)),
        _resident((None, D_MODEL, IN_DIM), lambda i: (layer, 0, 0)),
        _resident((FOURIER_DIM, 2 * FOURIER_DIM), lambda i: (0, 0)),
    ]
    args = [x, mods, norm_g, win, cdft]
    if rotary:
        tiles_per_seq = rope_tabs[0].shape[0] // tm
        for tab in rope_tabs:
            in_specs.append(pl.BlockSpec((tm, LANES), lambda i: (i % tiles_per_seq, 0)))
            args.append(tab)

    def rows(width):
        return pl.BlockSpec((tm, width), lambda i: (i, 0))

    batch = n // seq
    if tm <= seq:
        tiles_per_seq = seq // tm
        kt_spec = pl.BlockSpec((None, ATT_DIM, tm), lambda i: (i // tiles_per_seq, 0, i % tiles_per_seq))
    else:
        kt_spec = pl.BlockSpec((tm // seq, ATT_DIM, seq), lambda i: (i, 0, 0))
    out_shape = [
        jax.ShapeDtypeStruct((n, ATT_DIM), BF16),
        jax.ShapeDtypeStruct((batch, ATT_DIM, seq), BF16),
        jax.ShapeDtypeStruct((n, ATT_DIM), BF16),
        jax.ShapeDtypeStruct((n, CONV_DIM), F32),
        jax.ShapeDtypeStruct((n, CONV_DIM), F32),
        jax.ShapeDtypeStruct((n, 2 * FOURIER_DIM), BF16),
    ]
    out_specs = [rows(ATT_DIM), kt_spec, rows(ATT_DIM), rows(CONV_DIM), rows(CONV_DIM),
                 rows(2 * FOURIER_DIM)]
    aliases = {}
    if not rotary:
        seqs = tm // seq
        out_shape += [jax.ShapeDtypeStruct((batch, DEPTH, ATT_DIM, seq), F32),
                      jax.ShapeDtypeStruct((batch, DEPTH, N_HEADS * seq, V_DIM), F32)]
        out_specs += [pl.BlockSpec((seqs, None, ATT_DIM, seq), lambda i: (i, layer, 0, 0)),
                      pl.BlockSpec((seqs, None, N_HEADS * seq, V_DIM), lambda i: (i, layer, 0, 0))]
        if caches is not None:
            first_cache_out = len(out_shape) - 2
            for offset, arr in enumerate(caches):
                aliases[len(args)] = first_cache_out + offset
                in_specs.append(pl.BlockSpec(memory_space=pl.ANY))
                args.append(arr)

    return pl.pallas_call(
        functools.partial(_mixer_in_kernel, rotary=rotary, seq=seq),
        out_shape=out_shape,
        grid=(n // tm,),
        in_specs=in_specs,
        out_specs=out_specs,
        input_output_aliases=aliases,
        compiler_params=_params(("arbitrary",)),
        name="mixer_in",
    )(*args)


def _lambda(lam_ref, lambda_init):
    lq = lam_ref[...]
    s01 = jnp.sum(lq[0:1, :] * lq[1:2, :], axis=-1, keepdims=True)
    s23 = jnp.sum(lq[2:3, :] * lq[3:4, :], axis=-1, keepdims=True)
    return jnp.exp(s01) - jnp.exp(s23) + lambda_init


def _attention_kernel(*refs, has_ctx, lambda_init, nb):
    if has_ctx:
        q_ref, kt_ref, v_ref, ckt_ref, cv_ref, lam_ref, sg_ref, o_ref = refs
    else:
        q_ref, kt_ref, v_ref, lam_ref, sg_ref, o_ref = refs
    lam = _lambda(lam_ref, lambda_init)
    lane = lax.broadcasted_iota(jnp.int32, (1, V_DIM), 1)
    first_map = lane < QK_DIM
    sub_g = sg_ref[...] * (1.0 - lambda_init)

    def scores(b, hd):
        sl = slice(hd * V_DIM, (hd + 1) * V_DIM)
        q = q_ref[b, :, sl]
        zero = jnp.zeros_like(q)
        kt_sets = [kt_ref[b, sl, :]]
        if has_ctx:
            kt_sets.append(ckt_ref[sl, :].astype(BF16))
        out = []
        for keep in (first_map, jnp.logical_not(first_map)):
            qm = jnp.where(keep, q, zero)
            out.append([_dot(qm, kt) for kt in kt_sets])
        return out

    def finish(b, hd, s_maps):
        sl = slice(hd * V_DIM, (hd + 1) * V_DIM)
        v_sets = [v_ref[b, :, sl]]
        if has_ctx:
            past = ckt_ref.shape[1]
            v_sets.append(cv_ref[pl.ds(hd, past, stride=N_HEADS), :].astype(BF16))
        exps, sums = [], []
        for s in s_maps:
            m = functools.reduce(jnp.maximum, [jnp.max(t, axis=-1, keepdims=True) for t in s])
            e = [jnp.exp2(t - m) for t in s]
            exps.append(e)
            sums.append(functools.reduce(jnp.add, [jnp.sum(t, axis=-1, keepdims=True) for t in e]))
        ratio = lam * sums[0] / sums[1]
        o = functools.reduce(jnp.add, [
            _dot((e0 - ratio * e1).astype(BF16), vs) for e0, e1, vs in zip(exps[0], exps[1], v_sets)])
        o = o * (1.0 / sums[0])
        o_ref[b, :, sl] = (_rms(o, sub_g)).astype(o_ref.dtype)

    items = [(b, hd) for b in range(nb) for hd in range(N_HEADS)]
    pending = scores(*items[0])
    for idx, item in enumerate(items):
        upcoming = scores(*items[idx + 1]) if idx + 1 < len(items) else None
        finish(*item, pending)
        pending = upcoming


def _attention(q, kt, v, ctx_kt, ctx_v, lam_qk, subln_g, *, layer, tq, nb):
    b, n, _ = q.shape
    has_ctx = ctx_kt is not None
    assert not has_ctx or nb == 1
    in_specs = [
        pl.BlockSpec((nb, tq, ATT_DIM), lambda i, j: (i, j, 0)),
        pl.BlockSpec((nb, ATT_DIM, n), lambda i, j: (i, 0, 0)),
        pl.BlockSpec((nb, n, ATT_DIM), lambda i, j: (i, 0, 0)),
    ]
    args = [q, kt, v]
    if has_ctx:
        for arr in (ctx_kt, ctx_v):
            in_specs.append(pl.BlockSpec((None, None) + arr.shape[2:], lambda i, j: (i, layer, 0, 0)))
            args.append(arr)
    in_specs += [
        pl.BlockSpec((None, 4, QK_DIM), lambda i, j: (layer, 0, 0)),
        pl.BlockSpec((None, 1, V_DIM), lambda i, j: (layer, 0, 0)),
    ]
    args += [lam_qk, subln_g.reshape(DEPTH, 1, V_DIM)]
    return pl.pallas_call(
        functools.partial(_attention_kernel, has_ctx=has_ctx, lambda_init=_lambda_init(layer), nb=nb),
        out_shape=jax.ShapeDtypeStruct((b, n, ATT_DIM), BF16),
        grid=(b // nb, n // tq),
        in_specs=in_specs,
        out_specs=pl.BlockSpec((nb, tq, ATT_DIM), lambda i, j: (i, j, 0)),
        compiler_params=_params(("arbitrary", "arbitrary")),
        name="attention",
    )(*args)


def _mixer_out_kernel(x_ref, att_ref, gb_ref, u_ref, xcs_ref, dft_ref, cw_ref, mod_ref, g_ref, wout_ref,
                      o_ref, *, seqs, tr, n):
    r0 = pl.multiple_of(pl.program_id(1) * tr, tr)
    gate = mod_ref[5:6, :]
    cw = cw_ref[...]
    row = lax.broadcasted_iota(jnp.int32, (tr, 1), 0)
    four_scale = (n * FOURIER_GROUP_DIM) ** -0.5
    for s in range(seqs):
        four = (_dot(dft_ref[:, 0:n], xcs_ref[s, :, 0:FOURIER_DIM])
                + _dot(dft_ref[:, n:2 * n], xcs_ref[s, :, FOURIER_DIM:2 * FOURIER_DIM])) * four_scale
        u = u_ref[s, pl.ds(r0, tr), :]
        before_start = pl.multiple_of(jnp.maximum(r0 - SUBLANES, 0), SUBLANES)
        after_start = pl.multiple_of(jnp.minimum(r0 + tr, n - SUBLANES), SUBLANES)
        before = u_ref[s, pl.ds(before_start, SUBLANES), :][SUBLANES - 1:SUBLANES, :]
        after = u_ref[s, pl.ds(after_start, SUBLANES), :][0:1, :]
        before = jnp.where(r0 > 0, before, 0.0)
        after = jnp.where(r0 + tr < n, after, 0.0)
        prev = jnp.where(row == 0, before, pltpu.roll(u, 1, 0))
        nxt = jnp.where(row == tr - 1, after, pltpu.roll(u, tr - 1, 0))
        conv = gb_ref[s] * (prev * cw[0:1, :] + u * cw[1:2, :] + nxt * cw[2:3, :])
        y = (_dot(att_ref[s], wout_ref[0:ATT_DIM, :])
             + _dot(conv.astype(BF16), wout_ref[ATT_DIM:ATT_DIM + CONV_DIM, :])
             + _dot(four.astype(BF16), wout_ref[ATT_DIM + CONV_DIM:D_MODEL, :]))
        o_ref[s] = x_ref[s] + gate * _rms(y, g_ref[3:4, :])


def _mixer_out(x, att, gb, u, xcs, dft, conv_w, mods, norm_g, wout, *, layer, mod_row_of_batch, seqs, tr):
    b, n, _ = x.shape

    def tile(width):
        return pl.BlockSpec((seqs, tr, width), lambda i, j: (i, j, 0))

    def whole(width):
        return pl.BlockSpec((seqs, n, width), lambda i, j: (i, 0, 0))

    return pl.pallas_call(
        functools.partial(_mixer_out_kernel, seqs=seqs, tr=tr, n=n),
        out_shape=jax.ShapeDtypeStruct((b, n, D_MODEL), F32),
        grid=(b // seqs, n // tr),
        in_specs=[
            tile(D_MODEL), tile(ATT_DIM), tile(CONV_DIM), whole(CONV_DIM), whole(2 * FOURIER_DIM),
            pl.BlockSpec((tr, 2 * n), lambda i, j: (j, 0)),
            pl.BlockSpec((None, 3, CONV_DIM), lambda i, j: (layer, 0, 0)),
            pl.BlockSpec((None, None, N_MOD, D_MODEL), lambda i, j: (layer, mod_row_of_batch(i), 0, 0)),
            pl.BlockSpec((None, 6, D_MODEL), lambda i, j: (layer, 0, 0)),
            _resident((None, D_MODEL, D_MODEL), lambda i, j: (layer, 0, 0)),
        ],
        out_specs=tile(D_MODEL),
        compiler_params=_params(("arbitrary", "arbitrary")),
        name="mixer_out",
    )(x, att, gb, u, xcs, dft, conv_w, mods, norm_g, wout)


def _position_dft(n):
    jk = (np.arange(n, dtype=np.int64)[:, None] * np.arange(n, dtype=np.int64)[None, :]) % n
    ang = 2.0 * np.pi * jk.astype(np.float64) / n
    return np.concatenate([np.cos(ang), -np.sin(ang)], axis=1).astype(np.float32)


def _channel_dft():
    g = FOURIER_GROUP_DIM
    jk = (np.arange(g)[:, None] * np.arange(g)[None, :]) % g
    ang = 2.0 * np.pi * jk.astype(np.float64) / g
    eye = np.eye(FOURIER_DIM // g)
    return np.concatenate([np.kron(eye, np.cos(ang)), np.kron(eye, np.sin(ang))], axis=1).astype(np.float32)


def _rope_tables(n):
    t = jnp.arange(n)
    row = (t // GRID_W).astype(F32)
    col = (t % GRID_W).astype(F32)
    inv = 1.0 / (ROPE_THETA ** (jnp.arange(ROT_FREQS, dtype=F32) / ROT_FREQS))
    ang_r = row[:, None] * inv
    ang_c = col[:, None] * inv
    cos = jnp.concatenate([jnp.cos(ang_r)] * 2 + [jnp.cos(ang_c)] * 2, axis=-1)
    sin = jnp.concatenate([-jnp.sin(ang_r), jnp.sin(ang_r), -jnp.sin(ang_c), jnp.sin(ang_c)], axis=-1)
    return jnp.tile(cos, (1, 2)), jnp.tile(sin, (1, 2))


def _run_path(x, mods, mod_row, weights, ctx, rope_tabs, dft, *, tm, tq, seqs, tr):
    norm_g, wgu, wd, win, wout, cdft, conv_w, lam_qk, subln_g = weights
    b, n, _ = x.shape
    tiles_per_seq = max(n // tm, 1)
    seqs_per_tile = max(tm // n, 1)

    def mod_row_of_tile(i):
        return mod_row(i * seqs_per_tile // tiles_per_seq)

    def shape3(a):
        return a.reshape(b, n, a.shape[-1])

    ctx_k, ctx_v = ctx if ctx is not None else (None, None)
    caches = None
    for l in range(DEPTH):
        flat = x.reshape(b * n, D_MODEL)
        flat = _ffn(flat, mods, norm_g, wgu, wd, layer=l, half=0, mod_row_of_tile=mod_row_of_tile, tm=tm)
        q, kt, v, gb, u, xcs, *caches = _mixer_in(flat, mods, norm_g, win, cdft, rope_tabs, caches, layer=l,
                                                  mod_row_of_tile=mod_row_of_tile, tm=tm, seq=n)
        caches = caches or None
        att = _attention(shape3(q), kt, shape3(v), ctx_k, ctx_v, lam_qk, subln_g, layer=l, tq=tq, nb=seqs)
        x = _mixer_out(shape3(flat), att, shape3(gb), shape3(u), shape3(xcs), dft, conv_w, mods, norm_g,
                       wout, layer=l, mod_row_of_batch=lambda i: mod_row(i * seqs), seqs=seqs, tr=tr)
        flat = _ffn(x.reshape(b * n, D_MODEL), mods, norm_g, wgu, wd, layer=l, half=1,
                    mod_row_of_tile=mod_row_of_tile, tm=tm)
        x = flat.reshape(b, n, D_MODEL)
    return x, caches


def kernel(x_prompt, x_sample, cache_k, cache_v, c, c_ctx, w_mod, b_mod, norm_g, w_ffn_gu, w_ffn_down,
           w_in, w_out, conv_w, lam_qk, subln_g):
    batch, seq, _ = x_prompt.shape
    dec_batch, dec_seq, _ = x_sample.shape
    past = cache_k.shape[2]

    cvecs = jnp.zeros((MOD_ROWS, D_MODEL), F32).at[0].set(c_ctx).at[1:1 + dec_batch].set(c)
    mods = _modulation(cvecs, w_mod, b_mod)

    wgu = _regroup_gu(w_ffn_gu)
    wd = _cast_bf16(w_ffn_down.reshape(DEPTH * 2 * D_FF, D_MODEL), D_FF // 2)
    wd = wd.reshape(DEPTH, 2, N_FF_CHUNKS, FF_CHUNK, D_MODEL)
    win = _cast_bf16(w_in.reshape(DEPTH * D_MODEL, IN_DIM), D_MODEL // 2).reshape(DEPTH, D_MODEL, IN_DIM)
    wout = _cast_bf16(w_out.reshape(DEPTH * D_MODEL, D_MODEL), D_MODEL).reshape(DEPTH, D_MODEL, D_MODEL)
    cdft = jnp.asarray(_channel_dft()).astype(BF16)
    weights = (norm_g, wgu, wd, win, wout, cdft, conv_w, lam_qk, subln_g)

    dft_ctx = jnp.asarray(_position_dft(seq)).astype(BF16)
    y_prompt, (kc, vc) = _run_path(x_prompt, mods, lambda i: 0, weights, None, None, dft_ctx,
                                   tm=1024, tq=seq, seqs=4, tr=seq)
    new_cache_k = kc.reshape(batch, DEPTH, N_HEADS, 2, QK_DIM, seq).transpose(0, 1, 5, 2, 3, 4)
    new_cache_v = vc.reshape(batch, DEPTH, seq, N_HEADS, V_DIM)

    dft_lat = jnp.asarray(_position_dft(dec_seq)).astype(BF16)
    ctx = (cache_k.transpose(0, 1, 3, 4, 5, 2).reshape(dec_batch, DEPTH, ATT_DIM, past),
           cache_v.reshape(dec_batch, DEPTH, past * N_HEADS, V_DIM))
    y_sample, _ = _run_path(x_sample, mods, lambda i: 1 + i, weights, ctx, _rope_tables(dec_seq), dft_lat,
                            tm=1024, tq=256, seqs=1, tr=512)
    return (y_prompt, y_sample, new_cache_k, new_cache_v)
```

```python
import functools
import math

import jax
import jax.numpy as jnp
import numpy as np
from jax import lax
from jax.experimental import pallas as pl
from jax.experimental.pallas import tpu as pltpu

D_MODEL = 1024
DEPTH = 2
GRID_W = 64
V_DIM = 128
QK_DIM = V_DIM // 2
N_HEADS = 4
ATT_DIM = N_HEADS * V_DIM
CONV_DIM = 256
FOURIER_DIM = 256
FOURIER_GROUP_DIM = 64
IN_DIM = 3 * ATT_DIM + 3 * CONV_DIM + FOURIER_DIM
D_FF = 2816
N_MOD = 9
ROT_FREQS = 16
ROPE_THETA = 10000.0
EPS = 1e-6
FFN_RESIDUAL_WEIGHT = 0.5
LOG2_E = math.log2(math.e)

FF_CHUNK = 256
N_FF_CHUNKS = D_FF // FF_CHUNK
FF_GROUP = 3
assert (N_FF_CHUNKS - 2) % FF_GROUP == 0
MOD_ROWS = 8
SUBLANES = 8
LANES = 128
VMEM_LIMIT = 56 * 1024 * 1024

F32 = jnp.float32
BF16 = jnp.bfloat16


def _lambda_init(layer_idx):
    return 0.8 - 0.6 * math.exp(-0.3 * layer_idx)


def _rms(x, g):
    return x * lax.rsqrt(jnp.mean(x * x, axis=-1, keepdims=True) + EPS) * g


def _dot(a, b):
    return jnp.dot(a, b, preferred_element_type=F32)


def _params(semantics):
    return pltpu.CompilerParams(dimension_semantics=semantics, vmem_limit_bytes=VMEM_LIMIT)


def _resident(block_shape, index_map):
    return pl.BlockSpec(block_shape, index_map, pipeline_mode=pl.Buffered(1))


def _cast_kernel(x_ref, o_ref):
    o_ref[...] = x_ref[...].astype(o_ref.dtype)


def _cast_bf16(x, block_rows):
    rows, cols = x.shape
    return pl.pallas_call(
        _cast_kernel,
        out_shape=jax.ShapeDtypeStruct((rows, cols), BF16),
        grid=(rows // block_rows,),
        in_specs=[pl.BlockSpec((block_rows, cols), lambda i: (i, 0))],
        out_specs=pl.BlockSpec((block_rows, cols), lambda i: (i, 0)),
        compiler_params=_params(("arbitrary",)),
        name="cast_bf16",
    )(x)


def _regroup_gu_kernel(g_ref, u_ref, o_ref):
    for c in range(N_FF_CHUNKS):
        cols = slice(c * FF_CHUNK, (c + 1) * FF_CHUNK)
        o_ref[c, :, 0:FF_CHUNK] = g_ref[:, cols].astype(BF16)
        o_ref[c, :, FF_CHUNK:2 * FF_CHUNK] = u_ref[:, cols].astype(BF16)


def _regroup_gu(w_gu, block_rows=256):
    def half(part):
        return pl.BlockSpec((None, None, block_rows, D_FF), lambda l, s, r: (l, s, r, part))

    return pl.pallas_call(
        _regroup_gu_kernel,
        out_shape=jax.ShapeDtypeStruct((DEPTH, 2, N_FF_CHUNKS, D_MODEL, 2 * FF_CHUNK), BF16),
        grid=(DEPTH, 2, D_MODEL // block_rows),
        in_specs=[half(0), half(1)],
        out_specs=pl.BlockSpec((None, None, N_FF_CHUNKS, block_rows, 2 * FF_CHUNK),
                               lambda l, s, r: (l, s, 0, r, 0)),
        compiler_params=_params(("arbitrary",) * 3),
        name="regroup_gu",
    )(w_gu, w_gu)


def _mod_kernel(c_ref, w_ref, b_ref, o_ref):
    c = c_ref[...]
    s = (c / (1.0 + jnp.exp(-c))).astype(BF16)
    o_ref[...] = _dot(s, w_ref[...].astype(BF16)) + b_ref[...]


def _modulation(cvecs, w_mod, b_mod):
    col = D_MODEL
    out = pl.pallas_call(
        _mod_kernel,
        out_shape=jax.ShapeDtypeStruct((DEPTH, MOD_ROWS, N_MOD * D_MODEL), F32),
        grid=(DEPTH, N_MOD * D_MODEL // col),
        in_specs=[
            pl.BlockSpec((MOD_ROWS, D_MODEL), lambda l, j: (0, 0)),
            pl.BlockSpec((None, D_MODEL, col), lambda l, j: (l, 0, j)),
            pl.BlockSpec((None, 1, col), lambda l, j: (l, 0, j)),
        ],
        out_specs=pl.BlockSpec((None, MOD_ROWS, col), lambda l, j: (l, 0, j)),
        compiler_params=_params(("arbitrary", "arbitrary")),
        name="modulation",
    )(cvecs, w_mod, b_mod.reshape(DEPTH, 1, N_MOD * D_MODEL))
    return out.reshape(DEPTH, MOD_ROWS, N_MOD, D_MODEL)


def _ffn_kernel(x_ref, mod_ref, g_ref, wgu_ref, wd_ref, o_ref, acc_ref, *, sub, g_row):
    x = x_ref[...]
    shift = mod_ref[3 * sub:3 * sub + 1, :]
    scale = mod_ref[3 * sub + 1:3 * sub + 2, :]
    gate = mod_ref[3 * sub + 2:3 * sub + 3, :]
    h = _rms(x, g_ref[g_row:g_row + 1, :] * (1.0 + scale)) + shift
    hb = h.astype(BF16)

    def hidden(c):
        gu = _dot(hb, wgu_ref[c])
        g = gu[:, :FF_CHUNK]
        u = gu[:, FF_CHUNK:]
        return (g / (1.0 + jnp.exp(-g)) * u).astype(BF16)

    acc_ref[...] = _dot(hidden(0), wd_ref[0])

    def group(i, carry):
        c0 = 1 + i * FF_GROUP
        a = jnp.concatenate([hidden(c0 + k) for k in range(FF_GROUP)], axis=-1)
        wd = wd_ref[pl.ds(c0, FF_GROUP)].reshape(FF_GROUP * FF_CHUNK, D_MODEL)
        acc_ref[...] += _dot(a, wd)
        return carry

    lax.fori_loop(0, (N_FF_CHUNKS - 2) // FF_GROUP, group, 0)
    last = N_FF_CHUNKS - 1
    y = acc_ref[...] + _dot(hidden(last), wd_ref[last])
    o_ref[...] = x + _rms(y, g_ref[g_row + 1:g_row + 2, :] * (FFN_RESIDUAL_WEIGHT * gate))


def _ffn(x, mods, norm_g, wgu, wd, *, layer, half, mod_row_of_tile, tm):
    n = x.shape[0]
    sub = 2 * half
    g_row = 4 * half
    return pl.pallas_call(
        functools.partial(_ffn_kernel, sub=sub, g_row=g_row),
        out_shape=jax.ShapeDtypeStruct((n, D_MODEL), F32),
        grid=(n // tm,),
        in_specs=[
            pl.BlockSpec((tm, D_MODEL), lambda i: (i, 0)),
            pl.BlockSpec((None, None, N_MOD, D_MODEL), lambda i: (layer, mod_row_of_tile(i), 0, 0)),
            pl.BlockSpec((None, 6, D_MODEL), lambda i: (layer, 0, 0)),
            _resident((None, None, N_FF_CHUNKS, D_MODEL, 2 * FF_CHUNK), lambda i: (layer, half, 0, 0, 0)),
            _resident((None, None, N_FF_CHUNKS, FF_CHUNK, D_MODEL), lambda i: (layer, half, 0, 0, 0)),
        ],
        out_specs=pl.BlockSpec((tm, D_MODEL), lambda i: (i, 0)),
        scratch_shapes=[pltpu.VMEM((tm, D_MODEL), F32)],
        compiler_params=_params(("arbitrary",)),
        name="ffn",
    )(x, mods, norm_g, wgu, wd)


def _rope(x, cos, sin_signed, lo_half):
    partner = jnp.where(lo_half, pltpu.roll(x, LANES - ROT_FREQS, 1), pltpu.roll(x, ROT_FREQS, 1))
    return x * cos + partner * sin_signed


def _mixer_in_kernel(*refs, rotary, seq, cache_slot):
    x_ref, mod_ref, g_ref, win_ref, cdft_ref = refs[:5]
    if rotary:
        cos_ref, sin_ref = refs[5:7]
        q_ref, kt_ref, v_ref, gb_ref, u_ref, xcs_ref = refs[7:]
    else:
        q_ref, kt_ref, v_ref, gb_ref, u_ref, xcs_ref, kc_ref, vc_ref = refs[-8:]
    x = x_ref[...]
    shift = mod_ref[3:4, :]
    scale = mod_ref[4:5, :]
    h = _rms(x, g_ref[2:3, :]) * (1.0 + scale) + shift
    hb = h.astype(BF16)

    q = _dot(hb, win_ref[:, 0:ATT_DIM]) * (QK_DIM ** -0.5 * LOG2_E)
    k = _dot(hb, win_ref[:, ATT_DIM:2 * ATT_DIM])
    v = _dot(hb, win_ref[:, 2 * ATT_DIM:3 * ATT_DIM])
    v_ref[...] = v.astype(v_ref.dtype)
    if rotary:
        cos = cos_ref[...]
        sin = sin_ref[...]
        lane = lax.broadcasted_iota(jnp.int32, (1, LANES), 1)
        lo_half = (lane % (2 * ROT_FREQS)) < ROT_FREQS
        for hd in range(N_HEADS):
            sl = slice(hd * V_DIM, (hd + 1) * V_DIM)
            q_ref[:, sl] = _rope(q[:, sl], cos, sin, lo_half).astype(q_ref.dtype)
            kt_ref[sl, :] = _rope(k[:, sl], cos, sin, lo_half).T.astype(kt_ref.dtype)
    else:
        q_ref[...] = q.astype(q_ref.dtype)
        for s in range(x.shape[0] // seq):
            rows = slice(s * seq, (s + 1) * seq)
            kt = k[rows, :].T
            kt_ref[s] = kt.astype(kt_ref.dtype)
            if cache_slot is None:
                kc_out, vc_out = kc_ref.at[s], vc_ref.at[s]
            else:
                kc_out, vc_out = kc_ref.at[s, cache_slot], vc_ref.at[s, cache_slot]
                for other in range(DEPTH):
                    if other != cache_slot:
                        kc_ref[s, other] = jnp.zeros(kc_ref.shape[2:], F32)
                        vc_ref[s, other] = jnp.zeros(vc_ref.shape[2:], F32)
            kc_out[...] = kt
            for hd in range(N_HEADS):
                vc_out[pl.ds(hd, seq, stride=N_HEADS), :] = v[rows, hd * V_DIM:(hd + 1) * V_DIM]

    c0 = 3 * ATT_DIM
    gates = _dot(hb, win_ref[:, c0:c0 + 3 * CONV_DIM])
    gb_ref[...] = gates[:, 0:CONV_DIM]
    u_ref[...] = gates[:, CONV_DIM:2 * CONV_DIM] * gates[:, 2 * CONV_DIM:3 * CONV_DIM]
    f = _dot(hb, win_ref[:, c0 + 3 * CONV_DIM:IN_DIM])
    xcs_ref[...] = _dot(f.astype(BF16), cdft_ref[...]).astype(BF16)


def _mixer_in(x, mods, norm_g, win, cdft, rope_tabs, caches, *, layer, mod_row_of_tile, tm, seq):
    n = x.shape[0]
    rotary = rope_tabs is not None
    in_specs = [
        pl.BlockSpec((tm, D_MODEL), lambda i: (i, 0)),
        pl.BlockSpec((None, None, N_MOD, D_MODEL), lambda i: (layer, mod_row_of_tile(i), 0, 0)),
        pl.BlockSpec((None, 6, D_MODEL), lambda i: (layer, 0, 0)),
        _resident((None, D_MODEL, IN_DIM), lambda i: (layer, 0, 0)),
        _resident((FOURIER_DIM, 2 * FOURIER_DIM), lambda i: (0, 0)),
    ]
    args = [x, mods, norm_g, win, cdft]
    if rotary:
        tiles_per_seq = rope_tabs[0].shape[0] // tm
        for tab in rope_tabs:
            in_specs.append(pl.BlockSpec((tm, LANES), lambda i: (i % tiles_per_seq, 0)))
            args.append(tab)

    def rows(width):
        return pl.BlockSpec((tm, width), lambda i: (i, 0))

    batch = n // seq
    if tm <= seq:
        tiles_per_seq = seq // tm
        kt_spec = pl.BlockSpec((None, ATT_DIM, tm), lambda i: (i // tiles_per_seq, 0, i % tiles_per_seq))
    else:
        kt_spec = pl.BlockSpec((tm // seq, ATT_DIM, seq), lambda i: (i, 0, 0))
    out_shape = [
        jax.ShapeDtypeStruct((n, ATT_DIM), BF16),
        jax.ShapeDtypeStruct((batch, ATT_DIM, seq), BF16),
        jax.ShapeDtypeStruct((n, ATT_DIM), BF16),
        jax.ShapeDtypeStruct((n, CONV_DIM), F32),
        jax.ShapeDtypeStruct((n, CONV_DIM), F32),
        jax.ShapeDtypeStruct((n, 2 * FOURIER_DIM), BF16),
    ]
    out_specs = [rows(ATT_DIM), kt_spec, rows(ATT_DIM), rows(CONV_DIM), rows(CONV_DIM),
                 rows(2 * FOURIER_DIM)]
    aliases = {}
    if not rotary:
        seqs = tm // seq
        out_shape += [jax.ShapeDtypeStruct((batch, DEPTH, ATT_DIM, seq), F32),
                      jax.ShapeDtypeStruct((batch, DEPTH, N_HEADS * seq, V_DIM), F32)]
        if caches is None:
            out_specs += [pl.BlockSpec((seqs, DEPTH, ATT_DIM, seq), lambda i: (i, 0, 0, 0)),
                          pl.BlockSpec((seqs, DEPTH, N_HEADS * seq, V_DIM), lambda i: (i, 0, 0, 0))]
        else:
            out_specs += [pl.BlockSpec((seqs, None, ATT_DIM, seq), lambda i: (i, layer, 0, 0)),
                          pl.BlockSpec((seqs, None, N_HEADS * seq, V_DIM), lambda i: (i, layer, 0, 0))]
            first_cache_out = len(out_shape) - 2
            for offset, arr in enumerate(caches):
                aliases[len(args)] = first_cache_out + offset
                in_specs.append(pl.BlockSpec(memory_space=pl.ANY))
                args.append(arr)

    cache_slot = layer if (not rotary and caches is None) else None
    return pl.pallas_call(
        functools.partial(_mixer_in_kernel, rotary=rotary, seq=seq, cache_slot=cache_slot),
        out_shape=out_shape,
        grid=(n // tm,),
        in_specs=in_specs,
        out_specs=out_specs,
        input_output_aliases=aliases,
        compiler_params=_params(("arbitrary",)),
        name="mixer_in",
    )(*args)


def _lambda(lam_ref, lambda_init):
    lq = lam_ref[...]
    s01 = jnp.sum(lq[0:1, :] * lq[1:2, :], axis=-1, keepdims=True)
    s23 = jnp.sum(lq[2:3, :] * lq[3:4, :], axis=-1, keepdims=True)
    return jnp.exp(s01) - jnp.exp(s23) + lambda_init


def _attention_kernel(*refs, has_ctx, lambda_init, nb, n):
    if has_ctx:
        q_ref, kt_ref, v_ref, ckt_ref, cv_ref, lam_ref, sg_ref, o_ref, kt_sc, vext_sc = refs
    else:
        q_ref, kt_ref, v_ref, lam_ref, sg_ref, o_ref, vext_sc = refs
        kt_sc = kt_ref
    lam = _lambda(lam_ref, lambda_init)
    lane = lax.broadcasted_iota(jnp.int32, (1, V_DIM), 1)
    first_map = lane < QK_DIM
    sub_g = sg_ref[...] * (1.0 - lambda_init)

    @pl.when(pl.program_id(1) == 0)
    def _():
        for b in range(nb):
            if has_ctx:
                kt_sc[b, :, 0:n] = kt_ref[b]
                kt_sc[b, :, n:] = ckt_ref[...].astype(BF16)
            for hd in range(N_HEADS):
                sl = slice(hd * V_DIM, (hd + 1) * V_DIM)
                vext_sc[b, hd, 0:n, 0:V_DIM] = v_ref[b, :, sl]
                if has_ctx:
                    past = ckt_ref.shape[1]
                    vext_sc[b, hd, n:, 0:V_DIM] = cv_ref[pl.ds(hd, past, stride=N_HEADS), :].astype(BF16)
                vext_sc[b, hd, :, V_DIM:] = jnp.ones((vext_sc.shape[2], V_DIM), BF16)

    def exponentials(b, hd):
        sl = slice(hd * V_DIM, (hd + 1) * V_DIM)
        q = q_ref[b, :, sl]
        zero = jnp.zeros_like(q)
        out = []
        for keep in (first_map, jnp.logical_not(first_map)):
            qm = jnp.where(keep, q, zero)
            s = _dot(qm, kt_sc[b, sl, :])
            m = jnp.max(s, axis=-1, keepdims=True)
            out.append(jnp.exp2((s - m).astype(BF16)))
        return out

    def finish(b, hd, e_maps):
        sl = slice(hd * V_DIM, (hd + 1) * V_DIM)
        vext = vext_sc[b, hd]
        r0, r1 = [_dot(e, vext) for e in e_maps]
        o = r0[:, :V_DIM] / r0[:, V_DIM:] - lam * (r1[:, :V_DIM] / r1[:, V_DIM:])
        o_ref[b, :, sl] = (_rms(o, sub_g)).astype(o_ref.dtype)

    items = [(b, hd) for b in range(nb) for hd in range(N_HEADS)]
    pending = exponentials(*items[0])
    for idx, item in enumerate(items):
        upcoming = exponentials(*items[idx + 1]) if idx + 1 < len(items) else None
        finish(*item, pending)
        pending = upcoming


def _attention(q, kt, v, ctx_kt, ctx_v, lam_qk, subln_g, *, layer, tq, nb):
    b, n, _ = q.shape
    has_ctx = ctx_kt is not None
    assert not has_ctx or nb == 1
    in_specs = [
        pl.BlockSpec((nb, tq, ATT_DIM), lambda i, j: (i, j, 0)),
        pl.BlockSpec((nb, ATT_DIM, n), lambda i, j: (i, 0, 0)),
        pl.BlockSpec((nb, n, ATT_DIM), lambda i, j: (i, 0, 0)),
    ]
    args = [q, kt, v]
    if has_ctx:
        for arr in (ctx_kt, ctx_v):
            in_specs.append(pl.BlockSpec((None, None) + arr.shape[2:], lambda i, j: (i, layer, 0, 0)))
            args.append(arr)
    in_specs += [
        pl.BlockSpec((None, 4, QK_DIM), lambda i, j: (layer, 0, 0)),
        pl.BlockSpec((None, 1, V_DIM), lambda i, j: (layer, 0, 0)),
    ]
    args += [lam_qk, subln_g.reshape(DEPTH, 1, V_DIM)]
    n_keys = n + (ctx_kt.shape[3] if has_ctx else 0)
    scratch = [pltpu.VMEM((nb, N_HEADS, n_keys, 2 * V_DIM), BF16)]
    if has_ctx:
        scratch.insert(0, pltpu.VMEM((nb, ATT_DIM, n_keys), BF16))
    return pl.pallas_call(
        functools.partial(_attention_kernel, has_ctx=has_ctx, lambda_init=_lambda_init(layer), nb=nb, n=n),
        out_shape=jax.ShapeDtypeStruct((b, n, ATT_DIM), BF16),
        grid=(b // nb, n // tq),
        in_specs=in_specs,
        out_specs=pl.BlockSpec((nb, tq, ATT_DIM), lambda i, j: (i, j, 0)),
        scratch_shapes=scratch,
        compiler_params=_params(("arbitrary", "arbitrary")),
        name="attention",
    )(*args)


def _mixer_out_kernel(x_ref, att_ref, gb_ref, u_ref, xcs_ref, dft_ref, cw_ref, mod_ref, g_ref, wout_ref,
                      o_ref, *, seqs, tr, n):
    r0 = pl.multiple_of(pl.program_id(1) * tr, tr)
    gate = mod_ref[5:6, :]
    cw = cw_ref[...]
    row = lax.broadcasted_iota(jnp.int32, (tr, 1), 0)
    four_scale = (n * FOURIER_GROUP_DIM) ** -0.5
    for s in range(seqs):
        four = (_dot(dft_ref[:, 0:n], xcs_ref[s, :, 0:FOURIER_DIM])
                + _dot(dft_ref[:, n:2 * n], xcs_ref[s, :, FOURIER_DIM:2 * FOURIER_DIM])) * four_scale
        u = u_ref[s, pl.ds(r0, tr), :]
        before_start = pl.multiple_of(jnp.maximum(r0 - SUBLANES, 0), SUBLANES)
        after_start = pl.multiple_of(jnp.minimum(r0 + tr, n - SUBLANES), SUBLANES)
        before = u_ref[s, pl.ds(before_start, SUBLANES), :][SUBLANES - 1:SUBLANES, :]
        after = u_ref[s, pl.ds(after_start, SUBLANES), :][0:1, :]
        before = jnp.where(r0 > 0, before, 0.0)
        after = jnp.where(r0 + tr < n, after, 0.0)
        prev = jnp.where(row == 0, before, pltpu.roll(u, 1, 0))
        nxt = jnp.where(row == tr - 1, after, pltpu.roll(u, tr - 1, 0))
        conv = gb_ref[s] * (prev * cw[0:1, :] + u * cw[1:2, :] + nxt * cw[2:3, :])
        y = (_dot(att_ref[s], wout_ref[0:ATT_DIM, :])
             + _dot(conv.astype(BF16), wout_ref[ATT_DIM:ATT_DIM + CONV_DIM, :])
             + _dot(four.astype(BF16), wout_ref[ATT_DIM + CONV_DIM:D_MODEL, :]))
        o_ref[s] = x_ref[s] + gate * _rms(y, g_ref[3:4, :])


def _mixer_out(x, att, gb, u, xcs, dft, conv_w, mods, norm_g, wout, *, layer, mod_row_of_batch, seqs, tr):
    b, n, _ = x.shape

    def tile(width):
        return pl.BlockSpec((seqs, tr, width), lambda i, j: (i, j, 0))

    def whole(width):
        return pl.BlockSpec((seqs, n, width), lambda i, j: (i, 0, 0))

    return pl.pallas_call(
        functools.partial(_mixer_out_kernel, seqs=seqs, tr=tr, n=n),
        out_shape=jax.ShapeDtypeStruct((b, n, D_MODEL), F32),
        grid=(b // seqs, n // tr),
        in_specs=[
            tile(D_MODEL), tile(ATT_DIM), tile(CONV_DIM), whole(CONV_DIM), whole(2 * FOURIER_DIM),
            pl.BlockSpec((tr, 2 * n), lambda i, j: (j, 0)),
            pl.BlockSpec((None, 3, CONV_DIM), lambda i, j: (layer, 0, 0)),
            pl.BlockSpec((None, None, N_MOD, D_MODEL), lambda i, j: (layer, mod_row_of_batch(i), 0, 0)),
            pl.BlockSpec((None, 6, D_MODEL), lambda i, j: (layer, 0, 0)),
            _resident((None, D_MODEL, D_MODEL), lambda i, j: (layer, 0, 0)),
        ],
        out_specs=tile(D_MODEL),
        compiler_params=_params(("arbitrary", "arbitrary")),
        name="mixer_out",
    )(x, att, gb, u, xcs, dft, conv_w, mods, norm_g, wout)


def _position_dft(n):
    jk = (np.arange(n, dtype=np.int64)[:, None] * np.arange(n, dtype=np.int64)[None, :]) % n
    ang = 2.0 * np.pi * jk.astype(np.float64) / n
    return np.concatenate([np.cos(ang), -np.sin(ang)], axis=1).astype(np.float32)


def _channel_dft():
    g = FOURIER_GROUP_DIM
    jk = (np.arange(g)[:, None] * np.arange(g)[None, :]) % g
    ang = 2.0 * np.pi * jk.astype(np.float64) / g
    eye = np.eye(FOURIER_DIM // g)
    return np.concatenate([np.kron(eye, np.cos(ang)), np.kron(eye, np.sin(ang))], axis=1).astype(np.float32)


def _rope_tables(n):
    t = jnp.arange(n)
    row = (t // GRID_W).astype(F32)
    col = (t % GRID_W).astype(F32)
    inv = 1.0 / (ROPE_THETA ** (jnp.arange(ROT_FREQS, dtype=F32) / ROT_FREQS))
    ang_r = row[:, None] * inv
    ang_c = col[:, None] * inv
    cos = jnp.concatenate([jnp.cos(ang_r)] * 2 + [jnp.cos(ang_c)] * 2, axis=-1)
    sin = jnp.concatenate([-jnp.sin(ang_r), jnp.sin(ang_r), -jnp.sin(ang_c), jnp.sin(ang_c)], axis=-1)
    return jnp.tile(cos, (1, 2)), jnp.tile(sin, (1, 2))


def _run_path(x, mods, mod_row, weights, ctx, rope_tabs, dft, *, tm, tq, seqs, tr):
    norm_g, wgu, wd, win, wout, cdft, conv_w, lam_qk, subln_g = weights
    b, n, _ = x.shape
    tiles_per_seq = max(n // tm, 1)
    seqs_per_tile = max(tm // n, 1)

    def mod_row_of_tile(i):
        return mod_row(i * seqs_per_tile // tiles_per_seq)

    def shape3(a):
        return a.reshape(b, n, a.shape[-1])

    ctx_k, ctx_v = ctx if ctx is not None else (None, None)
    caches = None
    for l in range(DEPTH):
        flat = x.reshape(b * n, D_MODEL)
        flat = _ffn(flat, mods, norm_g, wgu, wd, layer=l, half=0, mod_row_of_tile=mod_row_of_tile, tm=tm)
        q, kt, v, gb, u, xcs, *caches = _mixer_in(flat, mods, norm_g, win, cdft, rope_tabs, caches, layer=l,
                                                  mod_row_of_tile=mod_row_of_tile, tm=tm, seq=n)
        caches = caches or None
        att = _attention(shape3(q), kt, shape3(v), ctx_k, ctx_v, lam_qk, subln_g, layer=l, tq=tq, nb=seqs)
        x = _mixer_out(shape3(flat), att, shape3(gb), shape3(u), shape3(xcs), dft, conv_w, mods, norm_g,
                       wout, layer=l, mod_row_of_batch=lambda i: mod_row(i * seqs), seqs=seqs, tr=tr)
        flat = _ffn(x.reshape(b * n, D_MODEL), mods, norm_g, wgu, wd, layer=l, half=1,
                    mod_row_of_tile=mod_row_of_tile, tm=tm)
        x = flat.reshape(b, n, D_MODEL)
    return x, caches


def kernel(x_prompt, x_sample, cache_k, cache_v, c, c_ctx, w_mod, b_mod, norm_g, w_ffn_gu, w_ffn_down,
           w_in, w_out, conv_w, lam_qk, subln_g):
    batch, seq, _ = x_prompt.shape
    dec_batch, dec_seq, _ = x_sample.shape
    past = cache_k.shape[2]

    cvecs = jnp.zeros((MOD_ROWS, D_MODEL), F32).at[0].set(c_ctx).at[1:1 + dec_batch].set(c)
    mods = _modulation(cvecs, w_mod, b_mod)

    wgu = _regroup_gu(w_ffn_gu)
    wd = _cast_bf16(w_ffn_down.reshape(DEPTH * 2 * D_FF, D_MODEL), D_FF // 2)
    wd = wd.reshape(DEPTH, 2, N_FF_CHUNKS, FF_CHUNK, D_MODEL)
    win = _cast_bf16(w_in.reshape(DEPTH * D_MODEL, IN_DIM), D_MODEL // 2).reshape(DEPTH, D_MODEL, IN_DIM)
    wout = _cast_bf16(w_out.reshape(DEPTH * D_MODEL, D_MODEL), D_MODEL).reshape(DEPTH, D_MODEL, D_MODEL)
    cdft = jnp.asarray(_channel_dft()).astype(BF16)
    weights = (norm_g, wgu, wd, win, wout, cdft, conv_w, lam_qk, subln_g)

    dft_ctx = jnp.asarray(_position_dft(seq)).astype(BF16)
    y_prompt, (kc, vc) = _run_path(x_prompt, mods, lambda i: 0, weights, None, None, dft_ctx,
                                   tm=1024, tq=seq, seqs=4, tr=seq)
    new_cache_k = kc.reshape(batch, DEPTH, N_HEADS, 2, QK_DIM, seq).transpose(0, 1, 5, 2, 3, 4)
    new_cache_v = vc.reshape(batch, DEPTH, seq, N_HEADS, V_DIM)

    dft_lat = jnp.asarray(_position_dft(dec_seq)).astype(BF16)
    ctx = (cache_k.transpose(0, 1, 3, 4, 5, 2).reshape(dec_batch, DEPTH, ATT_DIM, past),
           cache_v.reshape(dec_batch, DEPTH, past * N_HEADS, V_DIM))
    y_sample, _ = _run_path(x_sample, mods, lambda i: 1 + i, weights, ctx, _rope_tables(dec_seq), dft_lat,
                            tm=1024, tq=256, seqs=1, tr=512)
    return (y_prompt, y_sample, new_cache_k, new_cache_v)
```

```python
import functools
import math

import jax
import jax.numpy as jnp
import numpy as np
from jax import lax
from jax.experimental import pallas as pl
from jax.experimental.pallas import tpu as pltpu

D_MODEL = 1024
DEPTH = 2
GRID_W = 64
V_DIM = 128
QK_DIM = V_DIM // 2
N_HEADS = 4
ATT_DIM = N_HEADS * V_DIM
CONV_DIM = 256
FOURIER_DIM = 256
FOURIER_GROUP_DIM = 64
IN_DIM = 3 * ATT_DIM + 3 * CONV_DIM + FOURIER_DIM
D_FF = 2816
N_MOD = 9
ROT_FREQS = 16
ROPE_THETA = 10000.0
EPS = 1e-6
FFN_RESIDUAL_WEIGHT = 0.5
LOG2_E = math.log2(math.e)

FF_CHUNK = 256
N_FF_CHUNKS = D_FF // FF_CHUNK
FF_GROUP = 3
assert (N_FF_CHUNKS - 2) % FF_GROUP == 0
MOD_ROWS = 8
SUBLANES = 8
LANES = 128
VMEM_LIMIT = 56 * 1024 * 1024

F32 = jnp.float32
BF16 = jnp.bfloat16


def _lambda_init(layer_idx):
    return 0.8 - 0.6 * math.exp(-0.3 * layer_idx)


def _rms(x, g):
    return x * lax.rsqrt(jnp.mean(x * x, axis=-1, keepdims=True) + EPS) * g


def _dot(a, b):
    return jnp.dot(a, b, preferred_element_type=F32)


def _params(semantics):
    return pltpu.CompilerParams(dimension_semantics=semantics, vmem_limit_bytes=VMEM_LIMIT)


def _resident(block_shape, index_map):
    return pl.BlockSpec(block_shape, index_map, pipeline_mode=pl.Buffered(1))


def _cast_kernel(x_ref, o_ref):
    o_ref[...] = x_ref[...].astype(o_ref.dtype)


def _cast_bf16(x, block_rows):
    rows, cols = x.shape
    return pl.pallas_call(
        _cast_kernel,
        out_shape=jax.ShapeDtypeStruct((rows, cols), BF16),
        grid=(rows // block_rows,),
        in_specs=[pl.BlockSpec((block_rows, cols), lambda i: (i, 0))],
        out_specs=pl.BlockSpec((block_rows, cols), lambda i: (i, 0)),
        compiler_params=_params(("arbitrary",)),
        name="cast_bf16",
    )(x)


def _mod_kernel(c_ref, w_ref, b_ref, o_ref):
    c = c_ref[...]
    s = (c / (1.0 + jnp.exp(-c))).astype(BF16)
    o_ref[...] = _dot(s, w_ref[...].astype(BF16)) + b_ref[...]


def _modulation(cvecs, w_mod, b_mod):
    col = D_MODEL
    out = pl.pallas_call(
        _mod_kernel,
        out_shape=jax.ShapeDtypeStruct((DEPTH, MOD_ROWS, N_MOD * D_MODEL), F32),
        grid=(DEPTH, N_MOD * D_MODEL // col),
        in_specs=[
            pl.BlockSpec((MOD_ROWS, D_MODEL), lambda l, j: (0, 0)),
            pl.BlockSpec((None, D_MODEL, col), lambda l, j: (l, 0, j)),
            pl.BlockSpec((None, 1, col), lambda l, j: (l, 0, j)),
        ],
        out_specs=pl.BlockSpec((None, MOD_ROWS, col), lambda l, j: (l, 0, j)),
        compiler_params=_params(("arbitrary", "arbitrary")),
        name="modulation",
    )(cvecs, w_mod, b_mod.reshape(DEPTH, 1, N_MOD * D_MODEL))
    return out.reshape(DEPTH, MOD_ROWS, N_MOD, D_MODEL)


def _ffn_kernel(x_ref, mod_ref, g_ref, wg_ref, wu_ref, wdn_ref, o_ref, acc_ref, hb_ref, wgu_sc, wd_sc,
                *, sub, g_row):
    step = pl.program_id(0)
    last = N_FF_CHUNKS - 1
    shift = mod_ref[3 * sub:3 * sub + 1, :]
    scale = mod_ref[3 * sub + 1:3 * sub + 2, :]
    gate = mod_ref[3 * sub + 2:3 * sub + 3, :]

    def pre_norm():
        h = _rms(x_ref[...], g_ref[g_row:g_row + 1, :] * (1.0 + scale)) + shift
        return h.astype(BF16)

    def hidden(hb, c):
        gu = _dot(hb, wgu_sc[c])
        g = gu[:, :FF_CHUNK]
        u = gu[:, FF_CHUNK:]
        return (g / (1.0 + jnp.exp(-g)) * u).astype(BF16)

    def finish(y):
        o_ref[...] = x_ref[...] + _rms(y, g_ref[g_row + 1:g_row + 2, :] * (FFN_RESIDUAL_WEIGHT * gate))

    @pl.when(step <= last)
    def _():
        @pl.when(step == 0)
        def _():
            hb_ref[...] = pre_norm()
            acc_ref[...] = jnp.zeros_like(acc_ref)

        wgu_sc[step, :, 0:FF_CHUNK] = wg_ref[...].astype(BF16)
        wgu_sc[step, :, FF_CHUNK:2 * FF_CHUNK] = wu_ref[...].astype(BF16)
        wd_sc[step] = wdn_ref[...].astype(BF16)
        acc_ref[...] += _dot(hidden(hb_ref[...], step), wd_sc[step])

        @pl.when(step == last)
        def _():
            finish(acc_ref[...])

    @pl.when(step > last)
    def _():
        hb = pre_norm()
        acc_ref[...] = _dot(hidden(hb, 0), wd_sc[0])

        def group(i, carry):
            c0 = 1 + i * FF_GROUP
            a = jnp.concatenate([hidden(hb, c0 + k) for k in range(FF_GROUP)], axis=-1)
            wd = wd_sc[pl.ds(c0, FF_GROUP)].reshape(FF_GROUP * FF_CHUNK, D_MODEL)
            acc_ref[...] += _dot(a, wd)
            return carry

        lax.fori_loop(0, (N_FF_CHUNKS - 2) // FF_GROUP, group, 0)
        finish(acc_ref[...] + _dot(hidden(hb, last), wd_sc[last]))


def _ffn(x, mods, norm_g, w_gu, w_down, *, layer, half, mod_row_of_tile, tm):
    n = x.shape[0]
    sub = 2 * half
    g_row = 4 * half
    last = N_FF_CHUNKS - 1

    def tile(s):
        return jnp.maximum(s - last, 0)

    def chunk(s):
        return jnp.minimum(s, last)

    return pl.pallas_call(
        functools.partial(_ffn_kernel, sub=sub, g_row=g_row),
        out_shape=jax.ShapeDtypeStruct((n, D_MODEL), F32),
        grid=(n // tm + last,),
        in_specs=[
            pl.BlockSpec((tm, D_MODEL), lambda s: (tile(s), 0)),
            pl.BlockSpec((None, None, N_MOD, D_MODEL), lambda s: (layer, mod_row_of_tile(tile(s)), 0, 0)),
            pl.BlockSpec((None, 6, D_MODEL), lambda s: (layer, 0, 0)),
            pl.BlockSpec((None, None, D_MODEL, FF_CHUNK), lambda s: (layer, half, 0, chunk(s))),
            pl.BlockSpec((None, None, D_MODEL, FF_CHUNK), lambda s: (layer, half, 0, N_FF_CHUNKS + chunk(s))),
            pl.BlockSpec((None, None, FF_CHUNK, D_MODEL), lambda s: (layer, half, chunk(s), 0)),
        ],
        out_specs=pl.BlockSpec((tm, D_MODEL), lambda s: (tile(s), 0)),
        scratch_shapes=[
            pltpu.VMEM((tm, D_MODEL), F32),
            pltpu.VMEM((tm, D_MODEL), BF16),
            pltpu.VMEM((N_FF_CHUNKS, D_MODEL, 2 * FF_CHUNK), BF16),
            pltpu.VMEM((N_FF_CHUNKS, FF_CHUNK, D_MODEL), BF16),
        ],
        compiler_params=_params(("arbitrary",)),
        name="ffn",
    )(x, mods, norm_g, w_gu, w_gu, w_down)


def _rope(x, cos, sin_signed, lo_half):
    partner = jnp.where(lo_half, pltpu.roll(x, LANES - ROT_FREQS, 1), pltpu.roll(x, ROT_FREQS, 1))
    return x * cos + partner * sin_signed


def _mixer_in_kernel(*refs, rotary, seq, cache_slot):
    x_ref, mod_ref, g_ref, win_ref, cdft_ref = refs[:5]
    if rotary:
        cos_ref, sin_ref = refs[5:7]
        q_ref, kt_ref, v_ref, gb_ref, u_ref, xcs_ref = refs[7:]
    else:
        q_ref, kt_ref, v_ref, gb_ref, u_ref, xcs_ref, kc_ref, vc_ref = refs[-8:]
    x = x_ref[...]
    shift = mod_ref[3:4, :]
    scale = mod_ref[4:5, :]
    h = _rms(x, g_ref[2:3, :]) * (1.0 + scale) + shift
    hb = h.astype(BF16)

    q = _dot(hb, win_ref[:, 0:ATT_DIM]) * (QK_DIM ** -0.5 * LOG2_E)
    k = _dot(hb, win_ref[:, ATT_DIM:2 * ATT_DIM])
    v = _dot(hb, win_ref[:, 2 * ATT_DIM:3 * ATT_DIM])
    v_ref[...] = v.astype(v_ref.dtype)
    if rotary:
        cos = cos_ref[...]
        sin = sin_ref[...]
        lane = lax.broadcasted_iota(jnp.int32, (1, LANES), 1)
        lo_half = (lane % (2 * ROT_FREQS)) < ROT_FREQS
        for hd in range(N_HEADS):
            sl = slice(hd * V_DIM, (hd + 1) * V_DIM)
            q_ref[:, sl] = _rope(q[:, sl], cos, sin, lo_half).astype(q_ref.dtype)
            kt_ref[sl, :] = _rope(k[:, sl], cos, sin, lo_half).T.astype(kt_ref.dtype)
    else:
        q_ref[...] = q.astype(q_ref.dtype)
        for s in range(x.shape[0] // seq):
            rows = slice(s * seq, (s + 1) * seq)
            kt = k[rows, :].T
            kt_ref[s] = kt.astype(kt_ref.dtype)
            if cache_slot is None:
                kc_out, vc_out = kc_ref.at[s], vc_ref.at[s]
            else:
                kc_out, vc_out = kc_ref.at[s, cache_slot], vc_ref.at[s, cache_slot]
                for other in range(DEPTH):
                    if other != cache_slot:
                        kc_ref[s, other] = jnp.zeros(kc_ref.shape[2:], F32)
                        vc_ref[s, other] = jnp.zeros(vc_ref.shape[2:], F32)
            kc_out[...] = kt
            for hd in range(N_HEADS):
                vc_out[pl.ds(hd, seq, stride=N_HEADS), :] = v[rows, hd * V_DIM:(hd + 1) * V_DIM]

    c0 = 3 * ATT_DIM
    gates = _dot(hb, win_ref[:, c0:c0 + 3 * CONV_DIM])
    gb_ref[...] = gates[:, 0:CONV_DIM]
    u_ref[...] = gates[:, CONV_DIM:2 * CONV_DIM] * gates[:, 2 * CONV_DIM:3 * CONV_DIM]
    f = _dot(hb, win_ref[:, c0 + 3 * CONV_DIM:IN_DIM])
    xcs_ref[...] = _dot(f.astype(BF16), cdft_ref[...]).astype(BF16)


def _mixer_in(x, mods, norm_g, win, cdft, rope_tabs, caches, *, layer, mod_row_of_tile, tm, seq):
    n = x.shape[0]
    rotary = rope_tabs is not None
    in_specs = [
        pl.BlockSpec((tm, D_MODEL), lambda i: (i, 0)),
        pl.BlockSpec((None, None, N_MOD, D_MODEL), lambda i: (layer, mod_row_of_tile(i), 0, 0)),
        pl.BlockSpec((None, 6, D_MODEL), lambda i: (layer, 0, 0)),
        _resident((None, D_MODEL, IN_DIM), lambda i: (layer, 0, 0)),
        _resident((FOURIER_DIM, 2 * FOURIER_DIM), lambda i: (0, 0)),
    ]
    args = [x, mods, norm_g, win, cdft]
    if rotary:
        tiles_per_seq = rope_tabs[0].shape[0] // tm
        for tab in rope_tabs:
            in_specs.append(pl.BlockSpec((tm, LANES), lambda i: (i % tiles_per_seq, 0)))
            args.append(tab)

    def rows(width):
        return pl.BlockSpec((tm, width), lambda i: (i, 0))

    batch = n // seq
    if tm <= seq:
        tiles_per_seq = seq // tm
        kt_spec = pl.BlockSpec((None, ATT_DIM, tm), lambda i: (i // tiles_per_seq, 0, i % tiles_per_seq))
    else:
        kt_spec = pl.BlockSpec((tm // seq, ATT_DIM, seq), lambda i: (i, 0, 0))
    out_shape = [
        jax.ShapeDtypeStruct((n, ATT_DIM), BF16),
        jax.ShapeDtypeStruct((batch, ATT_DIM, seq), BF16),
        jax.ShapeDtypeStruct((n, ATT_DIM), BF16),
        jax.ShapeDtypeStruct((n, CONV_DIM), F32),
        jax.ShapeDtypeStruct((n, CONV_DIM), F32),
        jax.ShapeDtypeStruct((n, 2 * FOURIER_DIM), BF16),
    ]
    out_specs = [rows(ATT_DIM), kt_spec, rows(ATT_DIM), rows(CONV_DIM), rows(CONV_DIM),
                 rows(2 * FOURIER_DIM)]
    aliases = {}
    if not rotary:
        seqs = tm // seq
        out_shape += [jax.ShapeDtypeStruct((batch, DEPTH, ATT_DIM, seq), F32),
                      jax.ShapeDtypeStruct((batch, DEPTH, N_HEADS * seq, V_DIM), F32)]
        if caches is None:
            out_specs += [pl.BlockSpec((seqs, DEPTH, ATT_DIM, seq), lambda i: (i, 0, 0, 0)),
                          pl.BlockSpec((seqs, DEPTH, N_HEADS * seq, V_DIM), lambda i: (i, 0, 0, 0))]
        else:
            out_specs += [pl.BlockSpec((seqs, None, ATT_DIM, seq), lambda i: (i, layer, 0, 0)),
                          pl.BlockSpec((seqs, None, N_HEADS * seq, V_DIM), lambda i: (i, layer, 0, 0))]
            first_cache_out = len(out_shape) - 2
            for offset, arr in enumerate(caches):
                aliases[len(args)] = first_cache_out + offset
                in_specs.append(pl.BlockSpec(memory_space=pl.ANY))
                args.append(arr)

    cache_slot = layer if (not rotary and caches is None) else None
    return pl.pallas_call(
        functools.partial(_mixer_in_kernel, rotary=rotary, seq=seq, cache_slot=cache_slot),
        out_shape=out_shape,
        grid=(n // tm,),
        in_specs=in_specs,
        out_specs=out_specs,
        input_output_aliases=aliases,
        compiler_params=_params(("arbitrary",)),
        name="mixer_in",
    )(*args)


def _lambda(lam_ref, lambda_init):
    lq = lam_ref[...]
    s01 = jnp.sum(lq[0:1, :] * lq[1:2, :], axis=-1, keepdims=True)
    s23 = jnp.sum(lq[2:3, :] * lq[3:4, :], axis=-1, keepdims=True)
    return jnp.exp(s01) - jnp.exp(s23) + lambda_init


def _attention_kernel(*refs, has_ctx, lambda_init, nb, n):
    if has_ctx:
        q_ref, kt_ref, v_ref, ckt_ref, cv_ref, lam_ref, sg_ref, o_ref, kt_sc, vext_sc = refs
    else:
        q_ref, kt_ref, v_ref, lam_ref, sg_ref, o_ref, vext_sc = refs
        kt_sc = kt_ref
    lam = _lambda(lam_ref, lambda_init)
    lane = lax.broadcasted_iota(jnp.int32, (1, V_DIM), 1)
    first_map = lane < QK_DIM
    sub_g = sg_ref[...] * (1.0 - lambda_init)

    @pl.when(pl.program_id(1) == 0)
    def _():
        for b in range(nb):
            if has_ctx:
                kt_sc[b, :, 0:n] = kt_ref[b]
                kt_sc[b, :, n:] = ckt_ref[...].astype(BF16)
            for hd in range(N_HEADS):
                sl = slice(hd * V_DIM, (hd + 1) * V_DIM)
                vext_sc[b, hd, 0:n, 0:V_DIM] = v_ref[b, :, sl]
                if has_ctx:
                    past = ckt_ref.shape[1]
                    vext_sc[b, hd, n:, 0:V_DIM] = cv_ref[pl.ds(hd, past, stride=N_HEADS), :].astype(BF16)
                vext_sc[b, hd, :, V_DIM:] = jnp.ones((vext_sc.shape[2], V_DIM), BF16)

    def exponentials(b, hd):
        sl = slice(hd * V_DIM, (hd + 1) * V_DIM)
        q = q_ref[b, :, sl]
        zero = jnp.zeros_like(q)
        out = []
        for keep in (first_map, jnp.logical_not(first_map)):
            qm = jnp.where(keep, q, zero)
            s = _dot(qm, kt_sc[b, sl, :])
            m = jnp.max(s, axis=-1, keepdims=True)
            out.append(jnp.exp2((s - m).astype(BF16)))
        return out

    def finish(b, hd, e_maps):
        sl = slice(hd * V_DIM, (hd + 1) * V_DIM)
        vext = vext_sc[b, hd]
        r0, r1 = [_dot(e, vext) for e in e_maps]
        o = r0[:, :V_DIM] / r0[:, V_DIM:] - lam * (r1[:, :V_DIM] / r1[:, V_DIM:])
        o_ref[b, :, sl] = (_rms(o, sub_g)).astype(o_ref.dtype)

    items = [(b, hd) for b in range(nb) for hd in range(N_HEADS)]
    pending = exponentials(*items[0])
    for idx, item in enumerate(items):
        upcoming = exponentials(*items[idx + 1]) if idx + 1 < len(items) else None
        finish(*item, pending)
        pending = upcoming


def _attention(q, kt, v, ctx_kt, ctx_v, lam_qk, subln_g, *, layer, tq, nb):
    b, n, _ = q.shape
    has_ctx = ctx_kt is not None
    assert not has_ctx or nb == 1
    in_specs = [
        pl.BlockSpec((nb, tq, ATT_DIM), lambda i, j: (i, j, 0)),
        pl.BlockSpec((nb, ATT_DIM, n), lambda i, j: (i, 0, 0)),
        pl.BlockSpec((nb, n, ATT_DIM), lambda i, j: (i, 0, 0)),
    ]
    args = [q, kt, v]
    if has_ctx:
        for arr in (ctx_kt, ctx_v):
            in_specs.append(pl.BlockSpec((None, None) + arr.shape[2:], lambda i, j: (i, layer, 0, 0)))
            args.append(arr)
    in_specs += [
        pl.BlockSpec((None, 4, QK_DIM), lambda i, j: (layer, 0, 0)),
        pl.BlockSpec((None, 1, V_DIM), lambda i, j: (layer, 0, 0)),
    ]
    args += [lam_qk, subln_g.reshape(DEPTH, 1, V_DIM)]
    n_keys = n + (ctx_kt.shape[3] if has_ctx else 0)
    scratch = [pltpu.VMEM((nb, N_HEADS, n_keys, 2 * V_DIM), BF16)]
    if has_ctx:
        scratch.insert(0, pltpu.VMEM((nb, ATT_DIM, n_keys), BF16))
    return pl.pallas_call(
        functools.partial(_attention_kernel, has_ctx=has_ctx, lambda_init=_lambda_init(layer), nb=nb, n=n),
        out_shape=jax.ShapeDtypeStruct((b, n, ATT_DIM), BF16),
        grid=(b // nb, n // tq),
        in_specs=in_specs,
        out_specs=pl.BlockSpec((nb, tq, ATT_DIM), lambda i, j: (i, j, 0)),
        scratch_shapes=scratch,
        compiler_params=_params(("arbitrary", "arbitrary")),
        name="attention",
    )(*args)


def _mixer_out_kernel(x_ref, att_ref, gb_ref, u_ref, xcs_ref, dft_ref, cw_ref, mod_ref, g_ref, wout_ref,
                      o_ref, *, seqs, tr, n):
    r0 = pl.multiple_of(pl.program_id(1) * tr, tr)
    gate = mod_ref[5:6, :]
    cw = cw_ref[...]
    row = lax.broadcasted_iota(jnp.int32, (tr, 1), 0)
    four_scale = (n * FOURIER_GROUP_DIM) ** -0.5
    for s in range(seqs):
        four = (_dot(dft_ref[:, 0:n], xcs_ref[s, :, 0:FOURIER_DIM])
                + _dot(dft_ref[:, n:2 * n], xcs_ref[s, :, FOURIER_DIM:2 * FOURIER_DIM])) * four_scale
        u = u_ref[s, pl.ds(r0, tr), :]
        before_start = pl.multiple_of(jnp.maximum(r0 - SUBLANES, 0), SUBLANES)
        after_start = pl.multiple_of(jnp.minimum(r0 + tr, n - SUBLANES), SUBLANES)
        before = u_ref[s, pl.ds(before_start, SUBLANES), :][SUBLANES - 1:SUBLANES, :]
        after = u_ref[s, pl.ds(after_start, SUBLANES), :][0:1, :]
        before = jnp.where(r0 > 0, before, 0.0)
        after = jnp.where(r0 + tr < n, after, 0.0)
        prev = jnp.where(row == 0, before, pltpu.roll(u, 1, 0))
        nxt = jnp.where(row == tr - 1, after, pltpu.roll(u, tr - 1, 0))
        conv = gb_ref[s] * (prev * cw[0:1, :] + u * cw[1:2, :] + nxt * cw[2:3, :])
        y = (_dot(att_ref[s], wout_ref[0:ATT_DIM, :])
             + _dot(conv.astype(BF16), wout_ref[ATT_DIM:ATT_DIM + CONV_DIM, :])
             + _dot(four.astype(BF16), wout_ref[ATT_DIM + CONV_DIM:D_MODEL, :]))
        o_ref[s] = x_ref[s] + gate * _rms(y, g_ref[3:4, :])


def _mixer_out(x, att, gb, u, xcs, dft, conv_w, mods, norm_g, wout, *, layer, mod_row_of_batch, seqs, tr):
    b, n, _ = x.shape

    def tile(width):
        return pl.BlockSpec((seqs, tr, width), lambda i, j: (i, j, 0))

    def whole(width):
        return pl.BlockSpec((seqs, n, width), lambda i, j: (i, 0, 0))

    return pl.pallas_call(
        functools.partial(_mixer_out_kernel, seqs=seqs, tr=tr, n=n),
        out_shape=jax.ShapeDtypeStruct((b, n, D_MODEL), F32),
        grid=(b // seqs, n // tr),
        in_specs=[
            tile(D_MODEL), tile(ATT_DIM), tile(CONV_DIM), whole(CONV_DIM), whole(2 * FOURIER_DIM),
            pl.BlockSpec((tr, 2 * n), lambda i, j: (j, 0)),
            pl.BlockSpec((None, 3, CONV_DIM), lambda i, j: (layer, 0, 0)),
            pl.BlockSpec((None, None, N_MOD, D_MODEL), lambda i, j: (layer, mod_row_of_batch(i), 0, 0)),
            pl.BlockSpec((None, 6, D_MODEL), lambda i, j: (layer, 0, 0)),
            _resident((None, D_MODEL, D_MODEL), lambda i, j: (layer, 0, 0)),
        ],
        out_specs=tile(D_MODEL),
        compiler_params=_params(("arbitrary", "arbitrary")),
        name="mixer_out",
    )(x, att, gb, u, xcs, dft, conv_w, mods, norm_g, wout)


def _position_dft(n):
    jk = (np.arange(n, dtype=np.int64)[:, None] * np.arange(n, dtype=np.int64)[None, :]) % n
    ang = 2.0 * np.pi * jk.astype(np.float64) / n
    return np.concatenate([np.cos(ang), -np.sin(ang)], axis=1).astype(np.float32)


def _channel_dft():
    g = FOURIER_GROUP_DIM
    jk = (np.arange(g)[:, None] * np.arange(g)[None, :]) % g
    ang = 2.0 * np.pi * jk.astype(np.float64) / g
    eye = np.eye(FOURIER_DIM // g)
    return np.concatenate([np.kron(eye, np.cos(ang)), np.kron(eye, np.sin(ang))], axis=1).astype(np.float32)


def _rope_tables(n):
    t = jnp.arange(n)
    row = (t // GRID_W).astype(F32)
    col = (t % GRID_W).astype(F32)
    inv = 1.0 / (ROPE_THETA ** (jnp.arange(ROT_FREQS, dtype=F32) / ROT_FREQS))
    ang_r = row[:, None] * inv
    ang_c = col[:, None] * inv
    cos = jnp.concatenate([jnp.cos(ang_r)] * 2 + [jnp.cos(ang_c)] * 2, axis=-1)
    sin = jnp.concatenate([-jnp.sin(ang_r), jnp.sin(ang_r), -jnp.sin(ang_c), jnp.sin(ang_c)], axis=-1)
    return jnp.tile(cos, (1, 2)), jnp.tile(sin, (1, 2))


def _run_path(x, mods, mod_row, weights, ctx, rope_tabs, dft, *, tm, tq, seqs, tr):
    norm_g, wgu, wd, win, wout, cdft, conv_w, lam_qk, subln_g = weights
    b, n, _ = x.shape
    tiles_per_seq = max(n // tm, 1)
    seqs_per_tile = max(tm // n, 1)

    def mod_row_of_tile(i):
        return mod_row(i * seqs_per_tile // tiles_per_seq)

    def shape3(a):
        return a.reshape(b, n, a.shape[-1])

    ctx_k, ctx_v = ctx if ctx is not None else (None, None)
    caches = None
    for l in range(DEPTH):
        flat = x.reshape(b * n, D_MODEL)
        flat = _ffn(flat, mods, norm_g, wgu, wd, layer=l, half=0, mod_row_of_tile=mod_row_of_tile, tm=tm)
        q, kt, v, gb, u, xcs, *caches = _mixer_in(flat, mods, norm_g, win, cdft, rope_tabs, caches, layer=l,
                                                  mod_row_of_tile=mod_row_of_tile, tm=tm, seq=n)
        caches = caches or None
        att = _attention(shape3(q), kt, shape3(v), ctx_k, ctx_v, lam_qk, subln_g, layer=l, tq=tq, nb=seqs)
        x = _mixer_out(shape3(flat), att, shape3(gb), shape3(u), shape3(xcs), dft, conv_w, mods, norm_g,
                       wout, layer=l, mod_row_of_batch=lambda i: mod_row(i * seqs), seqs=seqs, tr=tr)
        flat = _ffn(x.reshape(b * n, D_MODEL), mods, norm_g, wgu, wd, layer=l, half=1,
                    mod_row_of_tile=mod_row_of_tile, tm=tm)
        x = flat.reshape(b, n, D_MODEL)
    return x, caches


def kernel(x_prompt, x_sample, cache_k, cache_v, c, c_ctx, w_mod, b_mod, norm_g, w_ffn_gu, w_ffn_down,
           w_in, w_out, conv_w, lam_qk, subln_g):
    batch, seq, _ = x_prompt.shape
    dec_batch, dec_seq, _ = x_sample.shape
    past = cache_k.shape[2]

    cvecs = jnp.zeros((MOD_ROWS, D_MODEL), F32).at[0].set(c_ctx).at[1:1 + dec_batch].set(c)
    mods = _modulation(cvecs, w_mod, b_mod)

    wgu, wd = w_ffn_gu, w_ffn_down
    win = _cast_bf16(w_in.reshape(DEPTH * D_MODEL, IN_DIM), D_MODEL // 2).reshape(DEPTH, D_MODEL, IN_DIM)
    wout = _cast_bf16(w_out.reshape(DEPTH * D_MODEL, D_MODEL), D_MODEL).reshape(DEPTH, D_MODEL, D_MODEL)
    cdft = jnp.asarray(_channel_dft()).astype(BF16)
    weights = (norm_g, wgu, wd, win, wout, cdft, conv_w, lam_qk, subln_g)

    dft_ctx = jnp.asarray(_position_dft(seq)).astype(BF16)
    y_prompt, (kc, vc) = _run_path(x_prompt, mods, lambda i: 0, weights, None, None, dft_ctx,
                                   tm=1024, tq=seq, seqs=4, tr=seq)
    new_cache_k = kc.reshape(batch, DEPTH, N_HEADS, 2, QK_DIM, seq).transpose(0, 1, 5, 2, 3, 4)
    new_cache_v = vc.reshape(batch, DEPTH, seq, N_HEADS, V_DIM)

    dft_lat = jnp.asarray(_position_dft(dec_seq)).astype(BF16)
    ctx = (cache_k.transpose(0, 1, 3, 4, 5, 2).reshape(dec_batch, DEPTH, ATT_DIM, past),
           cache_v.reshape(dec_batch, DEPTH, past * N_HEADS, V_DIM))
    y_sample, _ = _run_path(x_sample, mods, lambda i: 1 + i, weights, ctx, _rope_tables(dec_seq), dft_lat,
                            tm=1024, tq=256, seqs=1, tr=512)
    return (y_prompt, y_sample, new_cache_k, new_cache_v)
```

```python
import functools
import math

import jax
import jax.numpy as jnp
import numpy as np
from jax import lax
from jax.experimental import pallas as pl
from jax.experimental.pallas import tpu as pltpu

D_MODEL = 1024
DEPTH = 2
GRID_W = 64
V_DIM = 128
QK_DIM = V_DIM // 2
N_HEADS = 4
ATT_DIM = N_HEADS * V_DIM
CONV_DIM = 256
FOURIER_DIM = 256
FOURIER_GROUP_DIM = 64
IN_DIM = 3 * ATT_DIM + 3 * CONV_DIM + FOURIER_DIM
D_FF = 2816
N_MOD = 9
ROT_FREQS = 16
ROPE_THETA = 10000.0
EPS = 1e-6
FFN_RESIDUAL_WEIGHT = 0.5
LOG2_E = math.log2(math.e)

FF_CHUNK = 256
N_FF_CHUNKS = D_FF // FF_CHUNK
FF_GROUP = 3
assert (N_FF_CHUNKS - 2) % FF_GROUP == 0
MOD_ROWS = 8
SUBLANES = 8
LANES = 128
VMEM_LIMIT = 56 * 1024 * 1024

F32 = jnp.float32
BF16 = jnp.bfloat16


def _lambda_init(layer_idx):
    return 0.8 - 0.6 * math.exp(-0.3 * layer_idx)


def _rms(x, g):
    return x * lax.rsqrt(jnp.mean(x * x, axis=-1, keepdims=True) + EPS) * g


def _dot(a, b):
    return jnp.dot(a, b, preferred_element_type=F32)


def _params(semantics):
    return pltpu.CompilerParams(dimension_semantics=semantics, vmem_limit_bytes=VMEM_LIMIT)


def _resident(block_shape, index_map):
    return pl.BlockSpec(block_shape, index_map, pipeline_mode=pl.Buffered(1))


def _cast_kernel(x_ref, o_ref):
    o_ref[...] = x_ref[...].astype(o_ref.dtype)


def _cast_bf16(x, block_rows):
    rows, cols = x.shape
    return pl.pallas_call(
        _cast_kernel,
        out_shape=jax.ShapeDtypeStruct((rows, cols), BF16),
        grid=(rows // block_rows,),
        in_specs=[pl.BlockSpec((block_rows, cols), lambda i: (i, 0))],
        out_specs=pl.BlockSpec((block_rows, cols), lambda i: (i, 0)),
        compiler_params=_params(("arbitrary",)),
        name="cast_bf16",
    )(x)


def _mod_kernel(c_ref, w_ref, b_ref, o_ref):
    c = c_ref[...]
    s = (c / (1.0 + jnp.exp(-c))).astype(BF16)
    o_ref[...] = _dot(s, w_ref[...].astype(BF16)) + b_ref[...]


def _modulation(cvecs, w_mod, b_mod):
    col = D_MODEL
    out = pl.pallas_call(
        _mod_kernel,
        out_shape=jax.ShapeDtypeStruct((DEPTH, MOD_ROWS, N_MOD * D_MODEL), F32),
        grid=(DEPTH, N_MOD * D_MODEL // col),
        in_specs=[
            pl.BlockSpec((MOD_ROWS, D_MODEL), lambda l, j: (0, 0)),
            pl.BlockSpec((None, D_MODEL, col), lambda l, j: (l, 0, j)),
            pl.BlockSpec((None, 1, col), lambda l, j: (l, 0, j)),
        ],
        out_specs=pl.BlockSpec((None, MOD_ROWS, col), lambda l, j: (l, 0, j)),
        compiler_params=_params(("arbitrary", "arbitrary")),
        name="modulation",
    )(cvecs, w_mod, b_mod.reshape(DEPTH, 1, N_MOD * D_MODEL))
    return out.reshape(DEPTH, MOD_ROWS, N_MOD, D_MODEL)


def _ffn_kernel(x_ref, mod_ref, g_ref, wg_ref, wu_ref, wdn_ref, o_ref, acc_ref, hb_ref, wgu_sc, wd_sc,
                *, sub, g_row):
    step = pl.program_id(0)
    last = N_FF_CHUNKS - 1
    shift = mod_ref[3 * sub:3 * sub + 1, :]
    scale = mod_ref[3 * sub + 1:3 * sub + 2, :]
    gate = mod_ref[3 * sub + 2:3 * sub + 3, :]

    def pre_norm():
        h = _rms(x_ref[...], g_ref[g_row:g_row + 1, :] * (1.0 + scale)) + shift
        return h.astype(BF16)

    def hidden(hb, c):
        gu = _dot(hb, wgu_sc[c])
        g = gu[:, :FF_CHUNK]
        u = gu[:, FF_CHUNK:]
        return (g / (1.0 + jnp.exp(-g)) * u).astype(BF16)

    def finish(y):
        o_ref[...] = x_ref[...] + _rms(y, g_ref[g_row + 1:g_row + 2, :] * (FFN_RESIDUAL_WEIGHT * gate))

    @pl.when(step <= last)
    def _():
        @pl.when(step == 0)
        def _():
            hb_ref[...] = pre_norm()
            acc_ref[...] = jnp.zeros_like(acc_ref)

        wgu_sc[step, :, 0:FF_CHUNK] = wg_ref[...].astype(BF16)
        wgu_sc[step, :, FF_CHUNK:2 * FF_CHUNK] = wu_ref[...].astype(BF16)
        wd_sc[step] = wdn_ref[...].astype(BF16)
        acc_ref[...] += _dot(hidden(hb_ref[...], step), wd_sc[step])

        @pl.when(step == last)
        def _():
            finish(acc_ref[...])

    @pl.when(step > last)
    def _():
        hb = pre_norm()
        acc_ref[...] = _dot(hidden(hb, 0), wd_sc[0])

        def group(i, carry):
            c0 = 1 + i * FF_GROUP
            a = jnp.concatenate([hidden(hb, c0 + k) for k in range(FF_GROUP)], axis=-1)
            wd = wd_sc[pl.ds(c0, FF_GROUP)].reshape(FF_GROUP * FF_CHUNK, D_MODEL)
            acc_ref[...] += _dot(a, wd)
            return carry

        lax.fori_loop(0, (N_FF_CHUNKS - 2) // FF_GROUP, group, 0)
        finish(acc_ref[...] + _dot(hidden(hb, last), wd_sc[last]))


def _ffn(x, mods, norm_g, w_gu, w_down, *, layer, half, mod_row_of_tile, tm):
    n = x.shape[0]
    sub = 2 * half
    g_row = 4 * half
    last = N_FF_CHUNKS - 1

    def tile(s):
        return jnp.maximum(s - last, 0)

    def chunk(s):
        return jnp.minimum(s, last)

    return pl.pallas_call(
        functools.partial(_ffn_kernel, sub=sub, g_row=g_row),
        out_shape=jax.ShapeDtypeStruct((n, D_MODEL), F32),
        grid=(n // tm + last,),
        in_specs=[
            pl.BlockSpec((tm, D_MODEL), lambda s: (tile(s), 0)),
            pl.BlockSpec((None, None, N_MOD, D_MODEL), lambda s: (layer, mod_row_of_tile(tile(s)), 0, 0)),
            pl.BlockSpec((None, 6, D_MODEL), lambda s: (layer, 0, 0)),
            pl.BlockSpec((None, None, D_MODEL, FF_CHUNK), lambda s: (layer, half, 0, chunk(s))),
            pl.BlockSpec((None, None, D_MODEL, FF_CHUNK), lambda s: (layer, half, 0, N_FF_CHUNKS + chunk(s))),
            pl.BlockSpec((None, None, FF_CHUNK, D_MODEL), lambda s: (layer, half, chunk(s), 0)),
        ],
        out_specs=pl.BlockSpec((tm, D_MODEL), lambda s: (tile(s), 0)),
        scratch_shapes=[
            pltpu.VMEM((tm, D_MODEL), F32),
            pltpu.VMEM((tm, D_MODEL), BF16),
            pltpu.VMEM((N_FF_CHUNKS, D_MODEL, 2 * FF_CHUNK), BF16),
            pltpu.VMEM((N_FF_CHUNKS, FF_CHUNK, D_MODEL), BF16),
        ],
        compiler_params=_params(("arbitrary",)),
        name="ffn",
    )(x, mods, norm_g, w_gu, w_gu, w_down)


def _rope(x, cos, sin_signed, lo_half):
    partner = jnp.where(lo_half, pltpu.roll(x, LANES - ROT_FREQS, 1), pltpu.roll(x, ROT_FREQS, 1))
    return x * cos + partner * sin_signed


def _mixer_in_kernel(*refs, rotary, seq, cache_slot):
    x_ref, mod_ref, g_ref, win_ref, cdft_ref = refs[:5]
    if rotary:
        cos_ref, sin_ref = refs[5:7]
        q_ref, kt_ref, v_ref, gb_ref, u_ref, xcs_ref = refs[7:]
    else:
        q_ref, kt_ref, v_ref, gb_ref, u_ref, xcs_ref, kc_ref, vc_ref = refs[-8:]
    x = x_ref[...]
    shift = mod_ref[3:4, :]
    scale = mod_ref[4:5, :]
    h = _rms(x, g_ref[2:3, :]) * (1.0 + scale) + shift
    hb = h.astype(BF16)

    q = _dot(hb, win_ref[:, 0:ATT_DIM]) * (QK_DIM ** -0.5 * LOG2_E)
    k = _dot(hb, win_ref[:, ATT_DIM:2 * ATT_DIM])
    v = _dot(hb, win_ref[:, 2 * ATT_DIM:3 * ATT_DIM])
    v_ref[...] = v.astype(v_ref.dtype)
    if rotary:
        cos = cos_ref[...]
        sin = sin_ref[...]
        lane = lax.broadcasted_iota(jnp.int32, (1, LANES), 1)
        lo_half = (lane % (2 * ROT_FREQS)) < ROT_FREQS
        for hd in range(N_HEADS):
            sl = slice(hd * V_DIM, (hd + 1) * V_DIM)
            q_ref[:, sl] = _rope(q[:, sl], cos, sin, lo_half).astype(q_ref.dtype)
            kt_ref[sl, :] = _rope(k[:, sl], cos, sin, lo_half).T.astype(kt_ref.dtype)
    else:
        q_ref[...] = q.astype(q_ref.dtype)
        for s in range(x.shape[0] // seq):
            rows = slice(s * seq, (s + 1) * seq)
            kt = k[rows, :].T
            kt_ref[s] = kt.astype(kt_ref.dtype)
            if cache_slot is None:
                kc_out, vc_out = kc_ref.at[s], vc_ref.at[s]
            else:
                kc_out, vc_out = kc_ref.at[s, cache_slot], vc_ref.at[s, cache_slot]
                for other in range(DEPTH):
                    if other != cache_slot:
                        kc_ref[s, other] = jnp.zeros(kc_ref.shape[2:], F32)
                        vc_ref[s, other] = jnp.zeros(vc_ref.shape[2:], F32)
            kc_out[...] = kt
            for hd in range(N_HEADS):
                vc_out[pl.ds(hd, seq, stride=N_HEADS), :] = v[rows, hd * V_DIM:(hd + 1) * V_DIM]

    c0 = 3 * ATT_DIM
    gates = _dot(hb, win_ref[:, c0:c0 + 3 * CONV_DIM])
    gb_ref[...] = gates[:, 0:CONV_DIM]
    u_ref[...] = gates[:, CONV_DIM:2 * CONV_DIM] * gates[:, 2 * CONV_DIM:3 * CONV_DIM]
    f = _dot(hb, win_ref[:, c0 + 3 * CONV_DIM:IN_DIM])
    xcs_ref[...] = _dot(f.astype(BF16), cdft_ref[...]).astype(BF16)


def _mixer_in(x, mods, norm_g, win, cdft, rope_tabs, caches, *, layer, mod_row_of_tile, tm, seq):
    n = x.shape[0]
    rotary = rope_tabs is not None
    in_specs = [
        pl.BlockSpec((tm, D_MODEL), lambda i: (i, 0)),
        pl.BlockSpec((None, None, N_MOD, D_MODEL), lambda i: (layer, mod_row_of_tile(i), 0, 0)),
        pl.BlockSpec((None, 6, D_MODEL), lambda i: (layer, 0, 0)),
        _resident((None, D_MODEL, IN_DIM), lambda i: (layer, 0, 0)),
        _resident((FOURIER_DIM, 2 * FOURIER_DIM), lambda i: (0, 0)),
    ]
    args = [x, mods, norm_g, win, cdft]
    if rotary:
        tiles_per_seq = rope_tabs[0].shape[0] // tm
        for tab in rope_tabs:
            in_specs.append(pl.BlockSpec((tm, LANES), lambda i: (i % tiles_per_seq, 0)))
            args.append(tab)

    def rows(width):
        return pl.BlockSpec((tm, width), lambda i: (i, 0))

    batch = n // seq
    if tm <= seq:
        tiles_per_seq = seq // tm
        kt_spec = pl.BlockSpec((None, ATT_DIM, tm), lambda i: (i // tiles_per_seq, 0, i % tiles_per_seq))
    else:
        kt_spec = pl.BlockSpec((tm // seq, ATT_DIM, seq), lambda i: (i, 0, 0))
    out_shape = [
        jax.ShapeDtypeStruct((n, ATT_DIM), BF16),
        jax.ShapeDtypeStruct((batch, ATT_DIM, seq), BF16),
        jax.ShapeDtypeStruct((n, ATT_DIM), BF16),
        jax.ShapeDtypeStruct((n, CONV_DIM), F32),
        jax.ShapeDtypeStruct((n, CONV_DIM), F32),
        jax.ShapeDtypeStruct((n, 2 * FOURIER_DIM), BF16),
    ]
    out_specs = [rows(ATT_DIM), kt_spec, rows(ATT_DIM), rows(CONV_DIM), rows(CONV_DIM),
                 rows(2 * FOURIER_DIM)]
    aliases = {}
    if not rotary:
        seqs = tm // seq
        out_shape += [jax.ShapeDtypeStruct((batch, DEPTH, ATT_DIM, seq), F32),
                      jax.ShapeDtypeStruct((batch, DEPTH, N_HEADS * seq, V_DIM), F32)]
        if caches is None:
            out_specs += [pl.BlockSpec((seqs, DEPTH, ATT_DIM, seq), lambda i: (i, 0, 0, 0)),
                          pl.BlockSpec((seqs, DEPTH, N_HEADS * seq, V_DIM), lambda i: (i, 0, 0, 0))]
        else:
            out_specs += [pl.BlockSpec((seqs, None, ATT_DIM, seq), lambda i: (i, layer, 0, 0)),
                          pl.BlockSpec((seqs, None, N_HEADS * seq, V_DIM), lambda i: (i, layer, 0, 0))]
            first_cache_out = len(out_shape) - 2
            for offset, arr in enumerate(caches):
                aliases[len(args)] = first_cache_out + offset
                in_specs.append(pl.BlockSpec(memory_space=pl.ANY))
                args.append(arr)

    cache_slot = layer if (not rotary and caches is None) else None
    return pl.pallas_call(
        functools.partial(_mixer_in_kernel, rotary=rotary, seq=seq, cache_slot=cache_slot),
        out_shape=out_shape,
        grid=(n // tm,),
        in_specs=in_specs,
        out_specs=out_specs,
        input_output_aliases=aliases,
        compiler_params=_params(("arbitrary",)),
        name="mixer_in",
    )(*args)


def _lambda(lam_ref, lambda_init):
    lq = lam_ref[...]
    s01 = jnp.sum(lq[0:1, :] * lq[1:2, :], axis=-1, keepdims=True)
    s23 = jnp.sum(lq[2:3, :] * lq[3:4, :], axis=-1, keepdims=True)
    return jnp.exp(s01) - jnp.exp(s23) + lambda_init


def _attention_kernel(*refs, has_ctx, lambda_init, nb, n):
    if has_ctx:
        q_ref, kt_ref, v_ref, ckt_ref, cv_ref, lam_ref, sg_ref, o_ref, kt_sc, vext_sc = refs
    else:
        q_ref, kt_ref, v_ref, lam_ref, sg_ref, o_ref, vext_sc = refs
        kt_sc = kt_ref
    lam = _lambda(lam_ref, lambda_init)
    lane = lax.broadcasted_iota(jnp.int32, (1, V_DIM), 1)
    first_map = lane < QK_DIM
    sub_g = sg_ref[...] * (1.0 - lambda_init)

    @pl.when(pl.program_id(1) == 0)
    def _():
        for b in range(nb):
            if has_ctx:
                kt_sc[b, :, 0:n] = kt_ref[b]
                kt_sc[b, :, n:] = ckt_ref[...].astype(BF16)
            for hd in range(N_HEADS):
                sl = slice(hd * V_DIM, (hd + 1) * V_DIM)
                vext_sc[b, hd, 0:n, 0:V_DIM] = v_ref[b, :, sl]
                if has_ctx:
                    past = ckt_ref.shape[1]
                    vext_sc[b, hd, n:, 0:V_DIM] = cv_ref[pl.ds(hd, past, stride=N_HEADS), :].astype(BF16)
                vext_sc[b, hd, :, V_DIM:] = jnp.ones((vext_sc.shape[2], V_DIM), BF16)

    def exponentials(b, hd):
        sl = slice(hd * V_DIM, (hd + 1) * V_DIM)
        q = q_ref[b, :, sl]
        zero = jnp.zeros_like(q)
        out = []
        for keep in (first_map, jnp.logical_not(first_map)):
            qm = jnp.where(keep, q, zero)
            s = _dot(qm, kt_sc[b, sl, :])
            m = jnp.max(s, axis=-1, keepdims=True)
            out.append(jnp.exp2((s - m).astype(BF16)))
        return out

    def finish(b, hd, e_maps):
        sl = slice(hd * V_DIM, (hd + 1) * V_DIM)
        vext = vext_sc[b, hd]
        r0, r1 = [_dot(e, vext) for e in e_maps]
        o = r0[:, :V_DIM] / r0[:, V_DIM:] - lam * (r1[:, :V_DIM] / r1[:, V_DIM:])
        o_ref[b, :, sl] = (_rms(o, sub_g)).astype(o_ref.dtype)

    items = [(b, hd) for b in range(nb) for hd in range(N_HEADS)]
    pending = exponentials(*items[0])
    for idx, item in enumerate(items):
        upcoming = exponentials(*items[idx + 1]) if idx + 1 < len(items) else None
        finish(*item, pending)
        pending = upcoming


def _attention(q, kt, v, ctx_kt, ctx_v, lam_qk, subln_g, *, layer, tq, nb):
    b, n, _ = q.shape
    has_ctx = ctx_kt is not None
    assert not has_ctx or nb == 1
    in_specs = [
        pl.BlockSpec((nb, tq, ATT_DIM), lambda i, j: (i, j, 0)),
        pl.BlockSpec((nb, ATT_DIM, n), lambda i, j: (i, 0, 0)),
        pl.BlockSpec((nb, n, ATT_DIM), lambda i, j: (i, 0, 0)),
    ]
    args = [q, kt, v]
    if has_ctx:
        for arr in (ctx_kt, ctx_v):
            in_specs.append(pl.BlockSpec((None, None) + arr.shape[2:], lambda i, j: (i, layer, 0, 0)))
            args.append(arr)
    in_specs += [
        pl.BlockSpec((None, 4, QK_DIM), lambda i, j: (layer, 0, 0)),
        pl.BlockSpec((None, 1, V_DIM), lambda i, j: (layer, 0, 0)),
    ]
    args += [lam_qk, subln_g.reshape(DEPTH, 1, V_DIM)]
    n_keys = n + (ctx_kt.shape[3] if has_ctx else 0)
    scratch = [pltpu.VMEM((nb, N_HEADS, n_keys, 2 * V_DIM), BF16)]
    if has_ctx:
        scratch.insert(0, pltpu.VMEM((nb, ATT_DIM, n_keys), BF16))
    return pl.pallas_call(
        functools.partial(_attention_kernel, has_ctx=has_ctx, lambda_init=_lambda_init(layer), nb=nb, n=n),
        out_shape=jax.ShapeDtypeStruct((b, n, ATT_DIM), BF16),
        grid=(b // nb, n // tq),
        in_specs=in_specs,
        out_specs=pl.BlockSpec((nb, tq, ATT_DIM), lambda i, j: (i, j, 0)),
        scratch_shapes=scratch,
        compiler_params=_params(("arbitrary", "arbitrary")),
        name="attention",
    )(*args)


def _mixer_out_kernel(x_ref, att_ref, gb_ref, u_ref, xcs_ref, dft_ref, cw_ref, mod_ref, g_ref, wout_ref,
                      o_ref, *, seqs, tr, n, parts):
    sub = tr // parts
    gate_g = g_ref[3:4, :] * mod_ref[5:6, :]
    cw = cw_ref[...]
    row = lax.broadcasted_iota(jnp.int32, (sub, 1), 0)
    four_scale = (n * FOURIER_GROUP_DIM) ** -0.5

    def mix(s, t):
        rows = slice(t * sub, (t + 1) * sub)
        r0 = pl.multiple_of(pl.program_id(1) * tr + t * sub, sub)
        four = (_dot(dft_ref[pl.ds(r0, sub), 0:n], xcs_ref[s, :, 0:FOURIER_DIM])
                + _dot(dft_ref[pl.ds(r0, sub), n:2 * n], xcs_ref[s, :, FOURIER_DIM:2 * FOURIER_DIM])
                ) * four_scale
        u = u_ref[s, pl.ds(r0, sub), :]
        before_start = pl.multiple_of(jnp.maximum(r0 - SUBLANES, 0), SUBLANES)
        after_start = pl.multiple_of(jnp.minimum(r0 + sub, n - SUBLANES), SUBLANES)
        before = u_ref[s, pl.ds(before_start, SUBLANES), :][SUBLANES - 1:SUBLANES, :]
        after = u_ref[s, pl.ds(after_start, SUBLANES), :][0:1, :]
        before = jnp.where(r0 > 0, before, 0.0)
        after = jnp.where(r0 + sub < n, after, 0.0)
        prev = jnp.where(row == 0, before, pltpu.roll(u, 1, 0))
        nxt = jnp.where(row == sub - 1, after, pltpu.roll(u, sub - 1, 0))
        conv = gb_ref[s, rows, :] * (prev * cw[0:1, :] + u * cw[1:2, :] + nxt * cw[2:3, :])
        return conv.astype(BF16), four.astype(BF16)

    def project(s, t, conv, four):
        rows = slice(t * sub, (t + 1) * sub)
        y = (_dot(att_ref[s, rows, :], wout_ref[0:ATT_DIM, :])
             + _dot(conv, wout_ref[ATT_DIM:ATT_DIM + CONV_DIM, :])
             + _dot(four, wout_ref[ATT_DIM + CONV_DIM:D_MODEL, :]))
        o_ref[s, rows, :] = x_ref[s, rows, :] + _rms(y, gate_g)

    items = [(s, t) for s in range(seqs) for t in range(parts)]
    pending = mix(*items[0])
    for idx, item in enumerate(items):
        upcoming = mix(*items[idx + 1]) if idx + 1 < len(items) else None
        project(*item, *pending)
        pending = upcoming


def _mixer_out(x, att, gb, u, xcs, dft, conv_w, mods, norm_g, wout, *, layer, mod_row_of_batch, seqs, tr,
               parts):
    b, n, _ = x.shape

    def tile(width):
        return pl.BlockSpec((seqs, tr, width), lambda i, j: (i, j, 0))

    def whole(width):
        return pl.BlockSpec((seqs, n, width), lambda i, j: (i, 0, 0))

    return pl.pallas_call(
        functools.partial(_mixer_out_kernel, seqs=seqs, tr=tr, n=n, parts=parts),
        out_shape=jax.ShapeDtypeStruct((b, n, D_MODEL), F32),
        grid=(b // seqs, n // tr),
        in_specs=[
            tile(D_MODEL), tile(ATT_DIM), tile(CONV_DIM), whole(CONV_DIM), whole(2 * FOURIER_DIM),
            _resident((n, 2 * n), lambda i, j: (0, 0)),
            pl.BlockSpec((None, 3, CONV_DIM), lambda i, j: (layer, 0, 0)),
            pl.BlockSpec((None, None, N_MOD, D_MODEL), lambda i, j: (layer, mod_row_of_batch(i), 0, 0)),
            pl.BlockSpec((None, 6, D_MODEL), lambda i, j: (layer, 0, 0)),
            _resident((None, D_MODEL, D_MODEL), lambda i, j: (layer, 0, 0)),
        ],
        out_specs=tile(D_MODEL),
        compiler_params=_params(("arbitrary", "arbitrary")),
        name="mixer_out",
    )(x, att, gb, u, xcs, dft, conv_w, mods, norm_g, wout)


def _position_dft(n):
    jk = (np.arange(n, dtype=np.int64)[:, None] * np.arange(n, dtype=np.int64)[None, :]) % n
    ang = 2.0 * np.pi * jk.astype(np.float64) / n
    return np.concatenate([np.cos(ang), -np.sin(ang)], axis=1).astype(np.float32)


def _channel_dft():
    g = FOURIER_GROUP_DIM
    jk = (np.arange(g)[:, None] * np.arange(g)[None, :]) % g
    ang = 2.0 * np.pi * jk.astype(np.float64) / g
    eye = np.eye(FOURIER_DIM // g)
    return np.concatenate([np.kron(eye, np.cos(ang)), np.kron(eye, np.sin(ang))], axis=1).astype(np.float32)


def _rope_tables(n):
    t = jnp.arange(n)
    row = (t // GRID_W).astype(F32)
    col = (t % GRID_W).astype(F32)
    inv = 1.0 / (ROPE_THETA ** (jnp.arange(ROT_FREQS, dtype=F32) / ROT_FREQS))
    ang_r = row[:, None] * inv
    ang_c = col[:, None] * inv
    cos = jnp.concatenate([jnp.cos(ang_r)] * 2 + [jnp.cos(ang_c)] * 2, axis=-1)
    sin = jnp.concatenate([-jnp.sin(ang_r), jnp.sin(ang_r), -jnp.sin(ang_c), jnp.sin(ang_c)], axis=-1)
    return jnp.tile(cos, (1, 2)), jnp.tile(sin, (1, 2))


def _run_path(x, mods, mod_row, weights, ctx, rope_tabs, dft, *, tm, tq, seqs, tr, parts):
    norm_g, wgu, wd, win, wout, cdft, conv_w, lam_qk, subln_g = weights
    b, n, _ = x.shape
    tiles_per_seq = max(n // tm, 1)
    seqs_per_tile = max(tm // n, 1)

    def mod_row_of_tile(i):
        return mod_row(i * seqs_per_tile // tiles_per_seq)

    def shape3(a):
        return a.reshape(b, n, a.shape[-1])

    ctx_k, ctx_v = ctx if ctx is not None else (None, None)
    caches = None
    for l in range(DEPTH):
        flat = x.reshape(b * n, D_MODEL)
        flat = _ffn(flat, mods, norm_g, wgu, wd, layer=l, half=0, mod_row_of_tile=mod_row_of_tile, tm=tm)
        q, kt, v, gb, u, xcs, *caches = _mixer_in(flat, mods, norm_g, win, cdft, rope_tabs, caches, layer=l,
                                                  mod_row_of_tile=mod_row_of_tile, tm=tm, seq=n)
        caches = caches or None
        att = _attention(shape3(q), kt, shape3(v), ctx_k, ctx_v, lam_qk, subln_g, layer=l, tq=tq, nb=seqs)
        x = _mixer_out(shape3(flat), att, shape3(gb), shape3(u), shape3(xcs), dft, conv_w, mods, norm_g,
                       wout, layer=l, mod_row_of_batch=lambda i: mod_row(i * seqs), seqs=seqs, tr=tr,
                       parts=parts)
        flat = _ffn(x.reshape(b * n, D_MODEL), mods, norm_g, wgu, wd, layer=l, half=1,
                    mod_row_of_tile=mod_row_of_tile, tm=tm)
        x = flat.reshape(b, n, D_MODEL)
    return x, caches


def kernel(x_prompt, x_sample, cache_k, cache_v, c, c_ctx, w_mod, b_mod, norm_g, w_ffn_gu, w_ffn_down,
           w_in, w_out, conv_w, lam_qk, subln_g):
    batch, seq, _ = x_prompt.shape
    dec_batch, dec_seq, _ = x_sample.shape
    past = cache_k.shape[2]

    cvecs = jnp.zeros((MOD_ROWS, D_MODEL), F32).at[0].set(c_ctx).at[1:1 + dec_batch].set(c)
    mods = _modulation(cvecs, w_mod, b_mod)

    wgu, wd = w_ffn_gu, w_ffn_down
    win = _cast_bf16(w_in.reshape(DEPTH * D_MODEL, IN_DIM), D_MODEL // 2).reshape(DEPTH, D_MODEL, IN_DIM)
    wout = _cast_bf16(w_out.reshape(DEPTH * D_MODEL, D_MODEL), D_MODEL).reshape(DEPTH, D_MODEL, D_MODEL)
    cdft = jnp.asarray(_channel_dft()).astype(BF16)
    weights = (norm_g, wgu, wd, win, wout, cdft, conv_w, lam_qk, subln_g)

    dft_ctx = jnp.asarray(_position_dft(seq)).astype(BF16)
    y_prompt, (kc, vc) = _run_path(x_prompt, mods, lambda i: 0, weights, None, None, dft_ctx,
                                   tm=1024, tq=seq, seqs=4, tr=seq, parts=1)
    new_cache_k = kc.reshape(batch, DEPTH, N_HEADS, 2, QK_DIM, seq).transpose(0, 1, 5, 2, 3, 4)
    new_cache_v = vc.reshape(batch, DEPTH, seq, N_HEADS, V_DIM)

    dft_lat = jnp.asarray(_position_dft(dec_seq)).astype(BF16)
    ctx = (cache_k.transpose(0, 1, 3, 4, 5, 2).reshape(dec_batch, DEPTH, ATT_DIM, past),
           cache_v.reshape(dec_batch, DEPTH, past * N_HEADS, V_DIM))
    y_sample, _ = _run_path(x_sample, mods, lambda i: 1 + i, weights, ctx, _rope_tables(dec_seq), dft_lat,
                            tm=1024, tq=256, seqs=1, tr=512, parts=2)
    return (y_prompt, y_sample, new_cache_k, new_cache_v)
```

```python
import functools
import math

import jax
import jax.numpy as jnp
import numpy as np
from jax import lax
from jax.experimental import pallas as pl
from jax.experimental.pallas import tpu as pltpu

D_MODEL = 1024
DEPTH = 2
GRID_W = 64
V_DIM = 128
QK_DIM = V_DIM // 2
N_HEADS = 4
ATT_DIM = N_HEADS * V_DIM
CONV_DIM = 256
FOURIER_DIM = 256
FOURIER_GROUP_DIM = 64
IN_DIM = 3 * ATT_DIM + 3 * CONV_DIM + FOURIER_DIM
D_FF = 2816
N_MOD = 9
ROT_FREQS = 16
ROPE_THETA = 10000.0
EPS = 1e-6
FFN_RESIDUAL_WEIGHT = 0.5
LOG2_E = math.log2(math.e)

FF_CHUNK = 256
N_FF_CHUNKS = D_FF // FF_CHUNK
FF_GROUP = 3
FF_LOOP_TRIPS = 2
FF_TAIL_CHUNKS = N_FF_CHUNKS - 1 - FF_LOOP_TRIPS * FF_GROUP
assert FF_TAIL_CHUNKS > 0 and D_MODEL % (FF_TAIL_CHUNKS * FF_CHUNK) == 0
FF_TAIL_ROW_BLOCKS = 4
MOD_ROWS = 8
SUBLANES = 8
LANES = 128
VMEM_LIMIT = 61 * 1024 * 1024

F32 = jnp.float32
BF16 = jnp.bfloat16


def _lambda_init(layer_idx):
    return 0.8 - 0.6 * math.exp(-0.3 * layer_idx)


def _rms(x, g):
    return x * lax.rsqrt(jnp.mean(x * x, axis=-1, keepdims=True) + EPS) * g


def _dot(a, b):
    return jnp.dot(a, b, preferred_element_type=F32)


def _params(semantics):
    return pltpu.CompilerParams(dimension_semantics=semantics, vmem_limit_bytes=VMEM_LIMIT)


def _resident(block_shape, index_map):
    return pl.BlockSpec(block_shape, index_map, pipeline_mode=pl.Buffered(1))


def _cast_kernel(x_ref, o_ref):
    o_ref[...] = x_ref[...].astype(o_ref.dtype)


def _cast_bf16(x, block_rows):
    rows, cols = x.shape
    return pl.pallas_call(
        _cast_kernel,
        out_shape=jax.ShapeDtypeStruct((rows, cols), BF16),
        grid=(rows // block_rows,),
        in_specs=[pl.BlockSpec((block_rows, cols), lambda i: (i, 0))],
        out_specs=pl.BlockSpec((block_rows, cols), lambda i: (i, 0)),
        compiler_params=_params(("arbitrary",)),
        name="cast_bf16",
    )(x)


def _mod_kernel(c_ref, w_ref, b_ref, o_ref):
    c = c_ref[...]
    s = (c / (1.0 + jnp.exp(-c))).astype(BF16)
    o_ref[...] = _dot(s, w_ref[...].astype(BF16)) + b_ref[...]


def _modulation(cvecs, w_mod, b_mod):
    col = D_MODEL
    out = pl.pallas_call(
        _mod_kernel,
        out_shape=jax.ShapeDtypeStruct((DEPTH, MOD_ROWS, N_MOD * D_MODEL), F32),
        grid=(DEPTH, N_MOD * D_MODEL // col),
        in_specs=[
            pl.BlockSpec((MOD_ROWS, D_MODEL), lambda l, j: (0, 0)),
            pl.BlockSpec((None, D_MODEL, col), lambda l, j: (l, 0, j)),
            pl.BlockSpec((None, 1, col), lambda l, j: (l, 0, j)),
        ],
        out_specs=pl.BlockSpec((None, MOD_ROWS, col), lambda l, j: (l, 0, j)),
        compiler_params=_params(("arbitrary", "arbitrary")),
        name="modulation",
    )(cvecs, w_mod, b_mod.reshape(DEPTH, 1, N_MOD * D_MODEL))
    return out.reshape(DEPTH, MOD_ROWS, N_MOD, D_MODEL)


def _ffn_kernel(x_ref, xn_ref, mod_ref, modn_ref, g_ref, wg_ref, wu_ref, wdn_ref, o_ref, hb_ref, wgu_sc, wd_sc,
                *, sub, g_row):
    step = pl.program_id(0)
    last = N_FF_CHUNKS - 1

    def pre_norm(xr, mr):
        shift = mr[3 * sub:3 * sub + 1, :]
        scale = mr[3 * sub + 1:3 * sub + 2, :]
        h = _rms(xr[...], g_ref[g_row:g_row + 1, :] * (1.0 + scale)) + shift
        return h.astype(BF16)

    def hidden(slot, c):
        gu = _dot(hb_ref[slot], wgu_sc[c])
        g = gu[:, :FF_CHUNK]
        u = gu[:, FF_CHUNK:]
        return (g / (1.0 + jnp.exp(-g)) * u).astype(BF16)

    def hidden_cat(slot, c0, count):
        return jnp.concatenate([hidden(slot, c0 + k) for k in range(count)], axis=-1)

    def down_weights(c0, count):
        return wd_sc[pl.ds(c0, count)].reshape(count * FF_CHUNK, D_MODEL)

    def hidden_group(slot, c0, count):
        return _dot(hidden_cat(slot, c0, count), down_weights(c0, count))

    def finish(y, rows=slice(None)):
        gate = mod_ref[3 * sub + 2:3 * sub + 3, :]
        o_ref[rows, :] = x_ref[rows, :] + _rms(y, g_ref[g_row + 1:g_row + 2, :] * (FFN_RESIDUAL_WEIGHT * gate))

    @pl.when(step <= last)
    def _():
        @pl.when(step == 0)
        def _():
            hb_ref[0] = pre_norm(x_ref, mod_ref)
            o_ref[...] = jnp.zeros_like(o_ref)

        wgu_sc[step, :, 0:FF_CHUNK] = wg_ref[...].astype(BF16)
        wgu_sc[step, :, FF_CHUNK:2 * FF_CHUNK] = wu_ref[...].astype(BF16)
        wd_sc[step] = wdn_ref[...].astype(BF16)
        o_ref[...] += _dot(hidden(0, step), wd_sc[step])

        @pl.when(step == last)
        def _():
            hb_ref[1] = pre_norm(xn_ref, modn_ref)
            finish(o_ref[...])

    def whole_tile(slot):
        o_ref[...] = hidden_group(slot, 0, 1)

        def group(i, carry):
            o_ref[...] += hidden_group(slot, 1 + i * FF_GROUP, FF_GROUP)
            return carry

        lax.fori_loop(0, FF_LOOP_TRIPS, group, 0)
        tail = N_FF_CHUNKS - FF_TAIL_CHUNKS
        width = FF_TAIL_CHUNKS * FF_CHUNK
        hb_next = pre_norm(xn_ref, modn_ref)
        hb_ref[1 - slot] = hb_next
        zero = functools.reduce(jnp.add, [
            jnp.minimum(jnp.abs(hb_next[:, k * width:(k + 1) * width]), 0) for k in range(D_MODEL // width)])
        a = hidden_cat(slot, tail, FF_TAIL_CHUNKS) + zero
        wd = down_weights(tail, FF_TAIL_CHUNKS)
        block = o_ref.shape[0] // FF_TAIL_ROW_BLOCKS
        for r in range(FF_TAIL_ROW_BLOCKS):
            rows = slice(r * block, (r + 1) * block)
            finish(o_ref[rows, :] + _dot(a[rows, :], wd), rows)

    for slot in range(2):
        pl.when((step > last) & ((step - last) % 2 == slot))(functools.partial(whole_tile, slot))


def _ffn(x, mods, norm_g, w_gu, w_down, *, layer, half, mod_row_of_tile, tm):
    n = x.shape[0]
    sub = 2 * half
    g_row = 4 * half
    last = N_FF_CHUNKS - 1
    last_tile = n // tm - 1

    def tile(s):
        return jnp.maximum(s - last, 0)

    def next_tile(s):
        return jnp.minimum(tile(s) + 1, last_tile)

    def chunk(s):
        return jnp.minimum(s, last)

    def x_spec(which):
        return pl.BlockSpec((tm, D_MODEL), lambda s: (which(s), 0))

    def mod_spec(which):
        return pl.BlockSpec((None, None, N_MOD, D_MODEL), lambda s: (layer, mod_row_of_tile(which(s)), 0, 0))

    return pl.pallas_call(
        functools.partial(_ffn_kernel, sub=sub, g_row=g_row),
        out_shape=jax.ShapeDtypeStruct((n, D_MODEL), F32),
        grid=(n // tm + last,),
        in_specs=[
            x_spec(tile), x_spec(next_tile), mod_spec(tile), mod_spec(next_tile),
            pl.BlockSpec((None, 6, D_MODEL), lambda s: (layer, 0, 0)),
            pl.BlockSpec((None, None, D_MODEL, FF_CHUNK), lambda s: (layer, half, 0, chunk(s))),
            pl.BlockSpec((None, None, D_MODEL, FF_CHUNK), lambda s: (layer, half, 0, N_FF_CHUNKS + chunk(s))),
            pl.BlockSpec((None, None, FF_CHUNK, D_MODEL), lambda s: (layer, half, chunk(s), 0)),
        ],
        out_specs=x_spec(tile),
        scratch_shapes=[
            pltpu.VMEM((2, tm, D_MODEL), BF16),
            pltpu.VMEM((N_FF_CHUNKS, D_MODEL, 2 * FF_CHUNK), BF16),
            pltpu.VMEM((N_FF_CHUNKS, FF_CHUNK, D_MODEL), BF16),
        ],
        compiler_params=_params(("arbitrary",)),
        name="ffn",
    )(x, x, mods, mods, norm_g, w_gu, w_gu, w_down)


def _rope(x, cos, sin_signed, lo_half):
    partner = jnp.where(lo_half, pltpu.roll(x, LANES - ROT_FREQS, 1), pltpu.roll(x, ROT_FREQS, 1))
    return x * cos + partner * sin_signed


def _mixer_in_kernel(*refs, rotary, seq, cache_slot):
    x_ref, mod_ref, g_ref, win_ref, cdft_ref = refs[:5]
    if rotary:
        cos_ref, sin_ref = refs[5:7]
        q_ref, kt_ref, v_ref, gb_ref, u_ref, xcs_ref = refs[7:]
    else:
        q_ref, kt_ref, v_ref, gb_ref, u_ref, xcs_ref, kc_ref, vc_ref = refs[-8:]
    x = x_ref[...]
    shift = mod_ref[3:4, :]
    scale = mod_ref[4:5, :]
    h = _rms(x, g_ref[2:3, :]) * (1.0 + scale) + shift
    hb = h.astype(BF16)

    q = _dot(hb, win_ref[:, 0:ATT_DIM]) * (QK_DIM ** -0.5 * LOG2_E)
    k = _dot(hb, win_ref[:, ATT_DIM:2 * ATT_DIM])
    v = _dot(hb, win_ref[:, 2 * ATT_DIM:3 * ATT_DIM])
    v_ref[...] = v.astype(v_ref.dtype)
    if rotary:
        cos = cos_ref[...]
        sin = sin_ref[...]
        lane = lax.broadcasted_iota(jnp.int32, (1, LANES), 1)
        lo_half = (lane % (2 * ROT_FREQS)) < ROT_FREQS
        for hd in range(N_HEADS):
            sl = slice(hd * V_DIM, (hd + 1) * V_DIM)
            q_ref[:, sl] = _rope(q[:, sl], cos, sin, lo_half).astype(q_ref.dtype)
            kt_ref[sl, :] = _rope(k[:, sl], cos, sin, lo_half).T.astype(kt_ref.dtype)
    else:
        q_ref[...] = q.astype(q_ref.dtype)
        for s in range(x.shape[0] // seq):
            rows = slice(s * seq, (s + 1) * seq)
            kt = k[rows, :].T
            kt_ref[s] = kt.astype(kt_ref.dtype)
            if cache_slot is None:
                kc_out, vc_out = kc_ref.at[s], vc_ref.at[s]
            else:
                kc_out, vc_out = kc_ref.at[s, cache_slot], vc_ref.at[s, cache_slot]
                for other in range(DEPTH):
                    if other != cache_slot:
                        kc_ref[s, other] = jnp.zeros(kc_ref.shape[2:], F32)
                        vc_ref[s, other] = jnp.zeros(vc_ref.shape[2:], F32)
            kc_out[...] = kt
            for hd in range(N_HEADS):
                vc_out[pl.ds(hd, seq, stride=N_HEADS), :] = v[rows, hd * V_DIM:(hd + 1) * V_DIM]

    c0 = 3 * ATT_DIM
    gates = _dot(hb, win_ref[:, c0:c0 + 3 * CONV_DIM])
    gb_ref[...] = gates[:, 0:CONV_DIM]
    u_ref[...] = gates[:, CONV_DIM:2 * CONV_DIM] * gates[:, 2 * CONV_DIM:3 * CONV_DIM]
    f = _dot(hb, win_ref[:, c0 + 3 * CONV_DIM:IN_DIM])
    xcs_ref[...] = _dot(f.astype(BF16), cdft_ref[...]).astype(BF16)


def _mixer_in(x, mods, norm_g, win, cdft, rope_tabs, caches, *, layer, mod_row_of_tile, tm, seq):
    n = x.shape[0]
    rotary = rope_tabs is not None
    in_specs = [
        pl.BlockSpec((tm, D_MODEL), lambda i: (i, 0)),
        pl.BlockSpec((None, None, N_MOD, D_MODEL), lambda i: (layer, mod_row_of_tile(i), 0, 0)),
        pl.BlockSpec((None, 6, D_MODEL), lambda i: (layer, 0, 0)),
        _resident((None, D_MODEL, IN_DIM), lambda i: (layer, 0, 0)),
        _resident((FOURIER_DIM, 2 * FOURIER_DIM), lambda i: (0, 0)),
    ]
    args = [x, mods, norm_g, win, cdft]
    if rotary:
        tiles_per_seq = rope_tabs[0].shape[0] // tm
        for tab in rope_tabs:
            in_specs.append(pl.BlockSpec((tm, LANES), lambda i: (i % tiles_per_seq, 0)))
            args.append(tab)

    def rows(width):
        return pl.BlockSpec((tm, width), lambda i: (i, 0))

    batch = n // seq
    if tm <= seq:
        tiles_per_seq = seq // tm
        kt_spec = pl.BlockSpec((None, ATT_DIM, tm), lambda i: (i // tiles_per_seq, 0, i % tiles_per_seq))
    else:
        kt_spec = pl.BlockSpec((tm // seq, ATT_DIM, seq), lambda i: (i, 0, 0))
    out_shape = [
        jax.ShapeDtypeStruct((n, ATT_DIM), BF16),
        jax.ShapeDtypeStruct((batch, ATT_DIM, seq), BF16),
        jax.ShapeDtypeStruct((n, ATT_DIM), BF16),
        jax.ShapeDtypeStruct((n, CONV_DIM), F32),
        jax.ShapeDtypeStruct((n, CONV_DIM), F32),
        jax.ShapeDtypeStruct((n, 2 * FOURIER_DIM), BF16),
    ]
    out_specs = [rows(ATT_DIM), kt_spec, rows(ATT_DIM), rows(CONV_DIM), rows(CONV_DIM),
                 rows(2 * FOURIER_DIM)]
    aliases = {}
    if not rotary:
        seqs = tm // seq
        out_shape += [jax.ShapeDtypeStruct((batch, DEPTH, ATT_DIM, seq), F32),
                      jax.ShapeDtypeStruct((batch, DEPTH, N_HEADS * seq, V_DIM), F32)]
        if caches is None:
            out_specs += [pl.BlockSpec((seqs, DEPTH, ATT_DIM, seq), lambda i: (i, 0, 0, 0)),
                          pl.BlockSpec((seqs, DEPTH, N_HEADS * seq, V_DIM), lambda i: (i, 0, 0, 0))]
        else:
            out_specs += [pl.BlockSpec((seqs, None, ATT_DIM, seq), lambda i: (i, layer, 0, 0)),
                          pl.BlockSpec((seqs, None, N_HEADS * seq, V_DIM), lambda i: (i, layer, 0, 0))]
            first_cache_out = len(out_shape) - 2
            for offset, arr in enumerate(caches):
                aliases[len(args)] = first_cache_out + offset
                in_specs.append(pl.BlockSpec(memory_space=pl.ANY))
                args.append(arr)

    cache_slot = layer if (not rotary and caches is None) else None
    return pl.pallas_call(
        functools.partial(_mixer_in_kernel, rotary=rotary, seq=seq, cache_slot=cache_slot),
        out_shape=out_shape,
        grid=(n // tm,),
        in_specs=in_specs,
        out_specs=out_specs,
        input_output_aliases=aliases,
        compiler_params=_params(("arbitrary",)),
        name="mixer_in",
    )(*args)


def _lambda(lam_ref, lambda_init):
    lq = lam_ref[...]
    s01 = jnp.sum(lq[0:1, :] * lq[1:2, :], axis=-1, keepdims=True)
    s23 = jnp.sum(lq[2:3, :] * lq[3:4, :], axis=-1, keepdims=True)
    return jnp.exp(s01) - jnp.exp(s23) + lambda_init


def _attention_kernel(*refs, has_ctx, lambda_init, nb, n):
    if has_ctx:
        q_ref, kt_ref, v_ref, ckt_ref, cv_ref, lam_ref, sg_ref, o_ref, kt_sc, vext_sc = refs
    else:
        q_ref, kt_ref, v_ref, lam_ref, sg_ref, o_ref, vext_sc = refs
        kt_sc = kt_ref
    lam = _lambda(lam_ref, lambda_init)
    lane = lax.broadcasted_iota(jnp.int32, (1, V_DIM), 1)
    first_map = lane < QK_DIM
    sub_g = sg_ref[...] * (1.0 - lambda_init)

    @pl.when(pl.program_id(1) == 0)
    def _():
        for b in range(nb):
            if has_ctx:
                kt_sc[b, :, 0:n] = kt_ref[b]
                kt_sc[b, :, n:] = ckt_ref[...].astype(BF16)
            for hd in range(N_HEADS):
                sl = slice(hd * V_DIM, (hd + 1) * V_DIM)
                vext_sc[b, hd, 0:n, 0:V_DIM] = v_ref[b, :, sl]
                if has_ctx:
                    past = ckt_ref.shape[1]
                    vext_sc[b, hd, n:, 0:V_DIM] = cv_ref[pl.ds(hd, past, stride=N_HEADS), :].astype(BF16)
                vext_sc[b, hd, :, V_DIM:] = jnp.ones((vext_sc.shape[2], V_DIM), BF16)

    def exponentials(b, hd):
        sl = slice(hd * V_DIM, (hd + 1) * V_DIM)
        q = q_ref[b, :, sl]
        zero = jnp.zeros_like(q)
        out = []
        for keep in (first_map, jnp.logical_not(first_map)):
            qm = jnp.where(keep, q, zero)
            s = _dot(qm, kt_sc[b, sl, :])
            m = jnp.max(s, axis=-1, keepdims=True)
            out.append(jnp.exp2((s - m).astype(BF16)))
        return out

    def finish(b, hd, e_maps):
        sl = slice(hd * V_DIM, (hd + 1) * V_DIM)
        vext = vext_sc[b, hd]
        r0, r1 = [_dot(e, vext) for e in e_maps]
        o = r0[:, :V_DIM] / r0[:, V_DIM:] - lam * (r1[:, :V_DIM] / r1[:, V_DIM:])
        o_ref[b, :, sl] = (_rms(o, sub_g)).astype(o_ref.dtype)

    items = [(b, hd) for b in range(nb) for hd in range(N_HEADS)]
    pending = exponentials(*items[0])
    for idx, item in enumerate(items):
        upcoming = exponentials(*items[idx + 1]) if idx + 1 < len(items) else None
        finish(*item, pending)
        pending = upcoming


def _attention(q, kt, v, ctx_kt, ctx_v, lam_qk, subln_g, *, layer, tq, nb):
    b, n, _ = q.shape
    has_ctx = ctx_kt is not None
    assert not has_ctx or nb == 1
    in_specs = [
        pl.BlockSpec((nb, tq, ATT_DIM), lambda i, j: (i, j, 0)),
        pl.BlockSpec((nb, ATT_DIM, n), lambda i, j: (i, 0, 0)),
        pl.BlockSpec((nb, n, ATT_DIM), lambda i, j: (i, 0, 0)),
    ]
    args = [q, kt, v]
    if has_ctx:
        for arr in (ctx_kt, ctx_v):
            in_specs.append(pl.BlockSpec((None, None) + arr.shape[2:], lambda i, j: (i, layer, 0, 0)))
            args.append(arr)
    in_specs += [
        pl.BlockSpec((None, 4, QK_DIM), lambda i, j: (layer, 0, 0)),
        pl.BlockSpec((None, 1, V_DIM), lambda i, j: (layer, 0, 0)),
    ]
    args += [lam_qk, subln_g.reshape(DEPTH, 1, V_DIM)]
    n_keys = n + (ctx_kt.shape[3] if has_ctx else 0)
    scratch = [pltpu.VMEM((nb, N_HEADS, n_keys, 2 * V_DIM), BF16)]
    if has_ctx:
        scratch.insert(0, pltpu.VMEM((nb, ATT_DIM, n_keys), BF16))
    return pl.pallas_call(
        functools.partial(_attention_kernel, has_ctx=has_ctx, lambda_init=_lambda_init(layer), nb=nb, n=n),
        out_shape=jax.ShapeDtypeStruct((b, n, ATT_DIM), BF16),
        grid=(b // nb, n // tq),
        in_specs=in_specs,
        out_specs=pl.BlockSpec((nb, tq, ATT_DIM), lambda i, j: (i, j, 0)),
        scratch_shapes=scratch,
        compiler_params=_params(("arbitrary", "arbitrary")),
        name="attention",
    )(*args)


def _mixer_out_kernel(x_ref, att_ref, gb_ref, u_ref, xcs_ref, dft_ref, cw_ref, mod_ref, g_ref, wout_ref,
                      o_ref, *, seqs, tr, n, parts):
    sub = tr // parts
    gate_g = g_ref[3:4, :] * mod_ref[5:6, :]
    cw = cw_ref[...]
    row = lax.broadcasted_iota(jnp.int32, (sub, 1), 0)
    four_scale = (n * FOURIER_GROUP_DIM) ** -0.5

    def mix(s, t):
        rows = slice(t * sub, (t + 1) * sub)
        r0 = pl.multiple_of(pl.program_id(1) * tr + t * sub, sub)
        four = (_dot(dft_ref[pl.ds(r0, sub), 0:n], xcs_ref[s, :, 0:FOURIER_DIM])
                + _dot(dft_ref[pl.ds(r0, sub), n:2 * n], xcs_ref[s, :, FOURIER_DIM:2 * FOURIER_DIM])
                ) * four_scale
        u = u_ref[s, pl.ds(r0, sub), :]
        before_start = pl.multiple_of(jnp.maximum(r0 - SUBLANES, 0), SUBLANES)
        after_start = pl.multiple_of(jnp.minimum(r0 + sub, n - SUBLANES), SUBLANES)
        before = u_ref[s, pl.ds(before_start, SUBLANES), :][SUBLANES - 1:SUBLANES, :]
        after = u_ref[s, pl.ds(after_start, SUBLANES), :][0:1, :]
        before = jnp.where(r0 > 0, before, 0.0)
        after = jnp.where(r0 + sub < n, after, 0.0)
        prev = jnp.where(row == 0, before, pltpu.roll(u, 1, 0))
        nxt = jnp.where(row == sub - 1, after, pltpu.roll(u, sub - 1, 0))
        conv = gb_ref[s, rows, :] * (prev * cw[0:1, :] + u * cw[1:2, :] + nxt * cw[2:3, :])
        return conv.astype(BF16), four.astype(BF16)

    def project(s, t, conv, four):
        rows = slice(t * sub, (t + 1) * sub)
        y = (_dot(att_ref[s, rows, :], wout_ref[0:ATT_DIM, :])
             + _dot(conv, wout_ref[ATT_DIM:ATT_DIM + CONV_DIM, :])
             + _dot(four, wout_ref[ATT_DIM + CONV_DIM:D_MODEL, :]))
        o_ref[s, rows, :] = x_ref[s, rows, :] + _rms(y, gate_g)

    items = [(s, t) for s in range(seqs) for t in range(parts)]
    pending = mix(*items[0])
    for idx, item in enumerate(items):
        upcoming = mix(*items[idx + 1]) if idx + 1 < len(items) else None
        project(*item, *pending)
        pending = upcoming


def _mixer_out(x, att, gb, u, xcs, dft, conv_w, mods, norm_g, wout, *, layer, mod_row_of_batch, seqs, tr,
               parts):
    b, n, _ = x.shape

    def tile(width):
        return pl.BlockSpec((seqs, tr, width), lambda i, j: (i, j, 0))

    def whole(width):
        return pl.BlockSpec((seqs, n, width), lambda i, j: (i, 0, 0))

    return pl.pallas_call(
        functools.partial(_mixer_out_kernel, seqs=seqs, tr=tr, n=n, parts=parts),
        out_shape=jax.ShapeDtypeStruct((b, n, D_MODEL), F32),
        grid=(b // seqs, n // tr),
        in_specs=[
            tile(D_MODEL), tile(ATT_DIM), tile(CONV_DIM), whole(CONV_DIM), whole(2 * FOURIER_DIM),
            _resident((n, 2 * n), lambda i, j: (0, 0)),
            pl.BlockSpec((None, 3, CONV_DIM), lambda i, j: (layer, 0, 0)),
            pl.BlockSpec((None, None, N_MOD, D_MODEL), lambda i, j: (layer, mod_row_of_batch(i), 0, 0)),
            pl.BlockSpec((None, 6, D_MODEL), lambda i, j: (layer, 0, 0)),
            _resident((None, D_MODEL, D_MODEL), lambda i, j: (layer, 0, 0)),
        ],
        out_specs=tile(D_MODEL),
        compiler_params=_params(("arbitrary", "arbitrary")),
        name="mixer_out",
    )(x, att, gb, u, xcs, dft, conv_w, mods, norm_g, wout)


def _position_dft(n):
    jk = (np.arange(n, dtype=np.int64)[:, None] * np.arange(n, dtype=np.int64)[None, :]) % n
    ang = 2.0 * np.pi * jk.astype(np.float64) / n
    return np.concatenate([np.cos(ang), -np.sin(ang)], axis=1).astype(np.float32)


def _channel_dft():
    g = FOURIER_GROUP_DIM
    jk = (np.arange(g)[:, None] * np.arange(g)[None, :]) % g
    ang = 2.0 * np.pi * jk.astype(np.float64) / g
    eye = np.eye(FOURIER_DIM // g)
    return np.concatenate([np.kron(eye, np.cos(ang)), np.kron(eye, np.sin(ang))], axis=1).astype(np.float32)


def _rope_tables(n):
    t = jnp.arange(n)
    row = (t // GRID_W).astype(F32)
    col = (t % GRID_W).astype(F32)
    inv = 1.0 / (ROPE_THETA ** (jnp.arange(ROT_FREQS, dtype=F32) / ROT_FREQS))
    ang_r = row[:, None] * inv
    ang_c = col[:, None] * inv
    cos = jnp.concatenate([jnp.cos(ang_r)] * 2 + [jnp.cos(ang_c)] * 2, axis=-1)
    sin = jnp.concatenate([-jnp.sin(ang_r), jnp.sin(ang_r), -jnp.sin(ang_c), jnp.sin(ang_c)], axis=-1)
    return jnp.tile(cos, (1, 2)), jnp.tile(sin, (1, 2))


def _run_path(x, mods, mod_row, weights, ctx, rope_tabs, dft, *, tm, tq, seqs, tr, parts):
    norm_g, wgu, wd, win, wout, cdft, conv_w, lam_qk, subln_g = weights
    b, n, _ = x.shape
    tiles_per_seq = max(n // tm, 1)
    seqs_per_tile = max(tm // n, 1)

    def mod_row_of_tile(i):
        return mod_row(i * seqs_per_tile // tiles_per_seq)

    def shape3(a):
        return a.reshape(b, n, a.shape[-1])

    ctx_k, ctx_v = ctx if ctx is not None else (None, None)
    caches = None
    for l in range(DEPTH):
        flat = x.reshape(b * n, D_MODEL)
        flat = _ffn(flat, mods, norm_g, wgu, wd, layer=l, half=0, mod_row_of_tile=mod_row_of_tile, tm=tm)
        q, kt, v, gb, u, xcs, *caches = _mixer_in(flat, mods, norm_g, win, cdft, rope_tabs, caches, layer=l,
                                                  mod_row_of_tile=mod_row_of_tile, tm=tm, seq=n)
        caches = caches or None
        att = _attention(shape3(q), kt, shape3(v), ctx_k, ctx_v, lam_qk, subln_g, layer=l, tq=tq, nb=seqs)
        x = _mixer_out(shape3(flat), att, shape3(gb), shape3(u), shape3(xcs), dft, conv_w, mods, norm_g,
                       wout, layer=l, mod_row_of_batch=lambda i: mod_row(i * seqs), seqs=seqs, tr=tr,
                       parts=parts)
        flat = _ffn(x.reshape(b * n, D_MODEL), mods, norm_g, wgu, wd, layer=l, half=1,
                    mod_row_of_tile=mod_row_of_tile, tm=tm)
        x = flat.reshape(b, n, D_MODEL)
    return x, caches


def kernel(x_prompt, x_sample, cache_k, cache_v, c, c_ctx, w_mod, b_mod, norm_g, w_ffn_gu, w_ffn_down,
           w_in, w_out, conv_w, lam_qk, subln_g):
    batch, seq, _ = x_prompt.shape
    dec_batch, dec_seq, _ = x_sample.shape
    past = cache_k.shape[2]

    cvecs = jnp.zeros((MOD_ROWS, D_MODEL), F32).at[0].set(c_ctx).at[1:1 + dec_batch].set(c)
    mods = _modulation(cvecs, w_mod, b_mod)

    wgu, wd = w_ffn_gu, w_ffn_down
    win = _cast_bf16(w_in.reshape(DEPTH * D_MODEL, IN_DIM), D_MODEL // 2).reshape(DEPTH, D_MODEL, IN_DIM)
    wout = _cast_bf16(w_out.reshape(DEPTH * D_MODEL, D_MODEL), D_MODEL).reshape(DEPTH, D_MODEL, D_MODEL)
    cdft = jnp.asarray(_channel_dft()).astype(BF16)
    weights = (norm_g, wgu, wd, win, wout, cdft, conv_w, lam_qk, subln_g)

    dft_ctx = jnp.asarray(_position_dft(seq)).astype(BF16)
    y_prompt, (kc, vc) = _run_path(x_prompt, mods, lambda i: 0, weights, None, None, dft_ctx,
                                   tm=1024, tq=seq, seqs=4, tr=seq, parts=1)
    new_cache_k = kc.reshape(batch, DEPTH, N_HEADS, 2, QK_DIM, seq).transpose(0, 1, 5, 2, 3, 4)
    new_cache_v = vc.reshape(batch, DEPTH, seq, N_HEADS, V_DIM)

    dft_lat = jnp.asarray(_position_dft(dec_seq)).astype(BF16)
    ctx = (cache_k.transpose(0, 1, 3, 4, 5, 2).reshape(dec_batch, DEPTH, ATT_DIM, past),
           cache_v.reshape(dec_batch, DEPTH, past * N_HEADS, V_DIM))
    y_sample, _ = _run_path(x_sample, mods, lambda i: 1 + i, weights, ctx, _rope_tables(dec_seq), dft_lat,
                            tm=1024, tq=256, seqs=1, tr=512, parts=2)
    return (y_prompt, y_sample, new_cache_k, new_cache_v)
```

```python
import functools
import math

import jax
import jax.numpy as jnp
import numpy as np
from jax import lax
from jax.experimental import pallas as pl
from jax.experimental.pallas import tpu as pltpu

D_MODEL = 1024
DEPTH = 2
GRID_W = 64
V_DIM = 128
QK_DIM = V_DIM // 2
N_HEADS = 4
ATT_DIM = N_HEADS * V_DIM
CONV_DIM = 256
FOURIER_DIM = 256
FOURIER_GROUP_DIM = 64
IN_DIM = 3 * ATT_DIM + 3 * CONV_DIM + FOURIER_DIM
D_FF = 2816
N_MOD = 9
ROT_FREQS = 16
ROPE_THETA = 10000.0
EPS = 1e-6
FFN_RESIDUAL_WEIGHT = 0.5
LOG2_E = math.log2(math.e)

FF_CHUNK = 256
N_FF_CHUNKS = D_FF // FF_CHUNK
FF_GROUP = 3
assert (N_FF_CHUNKS - 2) % FF_GROUP == 0
MOD_ROWS = 8
SUBLANES = 8
LANES = 128
VMEM_LIMIT = 56 * 1024 * 1024

F32 = jnp.float32
BF16 = jnp.bfloat16


def _lambda_init(layer_idx):
    return 0.8 - 0.6 * math.exp(-0.3 * layer_idx)


def _rms(x, g):
    return x * lax.rsqrt(jnp.mean(x * x, axis=-1, keepdims=True) + EPS) * g


def _dot(a, b):
    return jnp.dot(a, b, preferred_element_type=F32)


def _params(semantics):
    return pltpu.CompilerParams(dimension_semantics=semantics, vmem_limit_bytes=VMEM_LIMIT)


def _resident(block_shape, index_map):
    return pl.BlockSpec(block_shape, index_map, pipeline_mode=pl.Buffered(1))


def _cast_kernel(x_ref, o_ref):
    o_ref[...] = x_ref[...].astype(o_ref.dtype)


def _cast_bf16(x, block_rows):
    rows, cols = x.shape
    return pl.pallas_call(
        _cast_kernel,
        out_shape=jax.ShapeDtypeStruct((rows, cols), BF16),
        grid=(rows // block_rows,),
        in_specs=[pl.BlockSpec((block_rows, cols), lambda i: (i, 0))],
        out_specs=pl.BlockSpec((block_rows, cols), lambda i: (i, 0)),
        compiler_params=_params(("arbitrary",)),
        name="cast_bf16",
    )(x)


def _mod_kernel(c_ref, w_ref, b_ref, o_ref):
    c = c_ref[...]
    s = (c / (1.0 + jnp.exp(-c))).astype(BF16)
    o_ref[...] = _dot(s, w_ref[...].astype(BF16)) + b_ref[...]


def _modulation(cvecs, w_mod, b_mod):
    col = D_MODEL
    out = pl.pallas_call(
        _mod_kernel,
        out_shape=jax.ShapeDtypeStruct((DEPTH, MOD_ROWS, N_MOD * D_MODEL), F32),
        grid=(DEPTH, N_MOD * D_MODEL // col),
        in_specs=[
            pl.BlockSpec((MOD_ROWS, D_MODEL), lambda l, j: (0, 0)),
            pl.BlockSpec((None, D_MODEL, col), lambda l, j: (l, 0, j)),
            pl.BlockSpec((None, 1, col), lambda l, j: (l, 0, j)),
        ],
        out_specs=pl.BlockSpec((None, MOD_ROWS, col), lambda l, j: (l, 0, j)),
        compiler_params=_params(("arbitrary", "arbitrary")),
        name="modulation",
    )(cvecs, w_mod, b_mod.reshape(DEPTH, 1, N_MOD * D_MODEL))
    return out.reshape(DEPTH, MOD_ROWS, N_MOD, D_MODEL)


def _ffn_kernel(x_ref, mod_ref, g_ref, wg_ref, wu_ref, wdn_ref, o_ref, acc_ref, hb_ref, wgu_sc, wd_sc,
                *, sub, g_row):
    step = pl.program_id(0)
    last = N_FF_CHUNKS - 1
    shift = mod_ref[3 * sub:3 * sub + 1, :]
    scale = mod_ref[3 * sub + 1:3 * sub + 2, :]
    gate = mod_ref[3 * sub + 2:3 * sub + 3, :]

    def pre_norm():
        h = _rms(x_ref[...], g_ref[g_row:g_row + 1, :] * (1.0 + scale)) + shift
        return h.astype(BF16)

    def hidden(hb, c):
        gu = _dot(hb, wgu_sc[c])
        g = gu[:, :FF_CHUNK]
        u = gu[:, FF_CHUNK:]
        return (g / (1.0 + jnp.exp(-g)) * u).astype(BF16)

    def finish(y):
        o_ref[...] = x_ref[...] + _rms(y, g_ref[g_row + 1:g_row + 2, :] * (FFN_RESIDUAL_WEIGHT * gate))

    @pl.when(step <= last)
    def _():
        @pl.when(step == 0)
        def _():
            hb_ref[...] = pre_norm()
            acc_ref[...] = jnp.zeros_like(acc_ref)

        wgu_sc[step, :, 0:FF_CHUNK] = wg_ref[...].astype(BF16)
        wgu_sc[step, :, FF_CHUNK:2 * FF_CHUNK] = wu_ref[...].astype(BF16)
        wd_sc[step] = wdn_ref[...].astype(BF16)
        acc_ref[...] += _dot(hidden(hb_ref[...], step), wd_sc[step])

        @pl.when(step == last)
        def _():
            finish(acc_ref[...])

    @pl.when(step > last)
    def _():
        hb = pre_norm()
        acc_ref[...] = _dot(hidden(hb, 0), wd_sc[0])

        def group(i, carry):
            c0 = 1 + i * FF_GROUP
            a = jnp.concatenate([hidden(hb, c0 + k) for k in range(FF_GROUP)], axis=-1)
            wd = wd_sc[pl.ds(c0, FF_GROUP)].reshape(FF_GROUP * FF_CHUNK, D_MODEL)
            acc_ref[...] += _dot(a, wd)
            return carry

        lax.fori_loop(0, (N_FF_CHUNKS - 2) // FF_GROUP, group, 0)
        finish(acc_ref[...] + _dot(hidden(hb, last), wd_sc[last]))


def _ffn(x, mods, norm_g, w_gu, w_down, *, layer, half, mod_row_of_tile, tm):
    n = x.shape[0]
    sub = 2 * half
    g_row = 4 * half
    last = N_FF_CHUNKS - 1

    def tile(s):
        return jnp.maximum(s - last, 0)

    def chunk(s):
        return jnp.minimum(s, last)

    return pl.pallas_call(
        functools.partial(_ffn_kernel, sub=sub, g_row=g_row),
        out_shape=jax.ShapeDtypeStruct((n, D_MODEL), F32),
        grid=(n // tm + last,),
        in_specs=[
            pl.BlockSpec((tm, D_MODEL), lambda s: (tile(s), 0)),
            pl.BlockSpec((None, None, N_MOD, D_MODEL), lambda s: (layer, mod_row_of_tile(tile(s)), 0, 0)),
            pl.BlockSpec((None, 6, D_MODEL), lambda s: (layer, 0, 0)),
            pl.BlockSpec((None, None, D_MODEL, FF_CHUNK), lambda s: (layer, half, 0, chunk(s))),
            pl.BlockSpec((None, None, D_MODEL, FF_CHUNK), lambda s: (layer, half, 0, N_FF_CHUNKS + chunk(s))),
            pl.BlockSpec((None, None, FF_CHUNK, D_MODEL), lambda s: (layer, half, chunk(s), 0)),
        ],
        out_specs=pl.BlockSpec((tm, D_MODEL), lambda s: (tile(s), 0)),
        scratch_shapes=[
            pltpu.VMEM((tm, D_MODEL), F32),
            pltpu.VMEM((tm, D_MODEL), BF16),
            pltpu.VMEM((N_FF_CHUNKS, D_MODEL, 2 * FF_CHUNK), BF16),
            pltpu.VMEM((N_FF_CHUNKS, FF_CHUNK, D_MODEL), BF16),
        ],
        compiler_params=_params(("arbitrary",)),
        name="ffn",
    )(x, mods, norm_g, w_gu, w_gu, w_down)


def _rope(x, cos, sin_signed, lo_half):
    partner = jnp.where(lo_half, pltpu.roll(x, LANES - ROT_FREQS, 1), pltpu.roll(x, ROT_FREQS, 1))
    return x * cos + partner * sin_signed


def _mixer_in_kernel(*refs, rotary, seq, cache_slot):
    x_ref, mod_ref, g_ref, win_ref, cdft_ref = refs[:5]
    if rotary:
        cos_ref, sin_ref = refs[5:7]
        q_ref, kt_ref, v_ref, gb_ref, u_ref, xcs_ref = refs[7:]
    else:
        q_ref, kt_ref, v_ref, gb_ref, u_ref, xcs_ref, kc_ref, vc_ref = refs[-8:]
    x = x_ref[...]
    shift = mod_ref[3:4, :]
    scale = mod_ref[4:5, :]
    h = _rms(x, g_ref[2:3, :]) * (1.0 + scale) + shift
    hb = h.astype(BF16)

    q = _dot(hb, win_ref[:, 0:ATT_DIM]) * (QK_DIM ** -0.5 * LOG2_E)
    k = _dot(hb, win_ref[:, ATT_DIM:2 * ATT_DIM])
    v = _dot(hb, win_ref[:, 2 * ATT_DIM:3 * ATT_DIM])
    v_ref[...] = v.astype(v_ref.dtype)
    if rotary:
        cos = cos_ref[...]
        sin = sin_ref[...]
        lane = lax.broadcasted_iota(jnp.int32, (1, LANES), 1)
        lo_half = (lane % (2 * ROT_FREQS)) < ROT_FREQS
        for hd in range(N_HEADS):
            sl = slice(hd * V_DIM, (hd + 1) * V_DIM)
            q_ref[:, sl] = _rope(q[:, sl], cos, sin, lo_half).astype(q_ref.dtype)
            kt_ref[sl, :] = _rope(k[:, sl], cos, sin, lo_half).T.astype(kt_ref.dtype)
    else:
        q_ref[...] = q.astype(q_ref.dtype)
        for s in range(x.shape[0] // seq):
            rows = slice(s * seq, (s + 1) * seq)
            kt = k[rows, :].T
            kt_ref[s] = kt.astype(kt_ref.dtype)
            if cache_slot is None:
                kc_out, vc_out = kc_ref.at[s], vc_ref.at[s]
            else:
                kc_out, vc_out = kc_ref.at[s, cache_slot], vc_ref.at[s, cache_slot]
                for other in range(DEPTH):
                    if other != cache_slot:
                        kc_ref[s, other] = jnp.zeros(kc_ref.shape[2:], F32)
                        vc_ref[s, other] = jnp.zeros(vc_ref.shape[2:], F32)
            kc_out[...] = kt
            for hd in range(N_HEADS):
                vc_out[pl.ds(hd, seq, stride=N_HEADS), :] = v[rows, hd * V_DIM:(hd + 1) * V_DIM]

    c0 = 3 * ATT_DIM
    gates = _dot(hb, win_ref[:, c0:c0 + 3 * CONV_DIM])
    gb_ref[...] = gates[:, 0:CONV_DIM]
    u_ref[...] = gates[:, CONV_DIM:2 * CONV_DIM] * gates[:, 2 * CONV_DIM:3 * CONV_DIM]
    f = _dot(hb, win_ref[:, c0 + 3 * CONV_DIM:IN_DIM])
    xcs_ref[...] = _dot(f.astype(BF16), cdft_ref[...]).astype(BF16)


def _mixer_in(x, mods, norm_g, win, cdft, rope_tabs, caches, *, layer, mod_row_of_tile, tm, seq):
    n = x.shape[0]
    rotary = rope_tabs is not None
    in_specs = [
        pl.BlockSpec((tm, D_MODEL), lambda i: (i, 0)),
        pl.BlockSpec((None, None, N_MOD, D_MODEL), lambda i: (layer, mod_row_of_tile(i), 0, 0)),
        pl.BlockSpec((None, 6, D_MODEL), lambda i: (layer, 0, 0)),
        _resident((None, D_MODEL, IN_DIM), lambda i: (layer, 0, 0)),
        _resident((FOURIER_DIM, 2 * FOURIER_DIM), lambda i: (0, 0)),
    ]
    args = [x, mods, norm_g, win, cdft]
    if rotary:
        tiles_per_seq = rope_tabs[0].shape[0] // tm
        for tab in rope_tabs:
            in_specs.append(pl.BlockSpec((tm, LANES), lambda i: (i % tiles_per_seq, 0)))
            args.append(tab)

    def rows(width):
        return pl.BlockSpec((tm, width), lambda i: (i, 0))

    batch = n // seq
    if tm <= seq:
        tiles_per_seq = seq // tm
        kt_spec = pl.BlockSpec((None, ATT_DIM, tm), lambda i: (i // tiles_per_seq, 0, i % tiles_per_seq))
    else:
        kt_spec = pl.BlockSpec((tm // seq, ATT_DIM, seq), lambda i: (i, 0, 0))
    out_shape = [
        jax.ShapeDtypeStruct((n, ATT_DIM), BF16),
        jax.ShapeDtypeStruct((batch, ATT_DIM, seq), BF16),
        jax.ShapeDtypeStruct((n, ATT_DIM), BF16),
        jax.ShapeDtypeStruct((n, CONV_DIM), F32),
        jax.ShapeDtypeStruct((n, CONV_DIM), F32),
        jax.ShapeDtypeStruct((n, 2 * FOURIER_DIM), BF16),
    ]
    out_specs = [rows(ATT_DIM), kt_spec, rows(ATT_DIM), rows(CONV_DIM), rows(CONV_DIM),
                 rows(2 * FOURIER_DIM)]
    aliases = {}
    if not rotary:
        seqs = tm // seq
        out_shape += [jax.ShapeDtypeStruct((batch, DEPTH, ATT_DIM, seq), F32),
                      jax.ShapeDtypeStruct((batch, DEPTH, N_HEADS * seq, V_DIM), F32)]
        if caches is None:
            out_specs += [pl.BlockSpec((seqs, DEPTH, ATT_DIM, seq), lambda i: (i, 0, 0, 0)),
                          pl.BlockSpec((seqs, DEPTH, N_HEADS * seq, V_DIM), lambda i: (i, 0, 0, 0))]
        else:
            out_specs += [pl.BlockSpec((seqs, None, ATT_DIM, seq), lambda i: (i, layer, 0, 0)),
                          pl.BlockSpec((seqs, None, N_HEADS * seq, V_DIM), lambda i: (i, layer, 0, 0))]
            first_cache_out = len(out_shape) - 2
            for offset, arr in enumerate(caches):
                aliases[len(args)] = first_cache_out + offset
                in_specs.append(pl.BlockSpec(memory_space=pl.ANY))
                args.append(arr)

    cache_slot = layer if (not rotary and caches is None) else None
    return pl.pallas_call(
        functools.partial(_mixer_in_kernel, rotary=rotary, seq=seq, cache_slot=cache_slot),
        out_shape=out_shape,
        grid=(n // tm,),
        in_specs=in_specs,
        out_specs=out_specs,
        input_output_aliases=aliases,
        compiler_params=_params(("arbitrary",)),
        name="mixer_in",
    )(*args)


def _lambda(lam_ref, lambda_init):
    lq = lam_ref[...]
    s01 = jnp.sum(lq[0:1, :] * lq[1:2, :], axis=-1, keepdims=True)
    s23 = jnp.sum(lq[2:3, :] * lq[3:4, :], axis=-1, keepdims=True)
    return jnp.exp(s01) - jnp.exp(s23) + lambda_init


def _stage_values(vext_sc, v_ref, cv_ref, nb, n):
    for b in range(nb):
        for hd in range(N_HEADS):
            sl = slice(hd * V_DIM, (hd + 1) * V_DIM)
            vext_sc[b, hd, 0:n, 0:V_DIM] = v_ref[b, :, sl]
            if cv_ref is not None:
                past = vext_sc.shape[2] - n
                vext_sc[b, hd, n:, 0:V_DIM] = cv_ref[pl.ds(hd, past, stride=N_HEADS), :].astype(BF16)
            vext_sc[b, hd, :, V_DIM:] = jnp.ones((vext_sc.shape[2], V_DIM), BF16)


def _attend(q_ref, kt_sc, vext_sc, o_ref, lam, sub_g, nb):
    lane = lax.broadcasted_iota(jnp.int32, (1, V_DIM), 1)
    first_map = lane < QK_DIM

    def exponentials(b, hd):
        sl = slice(hd * V_DIM, (hd + 1) * V_DIM)
        q = q_ref[b, :, sl]
        zero = jnp.zeros_like(q)
        out = []
        for keep in (first_map, jnp.logical_not(first_map)):
            qm = jnp.where(keep, q, zero)
            s = _dot(qm, kt_sc[b, sl, :])
            m = jnp.max(s, axis=-1, keepdims=True)
            out.append(jnp.exp2((s - m).astype(BF16)))
        return out

    def finish(b, hd, e_maps):
        sl = slice(hd * V_DIM, (hd + 1) * V_DIM)
        vext = vext_sc[b, hd]
        r0, r1 = [_dot(e, vext) for e in e_maps]
        o = r0[:, :V_DIM] / r0[:, V_DIM:] - lam * (r1[:, :V_DIM] / r1[:, V_DIM:])
        o_ref[b, :, sl] = (_rms(o, sub_g)).astype(o_ref.dtype)

    items = [(b, hd) for b in range(nb) for hd in range(N_HEADS)]
    pending = exponentials(*items[0])
    for idx, item in enumerate(items):
        upcoming = exponentials(*items[idx + 1]) if idx + 1 < len(items) else None
        finish(*item, pending)
        pending = upcoming


def _attention_kernel(*refs, has_ctx, lambda_init, nb, n):
    if has_ctx:
        q_ref, kt_ref, v_ref, ckt_ref, cv_ref, lam_ref, sg_ref, o_ref, kt_sc, vext_sc = refs
    else:
        q_ref, kt_ref, v_ref, lam_ref, sg_ref, o_ref, vext_sc = refs
        kt_sc, cv_ref = kt_ref, None

    @pl.when(pl.program_id(1) == 0)
    def _():
        if has_ctx:
            for b in range(nb):
                kt_sc[b, :, 0:n] = kt_ref[b]
                kt_sc[b, :, n:] = ckt_ref[...].astype(BF16)
        _stage_values(vext_sc, v_ref, cv_ref, nb, n)

    _attend(q_ref, kt_sc, vext_sc, o_ref, _lambda(lam_ref, lambda_init), sg_ref[...] * (1.0 - lambda_init), nb)


def _attention(q, kt, v, ctx_kt, ctx_v, lam_qk, subln_g, *, layer, tq, nb):
    b, n, _ = q.shape
    has_ctx = ctx_kt is not None
    assert not has_ctx or nb == 1
    in_specs = [
        pl.BlockSpec((nb, tq, ATT_DIM), lambda i, j: (i, j, 0)),
        pl.BlockSpec((nb, ATT_DIM, n), lambda i, j: (i, 0, 0)),
        pl.BlockSpec((nb, n, ATT_DIM), lambda i, j: (i, 0, 0)),
    ]
    args = [q, kt, v]
    if has_ctx:
        for arr in (ctx_kt, ctx_v):
            in_specs.append(pl.BlockSpec((None, None) + arr.shape[2:], lambda i, j: (i, layer, 0, 0)))
            args.append(arr)
    in_specs += [
        pl.BlockSpec((None, 4, QK_DIM), lambda i, j: (layer, 0, 0)),
        pl.BlockSpec((None, 1, V_DIM), lambda i, j: (layer, 0, 0)),
    ]
    args += [lam_qk, subln_g.reshape(DEPTH, 1, V_DIM)]
    n_keys = n + (ctx_kt.shape[3] if has_ctx else 0)
    scratch = [pltpu.VMEM((nb, N_HEADS, n_keys, 2 * V_DIM), BF16)]
    if has_ctx:
        scratch.insert(0, pltpu.VMEM((nb, ATT_DIM, n_keys), BF16))
    return pl.pallas_call(
        functools.partial(_attention_kernel, has_ctx=has_ctx, lambda_init=_lambda_init(layer), nb=nb, n=n),
        out_shape=jax.ShapeDtypeStruct((b, n, ATT_DIM), BF16),
        grid=(b // nb, n // tq),
        in_specs=in_specs,
        out_specs=pl.BlockSpec((nb, tq, ATT_DIM), lambda i, j: (i, j, 0)),
        scratch_shapes=scratch,
        compiler_params=_params(("arbitrary", "arbitrary")),
        name="attention",
    )(*args)


def _mixer_out_kernel(*refs, seqs, tr, n, parts, lambda_init):
    if lambda_init is None:
        x_ref, att_ref, gb_ref, u_ref, xcs_ref, dft_ref, cw_ref, mod_ref, g_ref, wout_ref, o_ref = refs
    else:
        (x_ref, q_ref, kt_ref, v_ref, lam_ref, sg_ref, gb_ref, u_ref, xcs_ref, dft_ref, cw_ref, mod_ref, g_ref,
         wout_ref, o_ref, att_ref, vext_sc) = refs
        _stage_values(vext_sc, v_ref, None, seqs, n)
        _attend(q_ref, kt_ref, vext_sc, att_ref, _lambda(lam_ref, lambda_init),
                sg_ref[...] * (1.0 - lambda_init), seqs)
    sub = tr // parts
    gate_g = g_ref[3:4, :] * mod_ref[5:6, :]
    cw = cw_ref[...]
    row = lax.broadcasted_iota(jnp.int32, (sub, 1), 0)
    four_scale = (n * FOURIER_GROUP_DIM) ** -0.5

    def mix(s, t):
        rows = slice(t * sub, (t + 1) * sub)
        r0 = pl.multiple_of(pl.program_id(1) * tr + t * sub, sub)
        four = (_dot(dft_ref[pl.ds(r0, sub), 0:n], xcs_ref[s, :, 0:FOURIER_DIM])
                + _dot(dft_ref[pl.ds(r0, sub), n:2 * n], xcs_ref[s, :, FOURIER_DIM:2 * FOURIER_DIM])
                ) * four_scale
        u = u_ref[s, pl.ds(r0, sub), :]
        before_start = pl.multiple_of(jnp.maximum(r0 - SUBLANES, 0), SUBLANES)
        after_start = pl.multiple_of(jnp.minimum(r0 + sub, n - SUBLANES), SUBLANES)
        before = u_ref[s, pl.ds(before_start, SUBLANES), :][SUBLANES - 1:SUBLANES, :]
        after = u_ref[s, pl.ds(after_start, SUBLANES), :][0:1, :]
        before = jnp.where(r0 > 0, before, 0.0)
        after = jnp.where(r0 + sub < n, after, 0.0)
        prev = jnp.where(row == 0, before, pltpu.roll(u, 1, 0))
        nxt = jnp.where(row == sub - 1, after, pltpu.roll(u, sub - 1, 0))
        conv = gb_ref[s, rows, :] * (prev * cw[0:1, :] + u * cw[1:2, :] + nxt * cw[2:3, :])
        return conv.astype(BF16), four.astype(BF16)

    def project(s, t, conv, four):
        rows = slice(t * sub, (t + 1) * sub)
        y = (_dot(att_ref[s, rows, :], wout_ref[0:ATT_DIM, :])
             + _dot(conv, wout_ref[ATT_DIM:ATT_DIM + CONV_DIM, :])
             + _dot(four, wout_ref[ATT_DIM + CONV_DIM:D_MODEL, :]))
        o_ref[s, rows, :] = x_ref[s, rows, :] + _rms(y, gate_g)

    items = [(s, t) for s in range(seqs) for t in range(parts)]
    pending = mix(*items[0])
    for idx, item in enumerate(items):
        upcoming = mix(*items[idx + 1]) if idx + 1 < len(items) else None
        project(*item, *pending)
        pending = upcoming


def _mixer_out(x, att, gb, u, xcs, dft, conv_w, mods, norm_g, wout, *, layer, mod_row_of_batch, seqs, tr,
               parts, attention_inputs=None):
    b, n, _ = x.shape
    fused = attention_inputs is not None
    assert not fused or (tr == n and att is None)

    def tile(width):
        return pl.BlockSpec((seqs, tr, width), lambda i, j: (i, j, 0))

    def whole(width):
        return pl.BlockSpec((seqs, n, width), lambda i, j: (i, 0, 0))

    if fused:
        q, kt, v, lam_qk, subln_g = attention_inputs
        att_specs = [tile(ATT_DIM), pl.BlockSpec((seqs, ATT_DIM, n), lambda i, j: (i, 0, 0)), tile(ATT_DIM),
                     pl.BlockSpec((None, 4, QK_DIM), lambda i, j: (layer, 0, 0)),
                     pl.BlockSpec((None, 1, V_DIM), lambda i, j: (layer, 0, 0))]
        att_args = [q, kt, v, lam_qk, subln_g.reshape(DEPTH, 1, V_DIM)]
        scratch = [pltpu.VMEM((seqs, n, ATT_DIM), BF16), pltpu.VMEM((seqs, N_HEADS, n, 2 * V_DIM), BF16)]
    else:
        att_specs, att_args, scratch = [tile(ATT_DIM)], [att], []

    return pl.pallas_call(
        functools.partial(_mixer_out_kernel, seqs=seqs, tr=tr, n=n, parts=parts,
                          lambda_init=_lambda_init(layer) if fused else None),
        out_shape=jax.ShapeDtypeStruct((b, n, D_MODEL), F32),
        grid=(b // seqs, n // tr),
        in_specs=[tile(D_MODEL)] + att_specs + [
            tile(CONV_DIM), whole(CONV_DIM), whole(2 * FOURIER_DIM),
            _resident((n, 2 * n), lambda i, j: (0, 0)),
            pl.BlockSpec((None, 3, CONV_DIM), lambda i, j: (layer, 0, 0)),
            pl.BlockSpec((None, None, N_MOD, D_MODEL), lambda i, j: (layer, mod_row_of_batch(i), 0, 0)),
            pl.BlockSpec((None, 6, D_MODEL), lambda i, j: (layer, 0, 0)),
            _resident((None, D_MODEL, D_MODEL), lambda i, j: (layer, 0, 0)),
        ],
        out_specs=tile(D_MODEL),
        scratch_shapes=scratch,
        compiler_params=_params(("arbitrary", "arbitrary")),
        name="mixer_out",
    )(x, *att_args, gb, u, xcs, dft, conv_w, mods, norm_g, wout)


def _position_dft(n):
    jk = (np.arange(n, dtype=np.int64)[:, None] * np.arange(n, dtype=np.int64)[None, :]) % n
    ang = 2.0 * np.pi * jk.astype(np.float64) / n
    return np.concatenate([np.cos(ang), -np.sin(ang)], axis=1).astype(np.float32)


def _channel_dft():
    g = FOURIER_GROUP_DIM
    jk = (np.arange(g)[:, None] * np.arange(g)[None, :]) % g
    ang = 2.0 * np.pi * jk.astype(np.float64) / g
    eye = np.eye(FOURIER_DIM // g)
    return np.concatenate([np.kron(eye, np.cos(ang)), np.kron(eye, np.sin(ang))], axis=1).astype(np.float32)


def _rope_tables(n):
    t = jnp.arange(n)
    row = (t // GRID_W).astype(F32)
    col = (t % GRID_W).astype(F32)
    inv = 1.0 / (ROPE_THETA ** (jnp.arange(ROT_FREQS, dtype=F32) / ROT_FREQS))
    ang_r = row[:, None] * inv
    ang_c = col[:, None] * inv
    cos = jnp.concatenate([jnp.cos(ang_r)] * 2 + [jnp.cos(ang_c)] * 2, axis=-1)
    sin = jnp.concatenate([-jnp.sin(ang_r), jnp.sin(ang_r), -jnp.sin(ang_c), jnp.sin(ang_c)], axis=-1)
    return jnp.tile(cos, (1, 2)), jnp.tile(sin, (1, 2))


def _run_path(x, mods, mod_row, weights, ctx, rope_tabs, dft, *, tm, tq, seqs, tr, parts):
    norm_g, wgu, wd, win, wout, cdft, conv_w, lam_qk, subln_g = weights
    b, n, _ = x.shape
    tiles_per_seq = max(n // tm, 1)
    seqs_per_tile = max(tm // n, 1)

    def mod_row_of_tile(i):
        return mod_row(i * seqs_per_tile // tiles_per_seq)

    def shape3(a):
        return a.reshape(b, n, a.shape[-1])

    ctx_k, ctx_v = ctx if ctx is not None else (None, None)
    caches = None
    for l in range(DEPTH):
        flat = x.reshape(b * n, D_MODEL)
        flat = _ffn(flat, mods, norm_g, wgu, wd, layer=l, half=0, mod_row_of_tile=mod_row_of_tile, tm=tm)
        q, kt, v, gb, u, xcs, *caches = _mixer_in(flat, mods, norm_g, win, cdft, rope_tabs, caches, layer=l,
                                                  mod_row_of_tile=mod_row_of_tile, tm=tm, seq=n)
        caches = caches or None
        if ctx is None and tr == n:
            att, attention_inputs = None, (shape3(q), kt, shape3(v), lam_qk, subln_g)
        else:
            att = _attention(shape3(q), kt, shape3(v), ctx_k, ctx_v, lam_qk, subln_g, layer=l, tq=tq, nb=seqs)
            attention_inputs = None
        x = _mixer_out(shape3(flat), att, shape3(gb), shape3(u), shape3(xcs), dft, conv_w, mods, norm_g,
                       wout, layer=l, mod_row_of_batch=lambda i: mod_row(i * seqs), seqs=seqs, tr=tr,
                       parts=parts, attention_inputs=attention_inputs)
        flat = _ffn(x.reshape(b * n, D_MODEL), mods, norm_g, wgu, wd, layer=l, half=1,
                    mod_row_of_tile=mod_row_of_tile, tm=tm)
        x = flat.reshape(b, n, D_MODEL)
    return x, caches


def kernel(x_prompt, x_sample, cache_k, cache_v, c, c_ctx, w_mod, b_mod, norm_g, w_ffn_gu, w_ffn_down,
           w_in, w_out, conv_w, lam_qk, subln_g):
    batch, seq, _ = x_prompt.shape
    dec_batch, dec_seq, _ = x_sample.shape
    past = cache_k.shape[2]

    cvecs = jnp.zeros((MOD_ROWS, D_MODEL), F32).at[0].set(c_ctx).at[1:1 + dec_batch].set(c)
    mods = _modulation(cvecs, w_mod, b_mod)

    wgu, wd = w_ffn_gu, w_ffn_down
    win = _cast_bf16(w_in.reshape(DEPTH * D_MODEL, IN_DIM), D_MODEL // 2).reshape(DEPTH, D_MODEL, IN_DIM)
    wout = _cast_bf16(w_out.reshape(DEPTH * D_MODEL, D_MODEL), D_MODEL).reshape(DEPTH, D_MODEL, D_MODEL)
    cdft = jnp.asarray(_channel_dft()).astype(BF16)
    weights = (norm_g, wgu, wd, win, wout, cdft, conv_w, lam_qk, subln_g)

    dft_ctx = jnp.asarray(_position_dft(seq)).astype(BF16)
    y_prompt, (kc, vc) = _run_path(x_prompt, mods, lambda i: 0, weights, None, None, dft_ctx,
                                   tm=1024, tq=seq, seqs=4, tr=seq, parts=1)
    new_cache_k = kc.reshape(batch, DEPTH, N_HEADS, 2, QK_DIM, seq).transpose(0, 1, 5, 2, 3, 4)
    new_cache_v = vc.reshape(batch, DEPTH, seq, N_HEADS, V_DIM)

    dft_lat = jnp.asarray(_position_dft(dec_seq)).astype(BF16)
    ctx = (cache_k.transpose(0, 1, 3, 4, 5, 2).reshape(dec_batch, DEPTH, ATT_DIM, past),
           cache_v.reshape(dec_batch, DEPTH, past * N_HEADS, V_DIM))
    y_sample, _ = _run_path(x_sample, mods, lambda i: 1 + i, weights, ctx, _rope_tables(dec_seq), dft_lat,
                            tm=1024, tq=256, seqs=1, tr=512, parts=2)
    return (y_prompt, y_sample, new_cache_k, new_cache_v)
```

```python
import functools
import math
from typing import NamedTuple

import jax
import jax.numpy as jnp
import numpy as np
from jax import lax
from jax.experimental import pallas as pl
from jax.experimental.pallas import tpu as pltpu

D_MODEL = 1024
DEPTH = 2
GRID_W = 64
V_DIM = 128
QK_DIM = V_DIM // 2
N_HEADS = 4
ATT_DIM = N_HEADS * V_DIM
CONV_DIM = 256
FOURIER_DIM = 256
FOURIER_GROUP_DIM = 64
IN_DIM = 3 * ATT_DIM + 3 * CONV_DIM + FOURIER_DIM
D_FF = 2816
N_MOD = 9
ROT_FREQS = 16
ROPE_THETA = 10000.0
EPS = 1e-6
FFN_RESIDUAL_WEIGHT = 0.5
LOG2_E = math.log2(math.e)

FF_CHUNK = 256
N_FF_CHUNKS = D_FF // FF_CHUNK
FF_GROUP = 3
assert (N_FF_CHUNKS - 2) % FF_GROUP == 0
TOKEN_ROWS = 1024
MIXER_ROWS = 256
MOD_ROWS = 8
SUBLANES = 8
LANES = 128
VMEM_LIMIT = 56 * 1024 * 1024

F32 = jnp.float32
BF16 = jnp.bfloat16


def _lambda_init(layer_idx):
    return 0.8 - 0.6 * math.exp(-0.3 * layer_idx)


def _rms(x, g):
    return x * lax.rsqrt(jnp.mean(x * x, axis=-1, keepdims=True) + EPS) * g


def _dot(a, b):
    return jnp.dot(a, b, preferred_element_type=F32)


def _params(semantics):
    return pltpu.CompilerParams(dimension_semantics=semantics, vmem_limit_bytes=VMEM_LIMIT)


def _resident(block_shape, index_map):
    return pl.BlockSpec(block_shape, index_map, pipeline_mode=pl.Buffered(1))


def _cast_kernel(x_ref, o_ref):
    o_ref[...] = x_ref[...].astype(o_ref.dtype)


def _cast_bf16(x, block_rows):
    rows, cols = x.shape
    return pl.pallas_call(
        _cast_kernel,
        out_shape=jax.ShapeDtypeStruct((rows, cols), BF16),
        grid=(rows // block_rows,),
        in_specs=[pl.BlockSpec((block_rows, cols), lambda i: (i, 0))],
        out_specs=pl.BlockSpec((block_rows, cols), lambda i: (i, 0)),
        compiler_params=_params(("arbitrary",)),
        name="cast_bf16",
    )(x)


def _mod_kernel(c_ref, w_ref, b_ref, o_ref):
    c = c_ref[...]
    s = (c / (1.0 + jnp.exp(-c))).astype(BF16)
    o_ref[...] = _dot(s, w_ref[...].astype(BF16)) + b_ref[...]


def _modulation(cvecs, w_mod, b_mod):
    col = D_MODEL
    out = pl.pallas_call(
        _mod_kernel,
        out_shape=jax.ShapeDtypeStruct((DEPTH, MOD_ROWS, N_MOD * D_MODEL), F32),
        grid=(DEPTH, N_MOD * D_MODEL // col),
        in_specs=[
            pl.BlockSpec((MOD_ROWS, D_MODEL), lambda l, j: (0, 0)),
            pl.BlockSpec((None, D_MODEL, col), lambda l, j: (l, 0, j)),
            pl.BlockSpec((None, 1, col), lambda l, j: (l, 0, j)),
        ],
        out_specs=pl.BlockSpec((None, MOD_ROWS, col), lambda l, j: (l, 0, j)),
        compiler_params=_params(("arbitrary", "arbitrary")),
        name="modulation",
    )(cvecs, w_mod, b_mod.reshape(DEPTH, 1, N_MOD * D_MODEL))
    return out.reshape(DEPTH, MOD_ROWS, N_MOD, D_MODEL)


def _ffn_kernel(x_ref, mod_ref, g_ref, wg_ref, wu_ref, wdn_ref, o_ref, acc_ref, hb_ref, wgu_sc, wd_sc,
                *, sub, g_row):
    step = pl.program_id(0)
    last = N_FF_CHUNKS - 1
    shift = mod_ref[3 * sub:3 * sub + 1, :]
    scale = mod_ref[3 * sub + 1:3 * sub + 2, :]
    gate = mod_ref[3 * sub + 2:3 * sub + 3, :]

    def pre_norm():
        h = _rms(x_ref[...], g_ref[g_row:g_row + 1, :] * (1.0 + scale)) + shift
        return h.astype(BF16)

    def hidden(hb, c):
        gu = _dot(hb, wgu_sc[c])
        g = gu[:, :FF_CHUNK]
        u = gu[:, FF_CHUNK:]
        return (g / (1.0 + jnp.exp(-g)) * u).astype(BF16)

    def finish(y):
        o_ref[...] = x_ref[...] + _rms(y, g_ref[g_row + 1:g_row + 2, :] * (FFN_RESIDUAL_WEIGHT * gate))

    @pl.when(step <= last)
    def _():
        @pl.when(step == 0)
        def _():
            hb_ref[...] = pre_norm()
            acc_ref[...] = jnp.zeros_like(acc_ref)

        wgu_sc[step, :, 0:FF_CHUNK] = wg_ref[...].astype(BF16)
        wgu_sc[step, :, FF_CHUNK:2 * FF_CHUNK] = wu_ref[...].astype(BF16)
        wd_sc[step] = wdn_ref[...].astype(BF16)
        acc_ref[...] += _dot(hidden(hb_ref[...], step), wd_sc[step])

        @pl.when(step == last)
        def _():
            finish(acc_ref[...])

    @pl.when(step > last)
    def _():
        hb = pre_norm()
        acc_ref[...] = _dot(hidden(hb, 0), wd_sc[0])

        def group(i, carry):
            c0 = 1 + i * FF_GROUP
            a = jnp.concatenate([hidden(hb, c0 + k) for k in range(FF_GROUP)], axis=-1)
            wd = wd_sc[pl.ds(c0, FF_GROUP)].reshape(FF_GROUP * FF_CHUNK, D_MODEL)
            acc_ref[...] += _dot(a, wd)
            return carry

        lax.fori_loop(0, (N_FF_CHUNKS - 2) // FF_GROUP, group, 0)
        finish(acc_ref[...] + _dot(hidden(hb, last), wd_sc[last]))


def _ffn(x, mods, norm_g, w_gu, w_down, *, layer, half, mod_row_of_tile, tm):
    n = x.shape[0]
    sub = 2 * half
    g_row = 4 * half
    last = N_FF_CHUNKS - 1

    def tile(s):
        return jnp.maximum(s - last, 0)

    def chunk(s):
        return jnp.minimum(s, last)

    return pl.pallas_call(
        functools.partial(_ffn_kernel, sub=sub, g_row=g_row),
        out_shape=jax.ShapeDtypeStruct((n, D_MODEL), F32),
        grid=(n // tm + last,),
        in_specs=[
            pl.BlockSpec((tm, D_MODEL), lambda s: (tile(s), 0)),
            pl.BlockSpec((None, None, N_MOD, D_MODEL), lambda s: (layer, mod_row_of_tile(tile(s)), 0, 0)),
            pl.BlockSpec((None, 6, D_MODEL), lambda s: (layer, 0, 0)),
            pl.BlockSpec((None, None, D_MODEL, FF_CHUNK), lambda s: (layer, half, 0, chunk(s))),
            pl.BlockSpec((None, None, D_MODEL, FF_CHUNK), lambda s: (layer, half, 0, N_FF_CHUNKS + chunk(s))),
            pl.BlockSpec((None, None, FF_CHUNK, D_MODEL), lambda s: (layer, half, chunk(s), 0)),
        ],
        out_specs=pl.BlockSpec((tm, D_MODEL), lambda s: (tile(s), 0)),
        scratch_shapes=[
            pltpu.VMEM((tm, D_MODEL), F32),
            pltpu.VMEM((tm, D_MODEL), BF16),
            pltpu.VMEM((N_FF_CHUNKS, D_MODEL, 2 * FF_CHUNK), BF16),
            pltpu.VMEM((N_FF_CHUNKS, FF_CHUNK, D_MODEL), BF16),
        ],
        compiler_params=_params(("arbitrary",)),
        name="ffn",
    )(x, mods, norm_g, w_gu, w_gu, w_down)


def _rope(x, cos, sin_signed, lo_half):
    partner = jnp.where(lo_half, pltpu.roll(x, LANES - ROT_FREQS, 1), pltpu.roll(x, ROT_FREQS, 1))
    return x * cos + partner * sin_signed


def _mixer_in_kernel(*refs, rotary, seq, cache_slot):
    x_ref, mod_ref, g_ref, win_ref, cdft_ref = refs[:5]
    if rotary:
        cos_ref, sin_ref = refs[5:7]
        q_ref, kt_ref, v_ref, gb_ref, u_ref, xcs_ref = refs[7:]
    else:
        q_ref, kt_ref, v_ref, gb_ref, u_ref, xcs_ref, kc_ref, vc_ref = refs[-8:]
    x = x_ref[...]
    shift = mod_ref[3:4, :]
    scale = mod_ref[4:5, :]
    h = _rms(x, g_ref[2:3, :]) * (1.0 + scale) + shift
    hb = h.astype(BF16)

    q = _dot(hb, win_ref[:, 0:ATT_DIM]) * (QK_DIM ** -0.5 * LOG2_E)
    k = _dot(hb, win_ref[:, ATT_DIM:2 * ATT_DIM])
    v = _dot(hb, win_ref[:, 2 * ATT_DIM:3 * ATT_DIM])
    v_ref[...] = v.astype(v_ref.dtype)
    if rotary:
        cos = cos_ref[...]
        sin = sin_ref[...]
        lane = lax.broadcasted_iota(jnp.int32, (1, LANES), 1)
        lo_half = (lane % (2 * ROT_FREQS)) < ROT_FREQS
        for hd in range(N_HEADS):
            sl = slice(hd * V_DIM, (hd + 1) * V_DIM)
            q_ref[:, sl] = _rope(q[:, sl], cos, sin, lo_half).astype(q_ref.dtype)
            kt_ref[sl, :] = _rope(k[:, sl], cos, sin, lo_half).T.astype(kt_ref.dtype)
    else:
        q_ref[...] = q.astype(q_ref.dtype)
        for s in range(x.shape[0] // seq):
            rows = slice(s * seq, (s + 1) * seq)
            kt = k[rows, :].T
            kt_ref[s] = kt.astype(kt_ref.dtype)
            if cache_slot is None:
                kc_out, vc_out = kc_ref.at[s], vc_ref.at[s]
            else:
                kc_out, vc_out = kc_ref.at[s, cache_slot], vc_ref.at[s, cache_slot]
                for other in range(DEPTH):
                    if other != cache_slot:
                        kc_ref[s, other] = jnp.zeros(kc_ref.shape[2:], F32)
                        vc_ref[s, other] = jnp.zeros(vc_ref.shape[2:], F32)
            kc_out[...] = kt
            for hd in range(N_HEADS):
                vc_out[pl.ds(hd, seq, stride=N_HEADS), :] = v[rows, hd * V_DIM:(hd + 1) * V_DIM]

    c0 = 3 * ATT_DIM
    gates = _dot(hb, win_ref[:, c0:c0 + 3 * CONV_DIM])
    gb_ref[...] = gates[:, 0:CONV_DIM]
    u_ref[...] = gates[:, CONV_DIM:2 * CONV_DIM] * gates[:, 2 * CONV_DIM:3 * CONV_DIM]
    f = _dot(hb, win_ref[:, c0 + 3 * CONV_DIM:IN_DIM])
    xcs_ref[...] = _dot(f.astype(BF16), cdft_ref[...]).astype(BF16)


def _mixer_in(x, mods, norm_g, win, cdft, rope_tabs, caches, *, layer, mod_row_of_tile, tm, seq):
    n = x.shape[0]
    rotary = rope_tabs is not None
    in_specs = [
        pl.BlockSpec((tm, D_MODEL), lambda i: (i, 0)),
        pl.BlockSpec((None, None, N_MOD, D_MODEL), lambda i: (layer, mod_row_of_tile(i), 0, 0)),
        pl.BlockSpec((None, 6, D_MODEL), lambda i: (layer, 0, 0)),
        _resident((None, D_MODEL, IN_DIM), lambda i: (layer, 0, 0)),
        _resident((FOURIER_DIM, 2 * FOURIER_DIM), lambda i: (0, 0)),
    ]
    args = [x, mods, norm_g, win, cdft]
    if rotary:
        tiles_per_seq = rope_tabs[0].shape[0] // tm
        for tab in rope_tabs:
            in_specs.append(pl.BlockSpec((tm, LANES), lambda i: (i % tiles_per_seq, 0)))
            args.append(tab)

    def rows(width):
        return pl.BlockSpec((tm, width), lambda i: (i, 0))

    batch = n // seq
    if tm <= seq:
        tiles_per_seq = seq // tm
        kt_spec = pl.BlockSpec((None, ATT_DIM, tm), lambda i: (i // tiles_per_seq, 0, i % tiles_per_seq))
    else:
        kt_spec = pl.BlockSpec((tm // seq, ATT_DIM, seq), lambda i: (i, 0, 0))
    out_shape = [
        jax.ShapeDtypeStruct((n, ATT_DIM), BF16),
        jax.ShapeDtypeStruct((batch, ATT_DIM, seq), BF16),
        jax.ShapeDtypeStruct((n, ATT_DIM), BF16),
        jax.ShapeDtypeStruct((n, CONV_DIM), F32),
        jax.ShapeDtypeStruct((n, CONV_DIM), F32),
        jax.ShapeDtypeStruct((n, 2 * FOURIER_DIM), BF16),
    ]
    out_specs = [rows(ATT_DIM), kt_spec, rows(ATT_DIM), rows(CONV_DIM), rows(CONV_DIM),
                 rows(2 * FOURIER_DIM)]
    aliases = {}
    if not rotary:
        seqs = tm // seq
        out_shape += [jax.ShapeDtypeStruct((batch, DEPTH, ATT_DIM, seq), F32),
                      jax.ShapeDtypeStruct((batch, DEPTH, N_HEADS * seq, V_DIM), F32)]
        if caches is None:
            out_specs += [pl.BlockSpec((seqs, DEPTH, ATT_DIM, seq), lambda i: (i, 0, 0, 0)),
                          pl.BlockSpec((seqs, DEPTH, N_HEADS * seq, V_DIM), lambda i: (i, 0, 0, 0))]
        else:
            out_specs += [pl.BlockSpec((seqs, None, ATT_DIM, seq), lambda i: (i, layer, 0, 0)),
                          pl.BlockSpec((seqs, None, N_HEADS * seq, V_DIM), lambda i: (i, layer, 0, 0))]
            first_cache_out = len(out_shape) - 2
            for offset, arr in enumerate(caches):
                aliases[len(args)] = first_cache_out + offset
                in_specs.append(pl.BlockSpec(memory_space=pl.ANY))
                args.append(arr)

    cache_slot = layer if (not rotary and caches is None) else None
    return pl.pallas_call(
        functools.partial(_mixer_in_kernel, rotary=rotary, seq=seq, cache_slot=cache_slot),
        out_shape=out_shape,
        grid=(n // tm,),
        in_specs=in_specs,
        out_specs=out_specs,
        input_output_aliases=aliases,
        compiler_params=_params(("arbitrary",)),
        name="mixer_in",
    )(*args)


def _lambda(lam_ref, lambda_init):
    lq = lam_ref[...]
    s01 = jnp.sum(lq[0:1, :] * lq[1:2, :], axis=-1, keepdims=True)
    s23 = jnp.sum(lq[2:3, :] * lq[3:4, :], axis=-1, keepdims=True)
    return jnp.exp(s01) - jnp.exp(s23) + lambda_init


def _stage_values(vext_sc, v_ref, cv_ref, nb, n):
    for b in range(nb):
        for hd in range(N_HEADS):
            sl = slice(hd * V_DIM, (hd + 1) * V_DIM)
            vext_sc[b, hd, 0:n, 0:V_DIM] = v_ref[b, :, sl]
            if cv_ref is not None:
                past = vext_sc.shape[2] - n
                vext_sc[b, hd, n:, 0:V_DIM] = cv_ref[pl.ds(hd, past, stride=N_HEADS), :].astype(BF16)
            vext_sc[b, hd, :, V_DIM:] = jnp.ones((vext_sc.shape[2], V_DIM), BF16)


def _attend(q_ref, kt_sc, vext_sc, o_ref, lam, sub_g, nb):
    lane = lax.broadcasted_iota(jnp.int32, (1, V_DIM), 1)
    first_map = lane < QK_DIM

    def exponentials(b, hd):
        sl = slice(hd * V_DIM, (hd + 1) * V_DIM)
        q = q_ref[b, :, sl]
        zero = jnp.zeros_like(q)
        out = []
        for keep in (first_map, jnp.logical_not(first_map)):
            qm = jnp.where(keep, q, zero)
            s = _dot(qm, kt_sc[b, sl, :])
            m = jnp.max(s, axis=-1, keepdims=True)
            out.append(jnp.exp2((s - m).astype(BF16)))
        return out

    def finish(b, hd, e_maps):
        sl = slice(hd * V_DIM, (hd + 1) * V_DIM)
        vext = vext_sc[b, hd]
        r0, r1 = [_dot(e, vext) for e in e_maps]
        o = r0[:, :V_DIM] / r0[:, V_DIM:] - lam * (r1[:, :V_DIM] / r1[:, V_DIM:])
        o_ref[b, :, sl] = (_rms(o, sub_g)).astype(o_ref.dtype)

    items = [(b, hd) for b in range(nb) for hd in range(N_HEADS)]
    pending = exponentials(*items[0])
    for idx, item in enumerate(items):
        upcoming = exponentials(*items[idx + 1]) if idx + 1 < len(items) else None
        finish(*item, pending)
        pending = upcoming


def _mixer_out_kernel(*refs, seqs, tr, n, parts, has_ctx, lambda_init, dft_resident):
    if has_ctx:
        (x_ref, q_ref, kt_ref, v_ref, ckt_ref, cv_ref, lam_ref, sg_ref, gb_ref, u_ref, xcs_ref, dft_ref, cw_ref,
         mod_ref, g_ref, wout_ref, o_ref, att_ref, vext_sc, kt_sc) = refs
    else:
        (x_ref, q_ref, kt_ref, v_ref, lam_ref, sg_ref, gb_ref, u_ref, xcs_ref, dft_ref, cw_ref,
         mod_ref, g_ref, wout_ref, o_ref, att_ref, vext_sc) = refs
        kt_sc, cv_ref = kt_ref, None

    @pl.when(pl.program_id(1) == 0)
    def _():
        if has_ctx:
            for b in range(seqs):
                kt_sc[b, :, 0:n] = kt_ref[b]
                kt_sc[b, :, n:] = ckt_ref[...].astype(BF16)
        _stage_values(vext_sc, v_ref, cv_ref, seqs, n)

    _attend(q_ref, kt_sc, vext_sc, att_ref, _lambda(lam_ref, lambda_init), sg_ref[...] * (1.0 - lambda_init), seqs)

    sub = tr // parts
    gate_g = g_ref[3:4, :] * mod_ref[5:6, :]
    cw = cw_ref[...]
    row = lax.broadcasted_iota(jnp.int32, (sub, 1), 0)
    four_scale = (n * FOURIER_GROUP_DIM) ** -0.5

    def mix(s, t):
        rows = slice(t * sub, (t + 1) * sub)
        r0 = pl.multiple_of(pl.program_id(1) * tr + t * sub, sub)
        dft_rows = pl.ds(r0, sub) if dft_resident else rows
        four = (_dot(dft_ref[dft_rows, 0:n], xcs_ref[s, :, 0:FOURIER_DIM])
                + _dot(dft_ref[dft_rows, n:2 * n], xcs_ref[s, :, FOURIER_DIM:2 * FOURIER_DIM])) * four_scale
        u = u_ref[s, pl.ds(r0, sub), :]
        before_start = pl.multiple_of(jnp.maximum(r0 - SUBLANES, 0), SUBLANES)
        after_start = pl.multiple_of(jnp.minimum(r0 + sub, n - SUBLANES), SUBLANES)
        before = u_ref[s, pl.ds(before_start, SUBLANES), :][SUBLANES - 1:SUBLANES, :]
        after = u_ref[s, pl.ds(after_start, SUBLANES), :][0:1, :]
        before = jnp.where(r0 > 0, before, 0.0)
        after = jnp.where(r0 + sub < n, after, 0.0)
        prev = jnp.where(row == 0, before, pltpu.roll(u, 1, 0))
        nxt = jnp.where(row == sub - 1, after, pltpu.roll(u, sub - 1, 0))
        conv = gb_ref[s, rows, :] * (prev * cw[0:1, :] + u * cw[1:2, :] + nxt * cw[2:3, :])
        return conv.astype(BF16), four.astype(BF16)

    def project(s, t, conv, four):
        rows = slice(t * sub, (t + 1) * sub)
        y = (_dot(att_ref[s, rows, :], wout_ref[0:ATT_DIM, :])
             + _dot(conv, wout_ref[ATT_DIM:ATT_DIM + CONV_DIM, :])
             + _dot(four, wout_ref[ATT_DIM + CONV_DIM:D_MODEL, :]))
        o_ref[s, rows, :] = x_ref[s, rows, :] + _rms(y, gate_g)

    items = [(s, t) for s in range(seqs) for t in range(parts)]
    pending = mix(*items[0])
    for idx, item in enumerate(items):
        upcoming = mix(*items[idx + 1]) if idx + 1 < len(items) else None
        project(*item, *pending)
        pending = upcoming


def _mixer_out(x, q, kt, v, ctx_kt, ctx_v, lam_qk, subln_g, gb, u, xcs, dft, conv_w, mods, norm_g, wout, *,
               layer, mod_row_of_batch, seqs, tr, parts):
    b, n, _ = x.shape
    has_ctx = ctx_kt is not None
    assert not has_ctx or seqs == 1
    dft_resident = tr == n

    def tile(width):
        return pl.BlockSpec((seqs, tr, width), lambda i, j: (i, j, 0))

    def whole(width):
        return pl.BlockSpec((seqs, n, width), lambda i, j: (i, 0, 0))

    in_specs = [tile(D_MODEL), tile(ATT_DIM), pl.BlockSpec((seqs, ATT_DIM, n), lambda i, j: (i, 0, 0)),
                whole(ATT_DIM)]
    args = [x, q, kt, v]
    if has_ctx:
        for arr in (ctx_kt, ctx_v):
            in_specs.append(pl.BlockSpec((None, None) + arr.shape[2:], lambda i, j: (i, layer, 0, 0)))
            args.append(arr)
    in_specs += [
        pl.BlockSpec((None, 4, QK_DIM), lambda i, j: (layer, 0, 0)),
        pl.BlockSpec((None, 1, V_DIM), lambda i, j: (layer, 0, 0)),
        tile(CONV_DIM), whole(CONV_DIM), whole(2 * FOURIER_DIM),
        _resident((n, 2 * n), lambda i, j: (0, 0)) if dft_resident
        else pl.BlockSpec((tr, 2 * n), lambda i, j: (j, 0)),
        pl.BlockSpec((None, 3, CONV_DIM), lambda i, j: (layer, 0, 0)),
        pl.BlockSpec((None, None, N_MOD, D_MODEL), lambda i, j: (layer, mod_row_of_batch(i), 0, 0)),
        pl.BlockSpec((None, 6, D_MODEL), lambda i, j: (layer, 0, 0)),
        _resident((None, D_MODEL, D_MODEL), lambda i, j: (layer, 0, 0)),
    ]
    args += [lam_qk, subln_g.reshape(DEPTH, 1, V_DIM), gb, u, xcs, dft, conv_w, mods, norm_g, wout]
    n_keys = n + (ctx_kt.shape[3] if has_ctx else 0)
    scratch = [pltpu.VMEM((seqs, tr, ATT_DIM), BF16), pltpu.VMEM((seqs, N_HEADS, n_keys, 2 * V_DIM), BF16)]
    if has_ctx:
        scratch.append(pltpu.VMEM((seqs, ATT_DIM, n_keys), BF16))
    return pl.pallas_call(
        functools.partial(_mixer_out_kernel, seqs=seqs, tr=tr, n=n, parts=parts, has_ctx=has_ctx,
                          lambda_init=_lambda_init(layer), dft_resident=dft_resident),
        out_shape=jax.ShapeDtypeStruct((b, n, D_MODEL), F32),
        grid=(b // seqs, n // tr),
        in_specs=in_specs,
        out_specs=tile(D_MODEL),
        scratch_shapes=scratch,
        compiler_params=_params(("arbitrary", "arbitrary")),
        name="mixer_out",
    )(*args)


def _position_dft(n):
    jk = (np.arange(n, dtype=np.int64)[:, None] * np.arange(n, dtype=np.int64)[None, :]) % n
    ang = 2.0 * np.pi * jk.astype(np.float64) / n
    return np.concatenate([np.cos(ang), -np.sin(ang)], axis=1).astype(np.float32)


def _channel_dft():
    g = FOURIER_GROUP_DIM
    jk = (np.arange(g)[:, None] * np.arange(g)[None, :]) % g
    ang = 2.0 * np.pi * jk.astype(np.float64) / g
    eye = np.eye(FOURIER_DIM // g)
    return np.concatenate([np.kron(eye, np.cos(ang)), np.kron(eye, np.sin(ang))], axis=1).astype(np.float32)


def _rope_tables(n):
    t = jnp.arange(n)
    row = (t // GRID_W).astype(F32)
    col = (t % GRID_W).astype(F32)
    inv = 1.0 / (ROPE_THETA ** (jnp.arange(ROT_FREQS, dtype=F32) / ROT_FREQS))
    ang_r = row[:, None] * inv
    ang_c = col[:, None] * inv
    cos = jnp.concatenate([jnp.cos(ang_r)] * 2 + [jnp.cos(ang_c)] * 2, axis=-1)
    sin = jnp.concatenate([-jnp.sin(ang_r), jnp.sin(ang_r), -jnp.sin(ang_c), jnp.sin(ang_c)], axis=-1)
    return jnp.tile(cos, (1, 2)), jnp.tile(sin, (1, 2))


class _Tiling(NamedTuple):
    token_rows: int
    seqs: int
    seq_rows: int
    parts: int


def _tiling(seq_len):
    if seq_len <= MIXER_ROWS:
        return _Tiling(TOKEN_ROWS, TOKEN_ROWS // seq_len, seq_len, 1)
    return _Tiling(TOKEN_ROWS, 1, MIXER_ROWS, 1)


def _run_path(x, mods, mod_row, weights, ctx, rope_tabs, dft):
    norm_g, wgu, wd, win, wout, cdft, conv_w, lam_qk, subln_g = weights
    b, n, _ = x.shape
    tm, seqs, tr, parts = _tiling(n)
    tiles_per_seq = max(n // tm, 1)
    seqs_per_tile = max(tm // n, 1)

    def mod_row_of_tile(i):
        return mod_row(i * seqs_per_tile // tiles_per_seq)

    def shape3(a):
        return a.reshape(b, n, a.shape[-1])

    ctx_k, ctx_v = ctx if ctx is not None else (None, None)
    caches = None
    for l in range(DEPTH):
        flat = x.reshape(b * n, D_MODEL)
        flat = _ffn(flat, mods, norm_g, wgu, wd, layer=l, half=0, mod_row_of_tile=mod_row_of_tile, tm=tm)
        q, kt, v, gb, u, xcs, *caches = _mixer_in(flat, mods, norm_g, win, cdft, rope_tabs, caches, layer=l,
                                                  mod_row_of_tile=mod_row_of_tile, tm=tm, seq=n)
        caches = caches or None
        x = _mixer_out(shape3(flat), shape3(q), kt, shape3(v), ctx_k, ctx_v, lam_qk, subln_g, shape3(gb),
                       shape3(u), shape3(xcs), dft, conv_w, mods, norm_g, wout, layer=l,
                       mod_row_of_batch=lambda i: mod_row(i * seqs), seqs=seqs, tr=tr, parts=parts)
        flat = _ffn(x.reshape(b * n, D_MODEL), mods, norm_g, wgu, wd, layer=l, half=1,
                    mod_row_of_tile=mod_row_of_tile, tm=tm)
        x = flat.reshape(b, n, D_MODEL)
    return x, caches


def kernel(x_prompt, x_sample, cache_k, cache_v, c, c_ctx, w_mod, b_mod, norm_g, w_ffn_gu, w_ffn_down,
           w_in, w_out, conv_w, lam_qk, subln_g):
    batch, seq, _ = x_prompt.shape
    dec_batch, dec_seq, _ = x_sample.shape
    past = cache_k.shape[2]

    cvecs = jnp.zeros((MOD_ROWS, D_MODEL), F32).at[0].set(c_ctx).at[1:1 + dec_batch].set(c)
    mods = _modulation(cvecs, w_mod, b_mod)

    wgu, wd = w_ffn_gu, w_ffn_down
    win = _cast_bf16(w_in.reshape(DEPTH * D_MODEL, IN_DIM), D_MODEL // 2).reshape(DEPTH, D_MODEL, IN_DIM)
    wout = _cast_bf16(w_out.reshape(DEPTH * D_MODEL, D_MODEL), D_MODEL).reshape(DEPTH, D_MODEL, D_MODEL)
    cdft = jnp.asarray(_channel_dft()).astype(BF16)
    weights = (norm_g, wgu, wd, win, wout, cdft, conv_w, lam_qk, subln_g)

    dft_ctx = jnp.asarray(_position_dft(seq)).astype(BF16)
    y_prompt, (kc, vc) = _run_path(x_prompt, mods, lambda i: 0, weights, None, None, dft_ctx)
    new_cache_k = kc.reshape(batch, DEPTH, N_HEADS, 2, QK_DIM, seq).transpose(0, 1, 5, 2, 3, 4)
    new_cache_v = vc.reshape(batch, DEPTH, seq, N_HEADS, V_DIM)

    dft_lat = jnp.asarray(_position_dft(dec_seq)).astype(BF16)
    ctx = (cache_k.transpose(0, 1, 3, 4, 5, 2).reshape(dec_batch, DEPTH, ATT_DIM, past),
           cache_v.reshape(dec_batch, DEPTH, past * N_HEADS, V_DIM))
    y_sample, _ = _run_path(x_sample, mods, lambda i: 1 + i, weights, ctx, _rope_tables(dec_seq), dft_lat)
    return (y_prompt, y_sample, new_cache_k, new_cache_v)
```

```python
import functools
import math
from typing import NamedTuple

import jax
import jax.numpy as jnp
import numpy as np
from jax import lax
from jax.experimental import pallas as pl
from jax.experimental.pallas import tpu as pltpu

D_MODEL = 1024
DEPTH = 2
GRID_W = 64
V_DIM = 128
QK_DIM = V_DIM // 2
N_HEADS = 4
ATT_DIM = N_HEADS * V_DIM
CONV_DIM = 256
FOURIER_DIM = 256
FOURIER_GROUP_DIM = 64
IN_DIM = 3 * ATT_DIM + 3 * CONV_DIM + FOURIER_DIM
D_FF = 2816
N_MOD = 9
ROT_FREQS = 16
ROPE_THETA = 10000.0
EPS = 1e-6
FFN_RESIDUAL_WEIGHT = 0.5
LOG2_E = math.log2(math.e)

FF_CHUNK = 256
N_FF_CHUNKS = D_FF // FF_CHUNK
FF_GROUP = 3
assert (N_FF_CHUNKS - 2) % FF_GROUP == 0
TOKEN_ROWS = 1024
MIXER_ROWS = 256
MOD_ROWS = 8
MOD_COLS = N_MOD * D_MODEL // 4
SUBLANES = 8
LANES = 128
VMEM_LIMIT = 56 * 1024 * 1024

F32 = jnp.float32
BF16 = jnp.bfloat16


def _lambda_init(layer_idx):
    return 0.8 - 0.6 * math.exp(-0.3 * layer_idx)


def _rms(x, g):
    return x * lax.rsqrt(jnp.mean(x * x, axis=-1, keepdims=True) + EPS) * g


def _dot(a, b):
    return jnp.dot(a, b, preferred_element_type=F32)


def _params(semantics):
    return pltpu.CompilerParams(dimension_semantics=semantics, vmem_limit_bytes=VMEM_LIMIT)


def _resident(block_shape, index_map):
    return pl.BlockSpec(block_shape, index_map, pipeline_mode=pl.Buffered(1))


def _cast_kernel(x_ref, o_ref):
    o_ref[...] = x_ref[...].astype(o_ref.dtype)


def _cast_bf16(x, block_rows):
    rows, cols = x.shape
    return pl.pallas_call(
        _cast_kernel,
        out_shape=jax.ShapeDtypeStruct((rows, cols), BF16),
        grid=(rows // block_rows,),
        in_specs=[pl.BlockSpec((block_rows, cols), lambda i: (i, 0))],
        out_specs=pl.BlockSpec((block_rows, cols), lambda i: (i, 0)),
        compiler_params=_params(("arbitrary",)),
        name="cast_bf16",
    )(x)


def _mod_kernel(c_ref, w_ref, b_ref, o_ref):
    c = c_ref[...]
    s = (c / (1.0 + jnp.exp(-c))).astype(BF16)
    o_ref[...] = _dot(s, w_ref[...].astype(BF16)) + b_ref[...]


def _modulation(cvecs, w_mod, b_mod):
    col = MOD_COLS
    out = pl.pallas_call(
        _mod_kernel,
        out_shape=jax.ShapeDtypeStruct((DEPTH, MOD_ROWS, N_MOD * D_MODEL), F32),
        grid=(DEPTH, N_MOD * D_MODEL // col),
        in_specs=[
            pl.BlockSpec((MOD_ROWS, D_MODEL), lambda l, j: (0, 0)),
            pl.BlockSpec((None, D_MODEL, col), lambda l, j: (l, 0, j)),
            pl.BlockSpec((None, 1, col), lambda l, j: (l, 0, j)),
        ],
        out_specs=pl.BlockSpec((None, MOD_ROWS, col), lambda l, j: (l, 0, j)),
        compiler_params=_params(("arbitrary", "arbitrary")),
        name="modulation",
    )(cvecs, w_mod, b_mod.reshape(DEPTH, 1, N_MOD * D_MODEL))
    return out.reshape(DEPTH, MOD_ROWS, N_MOD, D_MODEL)


def _ffn_kernel(x_ref, mod_ref, g_ref, wg_ref, wu_ref, wdn_ref, o_ref, acc_ref, hb_ref, wgu_sc, wd_sc,
                *, sub, g_row):
    step = pl.program_id(0)
    last = N_FF_CHUNKS - 1
    shift = mod_ref[3 * sub:3 * sub + 1, :]
    scale = mod_ref[3 * sub + 1:3 * sub + 2, :]
    gate = mod_ref[3 * sub + 2:3 * sub + 3, :]

    def pre_norm():
        h = _rms(x_ref[...], g_ref[g_row:g_row + 1, :] * (1.0 + scale)) + shift
        return h.astype(BF16)

    def hidden(hb, c):
        gu = _dot(hb, wgu_sc[c])
        g = gu[:, :FF_CHUNK]
        u = gu[:, FF_CHUNK:]
        return (g / (1.0 + jnp.exp(-g)) * u).astype(BF16)

    def finish(y):
        o_ref[...] = x_ref[...] + _rms(y, g_ref[g_row + 1:g_row + 2, :] * (FFN_RESIDUAL_WEIGHT * gate))

    @pl.when(step <= last)
    def _():
        @pl.when(step == 0)
        def _():
            hb_ref[...] = pre_norm()
            acc_ref[...] = jnp.zeros_like(acc_ref)

        wgu_sc[step, :, 0:FF_CHUNK] = wg_ref[...].astype(BF16)
        wgu_sc[step, :, FF_CHUNK:2 * FF_CHUNK] = wu_ref[...].astype(BF16)
        wd_sc[step] = wdn_ref[...].astype(BF16)
        acc_ref[...] += _dot(hidden(hb_ref[...], step), wd_sc[step])

        @pl.when(step == last)
        def _():
            finish(acc_ref[...])

    @pl.when(step > last)
    def _():
        hb = pre_norm()
        acc_ref[...] = _dot(hidden(hb, 0), wd_sc[0])

        def group(i, carry):
            c0 = 1 + i * FF_GROUP
            a = jnp.concatenate([hidden(hb, c0 + k) for k in range(FF_GROUP)], axis=-1)
            wd = wd_sc[pl.ds(c0, FF_GROUP)].reshape(FF_GROUP * FF_CHUNK, D_MODEL)
            acc_ref[...] += _dot(a, wd)
            return carry

        lax.fori_loop(0, (N_FF_CHUNKS - 2) // FF_GROUP, group, 0)
        finish(acc_ref[...] + _dot(hidden(hb, last), wd_sc[last]))


def _ffn(x, mods, norm_g, w_gu, w_down, *, layer, half, mod_row_of_tile, tm):
    n = x.shape[0]
    sub = 2 * half
    g_row = 4 * half
    last = N_FF_CHUNKS - 1

    def tile(s):
        return jnp.maximum(s - last, 0)

    def chunk(s):
        return jnp.minimum(s, last)

    return pl.pallas_call(
        functools.partial(_ffn_kernel, sub=sub, g_row=g_row),
        out_shape=jax.ShapeDtypeStruct((n, D_MODEL), F32),
        grid=(n // tm + last,),
        in_specs=[
            pl.BlockSpec((tm, D_MODEL), lambda s: (tile(s), 0)),
            pl.BlockSpec((None, None, N_MOD, D_MODEL), lambda s: (layer, mod_row_of_tile(tile(s)), 0, 0)),
            pl.BlockSpec((None, 6, D_MODEL), lambda s: (layer, 0, 0)),
            pl.BlockSpec((None, None, D_MODEL, FF_CHUNK), lambda s: (layer, half, 0, chunk(s))),
            pl.BlockSpec((None, None, D_MODEL, FF_CHUNK), lambda s: (layer, half, 0, N_FF_CHUNKS + chunk(s))),
            pl.BlockSpec((None, None, FF_CHUNK, D_MODEL), lambda s: (layer, half, chunk(s), 0)),
        ],
        out_specs=pl.BlockSpec((tm, D_MODEL), lambda s: (tile(s), 0)),
        scratch_shapes=[
            pltpu.VMEM((tm, D_MODEL), F32),
            pltpu.VMEM((tm, D_MODEL), BF16),
            pltpu.VMEM((N_FF_CHUNKS, D_MODEL, 2 * FF_CHUNK), BF16),
            pltpu.VMEM((N_FF_CHUNKS, FF_CHUNK, D_MODEL), BF16),
        ],
        compiler_params=_params(("arbitrary",)),
        name="ffn",
    )(x, mods, norm_g, w_gu, w_gu, w_down)


def _rope(x, cos, sin_signed, lo_half):
    partner = jnp.where(lo_half, pltpu.roll(x, LANES - ROT_FREQS, 1), pltpu.roll(x, ROT_FREQS, 1))
    return x * cos + partner * sin_signed


def _mixer_in_kernel(*refs, rotary, seq, cache_slot):
    x_ref, mod_ref, g_ref, win_ref, cdft_ref = refs[:5]
    if rotary:
        cos_ref, sin_ref = refs[5:7]
        q_ref, kt_ref, v_ref, gb_ref, u_ref, xcs_ref = refs[7:]
    else:
        q_ref, kt_ref, v_ref, gb_ref, u_ref, xcs_ref, kc_ref, vc_ref = refs[-8:]
    x = x_ref[...]
    shift = mod_ref[3:4, :]
    scale = mod_ref[4:5, :]
    h = _rms(x, g_ref[2:3, :]) * (1.0 + scale) + shift
    hb = h.astype(BF16)

    q = _dot(hb, win_ref[:, 0:ATT_DIM]) * (QK_DIM ** -0.5 * LOG2_E)
    k = _dot(hb, win_ref[:, ATT_DIM:2 * ATT_DIM])
    v = _dot(hb, win_ref[:, 2 * ATT_DIM:3 * ATT_DIM])
    v_ref[...] = v.astype(v_ref.dtype)
    if rotary:
        cos = cos_ref[...]
        sin = sin_ref[...]
        lane = lax.broadcasted_iota(jnp.int32, (1, LANES), 1)
        lo_half = (lane % (2 * ROT_FREQS)) < ROT_FREQS
        for hd in range(N_HEADS):
            sl = slice(hd * V_DIM, (hd + 1) * V_DIM)
            q_ref[:, sl] = _rope(q[:, sl], cos, sin, lo_half).astype(q_ref.dtype)
            kt_ref[sl, :] = _rope(k[:, sl], cos, sin, lo_half).T.astype(kt_ref.dtype)
    else:
        q_ref[...] = q.astype(q_ref.dtype)
        for s in range(x.shape[0] // seq):
            rows = slice(s * seq, (s + 1) * seq)
            kt = k[rows, :].T
            kt_ref[s] = kt.astype(kt_ref.dtype)
            if cache_slot is None:
                kc_out, vc_out = kc_ref.at[s], vc_ref.at[s]
            else:
                kc_out, vc_out = kc_ref.at[s, cache_slot], vc_ref.at[s, cache_slot]
                for other in range(DEPTH):
                    if other != cache_slot:
                        kc_ref[s, other] = jnp.zeros(kc_ref.shape[2:], F32)
                        vc_ref[s, other] = jnp.zeros(vc_ref.shape[2:], F32)
            kc_out[...] = kt
            for hd in range(N_HEADS):
                vc_out[pl.ds(hd, seq, stride=N_HEADS), :] = v[rows, hd * V_DIM:(hd + 1) * V_DIM]

    c0 = 3 * ATT_DIM
    gates = _dot(hb, win_ref[:, c0:c0 + 3 * CONV_DIM])
    gb_ref[...] = gates[:, 0:CONV_DIM]
    u_ref[...] = gates[:, CONV_DIM:2 * CONV_DIM] * gates[:, 2 * CONV_DIM:3 * CONV_DIM]
    f = _dot(hb, win_ref[:, c0 + 3 * CONV_DIM:IN_DIM])
    xcs_ref[...] = _dot(f.astype(BF16), cdft_ref[...]).astype(BF16)


def _mixer_in(x, mods, norm_g, win, cdft, rope_tabs, caches, *, layer, mod_row_of_tile, tm, seq):
    n = x.shape[0]
    rotary = rope_tabs is not None
    in_specs = [
        pl.BlockSpec((tm, D_MODEL), lambda i: (i, 0)),
        pl.BlockSpec((None, None, N_MOD, D_MODEL), lambda i: (layer, mod_row_of_tile(i), 0, 0)),
        pl.BlockSpec((None, 6, D_MODEL), lambda i: (layer, 0, 0)),
        _resident((None, D_MODEL, IN_DIM), lambda i: (layer, 0, 0)),
        _resident((FOURIER_DIM, 2 * FOURIER_DIM), lambda i: (0, 0)),
    ]
    args = [x, mods, norm_g, win, cdft]
    if rotary:
        tiles_per_seq = rope_tabs[0].shape[0] // tm
        for tab in rope_tabs:
            in_specs.append(pl.BlockSpec((tm, LANES), lambda i: (i % tiles_per_seq, 0)))
            args.append(tab)

    def rows(width):
        return pl.BlockSpec((tm, width), lambda i: (i, 0))

    batch = n // seq
    if tm <= seq:
        tiles_per_seq = seq // tm
        kt_spec = pl.BlockSpec((None, ATT_DIM, tm), lambda i: (i // tiles_per_seq, 0, i % tiles_per_seq))
    else:
        kt_spec = pl.BlockSpec((tm // seq, ATT_DIM, seq), lambda i: (i, 0, 0))
    out_shape = [
        jax.ShapeDtypeStruct((n, ATT_DIM), BF16),
        jax.ShapeDtypeStruct((batch, ATT_DIM, seq), BF16),
        jax.ShapeDtypeStruct((n, ATT_DIM), BF16),
        jax.ShapeDtypeStruct((n, CONV_DIM), F32),
        jax.ShapeDtypeStruct((n, CONV_DIM), F32),
        jax.ShapeDtypeStruct((n, 2 * FOURIER_DIM), BF16),
    ]
    out_specs = [rows(ATT_DIM), kt_spec, rows(ATT_DIM), rows(CONV_DIM), rows(CONV_DIM),
                 rows(2 * FOURIER_DIM)]
    aliases = {}
    if not rotary:
        seqs = tm // seq
        out_shape += [jax.ShapeDtypeStruct((batch, DEPTH, ATT_DIM, seq), F32),
                      jax.ShapeDtypeStruct((batch, DEPTH, N_HEADS * seq, V_DIM), F32)]
        if caches is None:
            out_specs += [pl.BlockSpec((seqs, DEPTH, ATT_DIM, seq), lambda i: (i, 0, 0, 0)),
                          pl.BlockSpec((seqs, DEPTH, N_HEADS * seq, V_DIM), lambda i: (i, 0, 0, 0))]
        else:
            out_specs += [pl.BlockSpec((seqs, None, ATT_DIM, seq), lambda i: (i, layer, 0, 0)),
                          pl.BlockSpec((seqs, None, N_HEADS * seq, V_DIM), lambda i: (i, layer, 0, 0))]
            first_cache_out = len(out_shape) - 2
            for offset, arr in enumerate(caches):
                aliases[len(args)] = first_cache_out + offset
                in_specs.append(pl.BlockSpec(memory_space=pl.ANY))
                args.append(arr)

    cache_slot = layer if (not rotary and caches is None) else None
    return pl.pallas_call(
        functools.partial(_mixer_in_kernel, rotary=rotary, seq=seq, cache_slot=cache_slot),
        out_shape=out_shape,
        grid=(n // tm,),
        in_specs=in_specs,
        out_specs=out_specs,
        input_output_aliases=aliases,
        compiler_params=_params(("arbitrary",)),
        name="mixer_in",
    )(*args)


def _lambda(lam_ref, lambda_init):
    lq = lam_ref[...]
    s01 = jnp.sum(lq[0:1, :] * lq[1:2, :], axis=-1, keepdims=True)
    s23 = jnp.sum(lq[2:3, :] * lq[3:4, :], axis=-1, keepdims=True)
    return jnp.exp(s01) - jnp.exp(s23) + lambda_init


def _stage_values(vext_sc, v_ref, cv_ref, nb, n):
    for b in range(nb):
        for hd in range(N_HEADS):
            sl = slice(hd * V_DIM, (hd + 1) * V_DIM)
            vext_sc[b, hd, 0:n, 0:V_DIM] = v_ref[b, :, sl]
            if cv_ref is not None:
                past = vext_sc.shape[2] - n
                vext_sc[b, hd, n:, 0:V_DIM] = cv_ref[pl.ds(hd, past, stride=N_HEADS), :].astype(BF16)
            vext_sc[b, hd, :, V_DIM:] = jnp.ones((vext_sc.shape[2], V_DIM), BF16)


def _attend(q_ref, kt_sc, vext_sc, o_ref, lam, sub_g, nb):
    lane = lax.broadcasted_iota(jnp.int32, (1, V_DIM), 1)
    first_map = lane < QK_DIM

    def exponentials(b, hd):
        sl = slice(hd * V_DIM, (hd + 1) * V_DIM)
        q = q_ref[b, :, sl]
        zero = jnp.zeros_like(q)
        out = []
        for keep in (first_map, jnp.logical_not(first_map)):
            qm = jnp.where(keep, q, zero)
            s = _dot(qm, kt_sc[b, sl, :])
            m = jnp.max(s, axis=-1, keepdims=True)
            out.append(jnp.exp2((s - m).astype(BF16)))
        return out

    def finish(b, hd, e_maps):
        sl = slice(hd * V_DIM, (hd + 1) * V_DIM)
        vext = vext_sc[b, hd]
        r0, r1 = [_dot(e, vext) for e in e_maps]
        o = r0[:, :V_DIM] / r0[:, V_DIM:] - lam * (r1[:, :V_DIM] / r1[:, V_DIM:])
        o_ref[b, :, sl] = (_rms(o, sub_g)).astype(o_ref.dtype)

    items = [(b, hd) for b in range(nb) for hd in range(N_HEADS)]
    pending = exponentials(*items[0])
    for idx, item in enumerate(items):
        upcoming = exponentials(*items[idx + 1]) if idx + 1 < len(items) else None
        finish(*item, pending)
        pending = upcoming


def _mixer_out_kernel(*refs, seqs, tr, n, parts, has_ctx, lambda_init, dft_resident):
    if has_ctx:
        (x_ref, q_ref, kt_ref, v_ref, ckt_ref, cv_ref, lam_ref, sg_ref, gb_ref, u_ref, xcs_ref, dft_ref, cw_ref,
         mod_ref, g_ref, wout_ref, o_ref, att_ref, vext_sc, kt_sc) = refs
    else:
        (x_ref, q_ref, kt_ref, v_ref, lam_ref, sg_ref, gb_ref, u_ref, xcs_ref, dft_ref, cw_ref,
         mod_ref, g_ref, wout_ref, o_ref, att_ref, vext_sc) = refs
        kt_sc, cv_ref = kt_ref, None

    @pl.when(pl.program_id(1) == 0)
    def _():
        if has_ctx:
            for b in range(seqs):
                kt_sc[b, :, 0:n] = kt_ref[b]
                kt_sc[b, :, n:] = ckt_ref[...].astype(BF16)
        _stage_values(vext_sc, v_ref, cv_ref, seqs, n)

    sub = tr // parts
    gate_g = g_ref[3:4, :] * mod_ref[5:6, :]
    cw = cw_ref[...]
    row = lax.broadcasted_iota(jnp.int32, (sub, 1), 0)
    four_scale = (n * FOURIER_GROUP_DIM) ** -0.5

    def mix(s, t):
        rows = slice(t * sub, (t + 1) * sub)
        r0 = pl.multiple_of(pl.program_id(1) * tr + t * sub, sub)
        dft_rows = pl.ds(r0, sub) if dft_resident else rows
        four = (_dot(dft_ref[dft_rows, 0:n], xcs_ref[s, :, 0:FOURIER_DIM])
                + _dot(dft_ref[dft_rows, n:2 * n], xcs_ref[s, :, FOURIER_DIM:2 * FOURIER_DIM])) * four_scale
        u = u_ref[s, pl.ds(r0, sub), :]
        before_start = pl.multiple_of(jnp.maximum(r0 - SUBLANES, 0), SUBLANES)
        after_start = pl.multiple_of(jnp.minimum(r0 + sub, n - SUBLANES), SUBLANES)
        before = u_ref[s, pl.ds(before_start, SUBLANES), :][SUBLANES - 1:SUBLANES, :]
        after = u_ref[s, pl.ds(after_start, SUBLANES), :][0:1, :]
        before = jnp.where(r0 > 0, before, 0.0)
        after = jnp.where(r0 + sub < n, after, 0.0)
        prev = jnp.where(row == 0, before, pltpu.roll(u, 1, 0))
        nxt = jnp.where(row == sub - 1, after, pltpu.roll(u, sub - 1, 0))
        conv = gb_ref[s, rows, :] * (prev * cw[0:1, :] + u * cw[1:2, :] + nxt * cw[2:3, :])
        return conv.astype(BF16), four.astype(BF16)

    def project(s, t, conv, four):
        rows = slice(t * sub, (t + 1) * sub)
        y = (_dot(att_ref[s, rows, :], wout_ref[0:ATT_DIM, :])
             + _dot(conv, wout_ref[ATT_DIM:ATT_DIM + CONV_DIM, :])
             + _dot(four, wout_ref[ATT_DIM + CONV_DIM:D_MODEL, :]))
        o_ref[s, rows, :] = x_ref[s, rows, :] + _rms(y, gate_g)

    items = [(s, t) for s in range(seqs) for t in range(parts)]
    attend = functools.partial(_attend, q_ref, kt_sc, vext_sc, att_ref, _lambda(lam_ref, lambda_init),
                               sg_ref[...] * (1.0 - lambda_init), seqs)
    if len(items) == 1:
        pending = mix(*items[0])
        attend()
    else:
        attend()
        pending = mix(*items[0])
    for idx, item in enumerate(items):
        upcoming = mix(*items[idx + 1]) if idx + 1 < len(items) else None
        project(*item, *pending)
        pending = upcoming


def _mixer_out(x, q, kt, v, ctx_kt, ctx_v, lam_qk, subln_g, gb, u, xcs, dft, conv_w, mods, norm_g, wout, *,
               layer, mod_row_of_batch, seqs, tr, parts):
    b, n, _ = x.shape
    has_ctx = ctx_kt is not None
    assert not has_ctx or seqs == 1
    dft_resident = tr == n

    def tile(width):
        return pl.BlockSpec((seqs, tr, width), lambda i, j: (i, j, 0))

    def whole(width):
        return pl.BlockSpec((seqs, n, width), lambda i, j: (i, 0, 0))

    in_specs = [tile(D_MODEL), tile(ATT_DIM), pl.BlockSpec((seqs, ATT_DIM, n), lambda i, j: (i, 0, 0)),
                whole(ATT_DIM)]
    args = [x, q, kt, v]
    if has_ctx:
        for arr in (ctx_kt, ctx_v):
            in_specs.append(pl.BlockSpec((None, None) + arr.shape[2:], lambda i, j: (i, layer, 0, 0)))
            args.append(arr)
    in_specs += [
        pl.BlockSpec((None, 4, QK_DIM), lambda i, j: (layer, 0, 0)),
        pl.BlockSpec((None, 1, V_DIM), lambda i, j: (layer, 0, 0)),
        tile(CONV_DIM), whole(CONV_DIM), whole(2 * FOURIER_DIM),
        _resident((n, 2 * n), lambda i, j: (0, 0)) if dft_resident
        else pl.BlockSpec((tr, 2 * n), lambda i, j: (j, 0)),
        pl.BlockSpec((None, 3, CONV_DIM), lambda i, j: (layer, 0, 0)),
        pl.BlockSpec((None, None, N_MOD, D_MODEL), lambda i, j: (layer, mod_row_of_batch(i), 0, 0)),
        pl.BlockSpec((None, 6, D_MODEL), lambda i, j: (layer, 0, 0)),
        _resident((None, D_MODEL, D_MODEL), lambda i, j: (layer, 0, 0)),
    ]
    args += [lam_qk, subln_g.reshape(DEPTH, 1, V_DIM), gb, u, xcs, dft, conv_w, mods, norm_g, wout]
    n_keys = n + (ctx_kt.shape[3] if has_ctx else 0)
    scratch = [pltpu.VMEM((seqs, tr, ATT_DIM), BF16), pltpu.VMEM((seqs, N_HEADS, n_keys, 2 * V_DIM), BF16)]
    if has_ctx:
        scratch.append(pltpu.VMEM((seqs, ATT_DIM, n_keys), BF16))
    return pl.pallas_call(
        functools.partial(_mixer_out_kernel, seqs=seqs, tr=tr, n=n, parts=parts, has_ctx=has_ctx,
                          lambda_init=_lambda_init(layer), dft_resident=dft_resident),
        out_shape=jax.ShapeDtypeStruct((b, n, D_MODEL), F32),
        grid=(b // seqs, n // tr),
        in_specs=in_specs,
        out_specs=tile(D_MODEL),
        scratch_shapes=scratch,
        compiler_params=_params(("arbitrary", "arbitrary")),
        name="mixer_out",
    )(*args)


def _position_dft(n):
    jk = (np.arange(n, dtype=np.int64)[:, None] * np.arange(n, dtype=np.int64)[None, :]) % n
    ang = 2.0 * np.pi * jk.astype(np.float64) / n
    return np.concatenate([np.cos(ang), -np.sin(ang)], axis=1).astype(np.float32)


def _channel_dft():
    g = FOURIER_GROUP_DIM
    jk = (np.arange(g)[:, None] * np.arange(g)[None, :]) % g
    ang = 2.0 * np.pi * jk.astype(np.float64) / g
    eye = np.eye(FOURIER_DIM // g)
    return np.concatenate([np.kron(eye, np.cos(ang)), np.kron(eye, np.sin(ang))], axis=1).astype(np.float32)


def _rope_tables(n):
    t = jnp.arange(n)
    row = (t // GRID_W).astype(F32)
    col = (t % GRID_W).astype(F32)
    inv = 1.0 / (ROPE_THETA ** (jnp.arange(ROT_FREQS, dtype=F32) / ROT_FREQS))
    ang_r = row[:, None] * inv
    ang_c = col[:, None] * inv
    cos = jnp.concatenate([jnp.cos(ang_r)] * 2 + [jnp.cos(ang_c)] * 2, axis=-1)
    sin = jnp.concatenate([-jnp.sin(ang_r), jnp.sin(ang_r), -jnp.sin(ang_c), jnp.sin(ang_c)], axis=-1)
    return jnp.tile(cos, (1, 2)), jnp.tile(sin, (1, 2))


class _Tiling(NamedTuple):
    token_rows: int
    seqs: int
    seq_rows: int
    parts: int


def _tiling(seq_len):
    if seq_len <= MIXER_ROWS:
        return _Tiling(TOKEN_ROWS, TOKEN_ROWS // seq_len, seq_len, 1)
    return _Tiling(TOKEN_ROWS, 1, MIXER_ROWS, 1)


def _run_path(x, mods, mod_row, weights, ctx, rope_tabs, dft):
    norm_g, wgu, wd, win, wout, cdft, conv_w, lam_qk, subln_g = weights
    b, n, _ = x.shape
    tm, seqs, tr, parts = _tiling(n)
    tiles_per_seq = max(n // tm, 1)
    seqs_per_tile = max(tm // n, 1)

    def mod_row_of_tile(i):
        return mod_row(i * seqs_per_tile // tiles_per_seq)

    def shape3(a):
        return a.reshape(b, n, a.shape[-1])

    ctx_k, ctx_v = ctx if ctx is not None else (None, None)
    caches = None
    for l in range(DEPTH):
        flat = x.reshape(b * n, D_MODEL)
        flat = _ffn(flat, mods, norm_g, wgu, wd, layer=l, half=0, mod_row_of_tile=mod_row_of_tile, tm=tm)
        q, kt, v, gb, u, xcs, *caches = _mixer_in(flat, mods, norm_g, win, cdft, rope_tabs, caches, layer=l,
                                                  mod_row_of_tile=mod_row_of_tile, tm=tm, seq=n)
        caches = caches or None
        x = _mixer_out(shape3(flat), shape3(q), kt, shape3(v), ctx_k, ctx_v, lam_qk, subln_g, shape3(gb),
                       shape3(u), shape3(xcs), dft, conv_w, mods, norm_g, wout, layer=l,
                       mod_row_of_batch=lambda i: mod_row(i * seqs), seqs=seqs, tr=tr, parts=parts)
        flat = _ffn(x.reshape(b * n, D_MODEL), mods, norm_g, wgu, wd, layer=l, half=1,
                    mod_row_of_tile=mod_row_of_tile, tm=tm)
        x = flat.reshape(b, n, D_MODEL)
    return x, caches


def kernel(x_prompt, x_sample, cache_k, cache_v, c, c_ctx, w_mod, b_mod, norm_g, w_ffn_gu, w_ffn_down,
           w_in, w_out, conv_w, lam_qk, subln_g):
    batch, seq, _ = x_prompt.shape
    dec_batch, dec_seq, _ = x_sample.shape
    past = cache_k.shape[2]

    cvecs = jnp.zeros((MOD_ROWS, D_MODEL), F32).at[0].set(c_ctx).at[1:1 + dec_batch].set(c)
    mods = _modulation(cvecs, w_mod, b_mod)

    wgu, wd = w_ffn_gu, w_ffn_down
    win = _cast_bf16(w_in.reshape(DEPTH * D_MODEL, IN_DIM), D_MODEL // 2).reshape(DEPTH, D_MODEL, IN_DIM)
    wout = _cast_bf16(w_out.reshape(DEPTH * D_MODEL, D_MODEL), D_MODEL).reshape(DEPTH, D_MODEL, D_MODEL)
    cdft = jnp.asarray(_channel_dft()).astype(BF16)
    weights = (norm_g, wgu, wd, win, wout, cdft, conv_w, lam_qk, subln_g)

    dft_ctx = jnp.asarray(_position_dft(seq)).astype(BF16)
    y_prompt, (kc, vc) = _run_path(x_prompt, mods, lambda i: 0, weights, None, None, dft_ctx)
    new_cache_k = kc.reshape(batch, DEPTH, N_HEADS, 2, QK_DIM, seq).transpose(0, 1, 5, 2, 3, 4)
    new_cache_v = vc.reshape(batch, DEPTH, seq, N_HEADS, V_DIM)

    dft_lat = jnp.asarray(_position_dft(dec_seq)).astype(BF16)
    ctx = (cache_k.transpose(0, 1, 3, 4, 5, 2).reshape(dec_batch, DEPTH, ATT_DIM, past),
           cache_v.reshape(dec_batch, DEPTH, past * N_HEADS, V_DIM))
    y_sample, _ = _run_path(x_sample, mods, lambda i: 1 + i, weights, ctx, _rope_tables(dec_seq), dft_lat)
    return (y_prompt, y_sample, new_cache_k, new_cache_v)
```

```python
import functools
import math
from typing import NamedTuple

import jax
import jax.numpy as jnp
import numpy as np
from jax import lax
from jax.experimental import pallas as pl
from jax.experimental.pallas import tpu as pltpu

D_MODEL = 1024
DEPTH = 2
GRID_W = 64
V_DIM = 128
QK_DIM = V_DIM // 2
N_HEADS = 4
ATT_DIM = N_HEADS * V_DIM
CONV_DIM = 256
FOURIER_DIM = 256
FOURIER_GROUP_DIM = 64
IN_DIM = 3 * ATT_DIM + 3 * CONV_DIM + FOURIER_DIM
D_FF = 2816
N_MOD = 9
ROT_FREQS = 16
ROPE_THETA = 10000.0
EPS = 1e-6
FFN_RESIDUAL_WEIGHT = 0.5
LOG2_E = math.log2(math.e)

FF_CHUNK = 256
N_FF_CHUNKS = D_FF // FF_CHUNK
FF_GROUP = 3
assert (N_FF_CHUNKS - 2) % FF_GROUP == 0
TOKEN_ROWS = 1024
MIXER_ROWS = 256
MOD_ROWS = 8
MOD_COLS = N_MOD * D_MODEL // 4
SUBLANES = 8
LANES = 128
VMEM_LIMIT = 56 * 1024 * 1024

F32 = jnp.float32
BF16 = jnp.bfloat16


def _lambda_init(layer_idx):
    return 0.8 - 0.6 * math.exp(-0.3 * layer_idx)


def _rms(x, g):
    return x * lax.rsqrt(jnp.mean(x * x, axis=-1, keepdims=True) + EPS) * g


def _dot(a, b):
    return jnp.dot(a, b, preferred_element_type=F32)


def _params(semantics):
    return pltpu.CompilerParams(dimension_semantics=semantics, vmem_limit_bytes=VMEM_LIMIT)


def _resident(block_shape, index_map):
    return pl.BlockSpec(block_shape, index_map, pipeline_mode=pl.Buffered(1))


def _cast_kernel(x_ref, o_ref):
    o_ref[...] = x_ref[...].astype(o_ref.dtype)


def _cast_bf16(x, block_rows):
    rows, cols = x.shape
    return pl.pallas_call(
        _cast_kernel,
        out_shape=jax.ShapeDtypeStruct((rows, cols), BF16),
        grid=(rows // block_rows,),
        in_specs=[pl.BlockSpec((block_rows, cols), lambda i: (i, 0))],
        out_specs=pl.BlockSpec((block_rows, cols), lambda i: (i, 0)),
        compiler_params=_params(("arbitrary",)),
        name="cast_bf16",
    )(x)


def _mod_kernel(c_ref, w_ref, b_ref, o_ref):
    c = c_ref[...]
    s = (c / (1.0 + jnp.exp(-c))).astype(BF16)
    o_ref[...] = _dot(s, w_ref[...].astype(BF16)) + b_ref[...]


def _modulation(cvecs, w_mod, b_mod):
    col = MOD_COLS
    out = pl.pallas_call(
        _mod_kernel,
        out_shape=jax.ShapeDtypeStruct((DEPTH, MOD_ROWS, N_MOD * D_MODEL), F32),
        grid=(DEPTH, N_MOD * D_MODEL // col),
        in_specs=[
            pl.BlockSpec((MOD_ROWS, D_MODEL), lambda l, j: (0, 0)),
            pl.BlockSpec((None, D_MODEL, col), lambda l, j: (l, 0, j)),
            pl.BlockSpec((None, 1, col), lambda l, j: (l, 0, j)),
        ],
        out_specs=pl.BlockSpec((None, MOD_ROWS, col), lambda l, j: (l, 0, j)),
        compiler_params=_params(("arbitrary", "arbitrary")),
        name="modulation",
    )(cvecs, w_mod, b_mod.reshape(DEPTH, 1, N_MOD * D_MODEL))
    return out.reshape(DEPTH, MOD_ROWS, N_MOD, D_MODEL)


def _ffn_kernel(x_ref, mod_ref, g_ref, wg_ref, wu_ref, wdn_ref, o_ref, acc_ref, hb_ref, wgu_sc, wd_sc,
                *, sub, g_row):
    step = pl.program_id(0)
    last = N_FF_CHUNKS - 1
    shift = mod_ref[3 * sub:3 * sub + 1, :]
    scale = mod_ref[3 * sub + 1:3 * sub + 2, :]
    gate = mod_ref[3 * sub + 2:3 * sub + 3, :]

    def pre_norm():
        h = _rms(x_ref[...], g_ref[g_row:g_row + 1, :] * (1.0 + scale)) + shift
        return h.astype(BF16)

    def hidden(hb, c):
        gu = _dot(hb, wgu_sc[c])
        g = gu[:, :FF_CHUNK]
        u = gu[:, FF_CHUNK:]
        return (g / (1.0 + jnp.exp(-g)) * u).astype(BF16)

    def finish(y):
        o_ref[...] = x_ref[...] + _rms(y, g_ref[g_row + 1:g_row + 2, :] * (FFN_RESIDUAL_WEIGHT * gate))

    @pl.when(step <= last)
    def _():
        @pl.when(step == 0)
        def _():
            hb_ref[...] = pre_norm()
            acc_ref[...] = jnp.zeros_like(acc_ref)

        wgu_sc[step, :, 0:FF_CHUNK] = wg_ref[...].astype(BF16)
        wgu_sc[step, :, FF_CHUNK:2 * FF_CHUNK] = wu_ref[...].astype(BF16)
        wd_sc[step] = wdn_ref[...].astype(BF16)
        acc_ref[...] += _dot(hidden(hb_ref[...], step), wd_sc[step])

        @pl.when(step == last)
        def _():
            finish(acc_ref[...])

    @pl.when(step > last)
    def _():
        hb = pre_norm()
        acc_ref[...] = _dot(hidden(hb, 0), wd_sc[0])
        for c0 in range(1, last, FF_GROUP):
            a = jnp.concatenate([hidden(hb, c0 + k) for k in range(FF_GROUP)], axis=-1)
            wd = wd_sc[c0:c0 + FF_GROUP].reshape(FF_GROUP * FF_CHUNK, D_MODEL)
            acc_ref[...] += _dot(a, wd)
        finish(acc_ref[...] + _dot(hidden(hb, last), wd_sc[last]))


def _ffn(x, mods, norm_g, w_gu, w_down, *, layer, half, mod_row_of_tile, tm):
    n = x.shape[0]
    sub = 2 * half
    g_row = 4 * half
    last = N_FF_CHUNKS - 1

    def tile(s):
        return jnp.maximum(s - last, 0)

    def chunk(s):
        return jnp.minimum(s, last)

    return pl.pallas_call(
        functools.partial(_ffn_kernel, sub=sub, g_row=g_row),
        out_shape=jax.ShapeDtypeStruct((n, D_MODEL), F32),
        grid=(n // tm + last,),
        in_specs=[
            pl.BlockSpec((tm, D_MODEL), lambda s: (tile(s), 0)),
            pl.BlockSpec((None, None, N_MOD, D_MODEL), lambda s: (layer, mod_row_of_tile(tile(s)), 0, 0)),
            pl.BlockSpec((None, 6, D_MODEL), lambda s: (layer, 0, 0)),
            pl.BlockSpec((None, None, D_MODEL, FF_CHUNK), lambda s: (layer, half, 0, chunk(s))),
            pl.BlockSpec((None, None, D_MODEL, FF_CHUNK), lambda s: (layer, half, 0, N_FF_CHUNKS + chunk(s))),
            pl.BlockSpec((None, None, FF_CHUNK, D_MODEL), lambda s: (layer, half, chunk(s), 0)),
        ],
        out_specs=pl.BlockSpec((tm, D_MODEL), lambda s: (tile(s), 0)),
        scratch_shapes=[
            pltpu.VMEM((tm, D_MODEL), F32),
            pltpu.VMEM((tm, D_MODEL), BF16),
            pltpu.VMEM((N_FF_CHUNKS, D_MODEL, 2 * FF_CHUNK), BF16),
            pltpu.VMEM((N_FF_CHUNKS, FF_CHUNK, D_MODEL), BF16),
        ],
        compiler_params=_params(("arbitrary",)),
        name="ffn",
    )(x, mods, norm_g, w_gu, w_gu, w_down)


def _rope(x, cos, sin_signed, lo_half):
    partner = jnp.where(lo_half, pltpu.roll(x, LANES - ROT_FREQS, 1), pltpu.roll(x, ROT_FREQS, 1))
    return x * cos + partner * sin_signed


def _mixer_in_kernel(*refs, rotary, seq, cache_slot):
    x_ref, mod_ref, g_ref, win_ref, cdft_ref = refs[:5]
    if rotary:
        cos_ref, sin_ref = refs[5:7]
        q_ref, kt_ref, v_ref, gb_ref, u_ref, xcs_ref = refs[7:]
    else:
        q_ref, kt_ref, v_ref, gb_ref, u_ref, xcs_ref, kc_ref, vc_ref = refs[-8:]
    x = x_ref[...]
    shift = mod_ref[3:4, :]
    scale = mod_ref[4:5, :]
    h = _rms(x, g_ref[2:3, :]) * (1.0 + scale) + shift
    hb = h.astype(BF16)

    q = _dot(hb, win_ref[:, 0:ATT_DIM]) * (QK_DIM ** -0.5 * LOG2_E)
    k = _dot(hb, win_ref[:, ATT_DIM:2 * ATT_DIM])
    v = _dot(hb, win_ref[:, 2 * ATT_DIM:3 * ATT_DIM])
    v_ref[...] = v.astype(v_ref.dtype)
    if rotary:
        cos = cos_ref[...]
        sin = sin_ref[...]
        lane = lax.broadcasted_iota(jnp.int32, (1, LANES), 1)
        lo_half = (lane % (2 * ROT_FREQS)) < ROT_FREQS
        for hd in range(N_HEADS):
            sl = slice(hd * V_DIM, (hd + 1) * V_DIM)
            q_ref[:, sl] = _rope(q[:, sl], cos, sin, lo_half).astype(q_ref.dtype)
            kt_ref[sl, :] = _rope(k[:, sl], cos, sin, lo_half).T.astype(kt_ref.dtype)
    else:
        q_ref[...] = q.astype(q_ref.dtype)
        for s in range(x.shape[0] // seq):
            rows = slice(s * seq, (s + 1) * seq)
            kt = k[rows, :].T
            kt_ref[s] = kt.astype(kt_ref.dtype)
            if cache_slot is None:
                kc_out, vc_out = kc_ref.at[s], vc_ref.at[s]
            else:
                kc_out, vc_out = kc_ref.at[s, cache_slot], vc_ref.at[s, cache_slot]
                for other in range(DEPTH):
                    if other != cache_slot:
                        kc_ref[s, other] = jnp.zeros(kc_ref.shape[2:], F32)
                        vc_ref[s, other] = jnp.zeros(vc_ref.shape[2:], F32)
            kc_out[...] = kt
            for hd in range(N_HEADS):
                vc_out[pl.ds(hd, seq, stride=N_HEADS), :] = v[rows, hd * V_DIM:(hd + 1) * V_DIM]

    c0 = 3 * ATT_DIM
    gates = _dot(hb, win_ref[:, c0:c0 + 3 * CONV_DIM])
    gb_ref[...] = gates[:, 0:CONV_DIM]
    u_ref[...] = gates[:, CONV_DIM:2 * CONV_DIM] * gates[:, 2 * CONV_DIM:3 * CONV_DIM]
    f = _dot(hb, win_ref[:, c0 + 3 * CONV_DIM:IN_DIM])
    xcs_ref[...] = _dot(f.astype(BF16), cdft_ref[...]).astype(BF16)


def _mixer_in(x, mods, norm_g, win, cdft, rope_tabs, caches, *, layer, mod_row_of_tile, tm, seq):
    n = x.shape[0]
    rotary = rope_tabs is not None
    in_specs = [
        pl.BlockSpec((tm, D_MODEL), lambda i: (i, 0)),
        pl.BlockSpec((None, None, N_MOD, D_MODEL), lambda i: (layer, mod_row_of_tile(i), 0, 0)),
        pl.BlockSpec((None, 6, D_MODEL), lambda i: (layer, 0, 0)),
        _resident((None, D_MODEL, IN_DIM), lambda i: (layer, 0, 0)),
        _resident((FOURIER_DIM, 2 * FOURIER_DIM), lambda i: (0, 0)),
    ]
    args = [x, mods, norm_g, win, cdft]
    if rotary:
        tiles_per_seq = rope_tabs[0].shape[0] // tm
        for tab in rope_tabs:
            in_specs.append(pl.BlockSpec((tm, LANES), lambda i: (i % tiles_per_seq, 0)))
            args.append(tab)

    def rows(width):
        return pl.BlockSpec((tm, width), lambda i: (i, 0))

    batch = n // seq
    if tm <= seq:
        tiles_per_seq = seq // tm
        kt_spec = pl.BlockSpec((None, ATT_DIM, tm), lambda i: (i // tiles_per_seq, 0, i % tiles_per_seq))
    else:
        kt_spec = pl.BlockSpec((tm // seq, ATT_DIM, seq), lambda i: (i, 0, 0))
    out_shape = [
        jax.ShapeDtypeStruct((n, ATT_DIM), BF16),
        jax.ShapeDtypeStruct((batch, ATT_DIM, seq), BF16),
        jax.ShapeDtypeStruct((n, ATT_DIM), BF16),
        jax.ShapeDtypeStruct((n, CONV_DIM), F32),
        jax.ShapeDtypeStruct((n, CONV_DIM), F32),
        jax.ShapeDtypeStruct((n, 2 * FOURIER_DIM), BF16),
    ]
    out_specs = [rows(ATT_DIM), kt_spec, rows(ATT_DIM), rows(CONV_DIM), rows(CONV_DIM),
                 rows(2 * FOURIER_DIM)]
    aliases = {}
    if not rotary:
        seqs = tm // seq
        out_shape += [jax.ShapeDtypeStruct((batch, DEPTH, ATT_DIM, seq), F32),
                      jax.ShapeDtypeStruct((batch, DEPTH, N_HEADS * seq, V_DIM), F32)]
        if caches is None:
            out_specs += [pl.BlockSpec((seqs, DEPTH, ATT_DIM, seq), lambda i: (i, 0, 0, 0)),
                          pl.BlockSpec((seqs, DEPTH, N_HEADS * seq, V_DIM), lambda i: (i, 0, 0, 0))]
        else:
            out_specs += [pl.BlockSpec((seqs, None, ATT_DIM, seq), lambda i: (i, layer, 0, 0)),
                          pl.BlockSpec((seqs, None, N_HEADS * seq, V_DIM), lambda i: (i, layer, 0, 0))]
            first_cache_out = len(out_shape) - 2
            for offset, arr in enumerate(caches):
                aliases[len(args)] = first_cache_out + offset
                in_specs.append(pl.BlockSpec(memory_space=pl.ANY))
                args.append(arr)

    cache_slot = layer if (not rotary and caches is None) else None
    return pl.pallas_call(
        functools.partial(_mixer_in_kernel, rotary=rotary, seq=seq, cache_slot=cache_slot),
        out_shape=out_shape,
        grid=(n // tm,),
        in_specs=in_specs,
        out_specs=out_specs,
        input_output_aliases=aliases,
        compiler_params=_params(("arbitrary",)),
        name="mixer_in",
    )(*args)


def _lambda(lam_ref, lambda_init):
    lq = lam_ref[...]
    s01 = jnp.sum(lq[0:1, :] * lq[1:2, :], axis=-1, keepdims=True)
    s23 = jnp.sum(lq[2:3, :] * lq[3:4, :], axis=-1, keepdims=True)
    return jnp.exp(s01) - jnp.exp(s23) + lambda_init


def _stage_values(vext_sc, v_ref, cv_ref, nb, n):
    for b in range(nb):
        for hd in range(N_HEADS):
            sl = slice(hd * V_DIM, (hd + 1) * V_DIM)
            vext_sc[b, hd, 0:n, 0:V_DIM] = v_ref[b, :, sl]
            if cv_ref is not None:
                past = vext_sc.shape[2] - n
                vext_sc[b, hd, n:, 0:V_DIM] = cv_ref[pl.ds(hd, past, stride=N_HEADS), :].astype(BF16)
            vext_sc[b, hd, :, V_DIM:] = jnp.ones((vext_sc.shape[2], V_DIM), BF16)


def _attend(q_ref, kt_sc, vext_sc, o_ref, lam, sub_g, nb):
    lane = lax.broadcasted_iota(jnp.int32, (1, V_DIM), 1)
    first_map = lane < QK_DIM

    def exponentials(b, hd):
        sl = slice(hd * V_DIM, (hd + 1) * V_DIM)
        q = q_ref[b, :, sl]
        zero = jnp.zeros_like(q)
        out = []
        for keep in (first_map, jnp.logical_not(first_map)):
            qm = jnp.where(keep, q, zero)
            s = _dot(qm, kt_sc[b, sl, :])
            m = jnp.max(s, axis=-1, keepdims=True)
            out.append(jnp.exp2((s - m).astype(BF16)))
        return out

    def finish(b, hd, e_maps):
        sl = slice(hd * V_DIM, (hd + 1) * V_DIM)
        vext = vext_sc[b, hd]
        r0, r1 = [_dot(e, vext) for e in e_maps]
        o = r0[:, :V_DIM] / r0[:, V_DIM:] - lam * (r1[:, :V_DIM] / r1[:, V_DIM:])
        o_ref[b, :, sl] = (_rms(o, sub_g)).astype(o_ref.dtype)

    items = [(b, hd) for b in range(nb) for hd in range(N_HEADS)]
    pending = exponentials(*items[0])
    for idx, item in enumerate(items):
        upcoming = exponentials(*items[idx + 1]) if idx + 1 < len(items) else None
        finish(*item, pending)
        pending = upcoming


def _mixer_out_kernel(*refs, seqs, tr, n, parts, has_ctx, lambda_init, dft_resident):
    if has_ctx:
        (x_ref, q_ref, kt_ref, v_ref, ckt_ref, cv_ref, lam_ref, sg_ref, gb_ref, u_ref, xcs_ref, dft_ref, cw_ref,
         mod_ref, g_ref, wout_ref, o_ref, att_ref, vext_sc, kt_sc) = refs
    else:
        (x_ref, q_ref, kt_ref, v_ref, lam_ref, sg_ref, gb_ref, u_ref, xcs_ref, dft_ref, cw_ref,
         mod_ref, g_ref, wout_ref, o_ref, att_ref, vext_sc) = refs
        kt_sc, cv_ref = kt_ref, None

    @pl.when(pl.program_id(1) == 0)
    def _():
        if has_ctx:
            for b in range(seqs):
                kt_sc[b, :, 0:n] = kt_ref[b]
                kt_sc[b, :, n:] = ckt_ref[...].astype(BF16)
        _stage_values(vext_sc, v_ref, cv_ref, seqs, n)

    sub = tr // parts
    gate_g = g_ref[3:4, :] * mod_ref[5:6, :]
    cw = cw_ref[...]
    row = lax.broadcasted_iota(jnp.int32, (sub, 1), 0)
    four_scale = (n * FOURIER_GROUP_DIM) ** -0.5

    def mix(s, t):
        rows = slice(t * sub, (t + 1) * sub)
        r0 = pl.multiple_of(pl.program_id(1) * tr + t * sub, sub)
        dft_rows = pl.ds(r0, sub) if dft_resident else rows
        four = (_dot(dft_ref[dft_rows, 0:n], xcs_ref[s, :, 0:FOURIER_DIM])
                + _dot(dft_ref[dft_rows, n:2 * n], xcs_ref[s, :, FOURIER_DIM:2 * FOURIER_DIM])) * four_scale
        u = u_ref[s, pl.ds(r0, sub), :]
        before_start = pl.multiple_of(jnp.maximum(r0 - SUBLANES, 0), SUBLANES)
        after_start = pl.multiple_of(jnp.minimum(r0 + sub, n - SUBLANES), SUBLANES)
        before = u_ref[s, pl.ds(before_start, SUBLANES), :][SUBLANES - 1:SUBLANES, :]
        after = u_ref[s, pl.ds(after_start, SUBLANES), :][0:1, :]
        before = jnp.where(r0 > 0, before, 0.0)
        after = jnp.where(r0 + sub < n, after, 0.0)
        prev = jnp.where(row == 0, before, pltpu.roll(u, 1, 0))
        nxt = jnp.where(row == sub - 1, after, pltpu.roll(u, sub - 1, 0))
        conv = gb_ref[s, rows, :] * (prev * cw[0:1, :] + u * cw[1:2, :] + nxt * cw[2:3, :])
        return conv.astype(BF16), four.astype(BF16)

    def project(s, t, conv, four):
        rows = slice(t * sub, (t + 1) * sub)
        y = (_dot(att_ref[s, rows, :], wout_ref[0:ATT_DIM, :])
             + _dot(conv, wout_ref[ATT_DIM:ATT_DIM + CONV_DIM, :])
             + _dot(four, wout_ref[ATT_DIM + CONV_DIM:D_MODEL, :]))
        o_ref[s, rows, :] = x_ref[s, rows, :] + _rms(y, gate_g)

    items = [(s, t) for s in range(seqs) for t in range(parts)]
    attend = functools.partial(_attend, q_ref, kt_sc, vext_sc, att_ref, _lambda(lam_ref, lambda_init),
                               sg_ref[...] * (1.0 - lambda_init), seqs)
    if len(items) == 1:
        pending = mix(*items[0])
        attend()
    else:
        attend()
        pending = mix(*items[0])
    for idx, item in enumerate(items):
        upcoming = mix(*items[idx + 1]) if idx + 1 < len(items) else None
        project(*item, *pending)
        pending = upcoming


def _mixer_out(x, q, kt, v, ctx_kt, ctx_v, lam_qk, subln_g, gb, u, xcs, dft, conv_w, mods, norm_g, wout, *,
               layer, mod_row_of_batch, seqs, tr, parts):
    b, n, _ = x.shape
    has_ctx = ctx_kt is not None
    assert not has_ctx or seqs == 1
    dft_resident = tr == n

    def tile(width):
        return pl.BlockSpec((seqs, tr, width), lambda i, j: (i, j, 0))

    def whole(width):
        return pl.BlockSpec((seqs, n, width), lambda i, j: (i, 0, 0))

    in_specs = [tile(D_MODEL), tile(ATT_DIM), pl.BlockSpec((seqs, ATT_DIM, n), lambda i, j: (i, 0, 0)),
                whole(ATT_DIM)]
    args = [x, q, kt, v]
    if has_ctx:
        for arr in (ctx_kt, ctx_v):
            in_specs.append(pl.BlockSpec((None, None) + arr.shape[2:], lambda i, j: (i, layer, 0, 0)))
            args.append(arr)
    in_specs += [
        pl.BlockSpec((None, 4, QK_DIM), lambda i, j: (layer, 0, 0)),
        pl.BlockSpec((None, 1, V_DIM), lambda i, j: (layer, 0, 0)),
        tile(CONV_DIM), whole(CONV_DIM), whole(2 * FOURIER_DIM),
        _resident((n, 2 * n), lambda i, j: (0, 0)) if dft_resident
        else pl.BlockSpec((tr, 2 * n), lambda i, j: (j, 0)),
        pl.BlockSpec((None, 3, CONV_DIM), lambda i, j: (layer, 0, 0)),
        pl.BlockSpec((None, None, N_MOD, D_MODEL), lambda i, j: (layer, mod_row_of_batch(i), 0, 0)),
        pl.BlockSpec((None, 6, D_MODEL), lambda i, j: (layer, 0, 0)),
        _resident((None, D_MODEL, D_MODEL), lambda i, j: (layer, 0, 0)),
    ]
    args += [lam_qk, subln_g.reshape(DEPTH, 1, V_DIM), gb, u, xcs, dft, conv_w, mods, norm_g, wout]
    n_keys = n + (ctx_kt.shape[3] if has_ctx else 0)
    scratch = [pltpu.VMEM((seqs, tr, ATT_DIM), BF16), pltpu.VMEM((seqs, N_HEADS, n_keys, 2 * V_DIM), BF16)]
    if has_ctx:
        scratch.append(pltpu.VMEM((seqs, ATT_DIM, n_keys), BF16))
    return pl.pallas_call(
        functools.partial(_mixer_out_kernel, seqs=seqs, tr=tr, n=n, parts=parts, has_ctx=has_ctx,
                          lambda_init=_lambda_init(layer), dft_resident=dft_resident),
        out_shape=jax.ShapeDtypeStruct((b, n, D_MODEL), F32),
        grid=(b // seqs, n // tr),
        in_specs=in_specs,
        out_specs=tile(D_MODEL),
        scratch_shapes=scratch,
        compiler_params=_params(("arbitrary", "arbitrary")),
        name="mixer_out",
    )(*args)


def _position_dft(n):
    jk = (np.arange(n, dtype=np.int64)[:, None] * np.arange(n, dtype=np.int64)[None, :]) % n
    ang = 2.0 * np.pi * jk.astype(np.float64) / n
    return np.concatenate([np.cos(ang), -np.sin(ang)], axis=1).astype(np.float32)


def _channel_dft():
    g = FOURIER_GROUP_DIM
    jk = (np.arange(g)[:, None] * np.arange(g)[None, :]) % g
    ang = 2.0 * np.pi * jk.astype(np.float64) / g
    eye = np.eye(FOURIER_DIM // g)
    return np.concatenate([np.kron(eye, np.cos(ang)), np.kron(eye, np.sin(ang))], axis=1).astype(np.float32)


def _rope_tables(n):
    t = jnp.arange(n)
    row = (t // GRID_W).astype(F32)
    col = (t % GRID_W).astype(F32)
    inv = 1.0 / (ROPE_THETA ** (jnp.arange(ROT_FREQS, dtype=F32) / ROT_FREQS))
    ang_r = row[:, None] * inv
    ang_c = col[:, None] * inv
    cos = jnp.concatenate([jnp.cos(ang_r)] * 2 + [jnp.cos(ang_c)] * 2, axis=-1)
    sin = jnp.concatenate([-jnp.sin(ang_r), jnp.sin(ang_r), -jnp.sin(ang_c), jnp.sin(ang_c)], axis=-1)
    return jnp.tile(cos, (1, 2)), jnp.tile(sin, (1, 2))


class _Tiling(NamedTuple):
    token_rows: int
    seqs: int
    seq_rows: int
    parts: int


def _tiling(seq_len):
    if seq_len <= MIXER_ROWS:
        return _Tiling(TOKEN_ROWS, TOKEN_ROWS // seq_len, seq_len, 1)
    return _Tiling(TOKEN_ROWS, 1, MIXER_ROWS, 1)


def _run_path(x, mods, mod_row, weights, ctx, rope_tabs, dft):
    norm_g, wgu, wd, win, wout, cdft, conv_w, lam_qk, subln_g = weights
    b, n, _ = x.shape
    tm, seqs, tr, parts = _tiling(n)
    tiles_per_seq = max(n // tm, 1)
    seqs_per_tile = max(tm // n, 1)

    def mod_row_of_tile(i):
        return mod_row(i * seqs_per_tile // tiles_per_seq)

    def shape3(a):
        return a.reshape(b, n, a.shape[-1])

    ctx_k, ctx_v = ctx if ctx is not None else (None, None)
    caches = None
    for l in range(DEPTH):
        flat = x.reshape(b * n, D_MODEL)
        flat = _ffn(flat, mods, norm_g, wgu, wd, layer=l, half=0, mod_row_of_tile=mod_row_of_tile, tm=tm)
        q, kt, v, gb, u, xcs, *caches = _mixer_in(flat, mods, norm_g, win, cdft, rope_tabs, caches, layer=l,
                                                  mod_row_of_tile=mod_row_of_tile, tm=tm, seq=n)
        caches = caches or None
        x = _mixer_out(shape3(flat), shape3(q), kt, shape3(v), ctx_k, ctx_v, lam_qk, subln_g, shape3(gb),
                       shape3(u), shape3(xcs), dft, conv_w, mods, norm_g, wout, layer=l,
                       mod_row_of_batch=lambda i: mod_row(i * seqs), seqs=seqs, tr=tr, parts=parts)
        flat = _ffn(x.reshape(b * n, D_MODEL), mods, norm_g, wgu, wd, layer=l, half=1,
                    mod_row_of_tile=mod_row_of_tile, tm=tm)
        x = flat.reshape(b, n, D_MODEL)
    return x, caches


def kernel(x_prompt, x_sample, cache_k, cache_v, c, c_ctx, w_mod, b_mod, norm_g, w_ffn_gu, w_ffn_down,
           w_in, w_out, conv_w, lam_qk, subln_g):
    batch, seq, _ = x_prompt.shape
    dec_batch, dec_seq, _ = x_sample.shape
    past = cache_k.shape[2]

    cvecs = jnp.zeros((MOD_ROWS, D_MODEL), F32).at[0].set(c_ctx).at[1:1 + dec_batch].set(c)
    mods = _modulation(cvecs, w_mod, b_mod)

    wgu, wd = w_ffn_gu, w_ffn_down
    win = _cast_bf16(w_in.reshape(DEPTH * D_MODEL, IN_DIM), D_MODEL // 2).reshape(DEPTH, D_MODEL, IN_DIM)
    wout = _cast_bf16(w_out.reshape(DEPTH * D_MODEL, D_MODEL), D_MODEL).reshape(DEPTH, D_MODEL, D_MODEL)
    cdft = jnp.asarray(_channel_dft()).astype(BF16)
    weights = (norm_g, wgu, wd, win, wout, cdft, conv_w, lam_qk, subln_g)

    dft_ctx = jnp.asarray(_position_dft(seq)).astype(BF16)
    y_prompt, (kc, vc) = _run_path(x_prompt, mods, lambda i: 0, weights, None, None, dft_ctx)
    new_cache_k = kc.reshape(batch, DEPTH, N_HEADS, 2, QK_DIM, seq).transpose(0, 1, 5, 2, 3, 4)
    new_cache_v = vc.reshape(batch, DEPTH, seq, N_HEADS, V_DIM)

    dft_lat = jnp.asarray(_position_dft(dec_seq)).astype(BF16)
    ctx = (cache_k.transpose(0, 1, 3, 4, 5, 2).reshape(dec_batch, DEPTH, ATT_DIM, past),
           cache_v.reshape(dec_batch, DEPTH, past * N_HEADS, V_DIM))
    y_sample, _ = _run_path(x_sample, mods, lambda i: 1 + i, weights, ctx, _rope_tables(dec_seq), dft_lat)
    return (y_prompt, y_sample, new_cache_k, new_cache_v)
```

```python
import functools
import math
from typing import NamedTuple

import jax
import jax.numpy as jnp
import numpy as np
from jax import lax
from jax.experimental import pallas as pl
from jax.experimental.pallas import tpu as pltpu

D_MODEL = 1024
DEPTH = 2
GRID_W = 64
V_DIM = 128
QK_DIM = V_DIM // 2
N_HEADS = 4
ATT_DIM = N_HEADS * V_DIM
CONV_DIM = 256
FOURIER_DIM = 256
FOURIER_GROUP_DIM = 64
IN_DIM = 3 * ATT_DIM + 3 * CONV_DIM + FOURIER_DIM
D_FF = 2816
N_MOD = 9
ROT_FREQS = 16
ROPE_THETA = 10000.0
EPS = 1e-6
FFN_RESIDUAL_WEIGHT = 0.5
LOG2_E = math.log2(math.e)

FF_CHUNK = 256
N_FF_CHUNKS = D_FF // FF_CHUNK
FF_GROUP = 3
assert (N_FF_CHUNKS - 2) % FF_GROUP == 0
TOKEN_ROWS = 1024
MIXER_ROWS = 256
MOD_ROWS = 8
MOD_COLS = N_MOD * D_MODEL // 4
SUBLANES = 8
LANES = 128
VMEM_LIMIT = 56 * 1024 * 1024

F32 = jnp.float32
BF16 = jnp.bfloat16


def _lambda_init(layer_idx):
    return 0.8 - 0.6 * math.exp(-0.3 * layer_idx)


def _rms(x, g):
    return x * lax.rsqrt(jnp.mean(x * x, axis=-1, keepdims=True) + EPS) * g


def _dot(a, b):
    return jnp.dot(a, b, preferred_element_type=F32)


def _params(semantics):
    return pltpu.CompilerParams(dimension_semantics=semantics, vmem_limit_bytes=VMEM_LIMIT)


def _resident(block_shape, index_map):
    return pl.BlockSpec(block_shape, index_map, pipeline_mode=pl.Buffered(1))


def _cast_kernel(x_ref, o_ref):
    o_ref[...] = x_ref[...].astype(o_ref.dtype)


def _cast_bf16(x, block_rows):
    rows, cols = x.shape
    return pl.pallas_call(
        _cast_kernel,
        out_shape=jax.ShapeDtypeStruct((rows, cols), BF16),
        grid=(rows // block_rows,),
        in_specs=[pl.BlockSpec((block_rows, cols), lambda i: (i, 0))],
        out_specs=pl.BlockSpec((block_rows, cols), lambda i: (i, 0)),
        compiler_params=_params(("arbitrary",)),
        name="cast_bf16",
    )(x)


def _mod_kernel(c_ref, w_ref, b_ref, o_ref):
    c = c_ref[...]
    s = (c / (1.0 + jnp.exp(-c))).astype(BF16)
    o_ref[...] = _dot(s, w_ref[...].astype(BF16)) + b_ref[...]


def _modulation(cvecs, w_mod, b_mod):
    col = MOD_COLS
    out = pl.pallas_call(
        _mod_kernel,
        out_shape=jax.ShapeDtypeStruct((DEPTH, MOD_ROWS, N_MOD * D_MODEL), F32),
        grid=(DEPTH, N_MOD * D_MODEL // col),
        in_specs=[
            pl.BlockSpec((MOD_ROWS, D_MODEL), lambda l, j: (0, 0)),
            pl.BlockSpec((None, D_MODEL, col), lambda l, j: (l, 0, j)),
            pl.BlockSpec((None, 1, col), lambda l, j: (l, 0, j)),
        ],
        out_specs=pl.BlockSpec((None, MOD_ROWS, col), lambda l, j: (l, 0, j)),
        compiler_params=_params(("arbitrary", "arbitrary")),
        name="modulation",
    )(cvecs, w_mod, b_mod.reshape(DEPTH, 1, N_MOD * D_MODEL))
    return out.reshape(DEPTH, MOD_ROWS, N_MOD, D_MODEL)


def _ffn_kernel(x_ref, mod_ref, g_ref, wg_ref, wu_ref, wdn_ref, o_ref, acc_ref, hb_ref, wgu_sc, wd_sc,
                *, sub, g_row):
    step = pl.program_id(0)
    last = N_FF_CHUNKS - 1
    shift = mod_ref[3 * sub:3 * sub + 1, :]
    scale = mod_ref[3 * sub + 1:3 * sub + 2, :]
    gate = mod_ref[3 * sub + 2:3 * sub + 3, :]

    def pre_norm(rows=slice(None)):
        h = _rms(x_ref[rows, :], g_ref[g_row:g_row + 1, :] * (1.0 + scale)) + shift
        return h.astype(BF16)

    def hidden(hb, c):
        gu = _dot(hb, wgu_sc[c])
        g = gu[:, :FF_CHUNK]
        u = gu[:, FF_CHUNK:]
        return (g / (1.0 + jnp.exp(-g)) * u).astype(BF16)

    def finish(y):
        o_ref[...] = x_ref[...] + _rms(y, g_ref[g_row + 1:g_row + 2, :] * (FFN_RESIDUAL_WEIGHT * gate))

    @pl.when(step <= last)
    def _():
        @pl.when(step == 0)
        def _():
            hb_ref[...] = pre_norm()
            acc_ref[...] = jnp.zeros_like(acc_ref)

        wgu_sc[step, :, 0:FF_CHUNK] = wg_ref[...].astype(BF16)
        wgu_sc[step, :, FF_CHUNK:2 * FF_CHUNK] = wu_ref[...].astype(BF16)
        wd_sc[step] = wdn_ref[...].astype(BF16)
        acc_ref[...] += _dot(hidden(hb_ref[...], step), wd_sc[step])

        @pl.when(step == last)
        def _():
            finish(acc_ref[...])

    @pl.when(step > last)
    def _():
        rows = x_ref.shape[0] // 2
        halves = [pre_norm(slice(r * rows, (r + 1) * rows)) for r in range(2)]
        for r, hb_half in enumerate(halves):
            acc_ref[r * rows:(r + 1) * rows, :] = _dot(hidden(hb_half, 0), wd_sc[0])
        hb = jnp.concatenate(halves, axis=0)
        for c0 in range(1, last, FF_GROUP):
            a = jnp.concatenate([hidden(hb, c0 + k) for k in range(FF_GROUP)], axis=-1)
            wd = wd_sc[c0:c0 + FF_GROUP].reshape(FF_GROUP * FF_CHUNK, D_MODEL)
            acc_ref[...] += _dot(a, wd)
        finish(acc_ref[...] + _dot(hidden(hb, last), wd_sc[last]))


def _ffn(x, mods, norm_g, w_gu, w_down, *, layer, half, mod_row_of_tile, tm):
    n = x.shape[0]
    sub = 2 * half
    g_row = 4 * half
    last = N_FF_CHUNKS - 1

    def tile(s):
        return jnp.maximum(s - last, 0)

    def chunk(s):
        return jnp.minimum(s, last)

    return pl.pallas_call(
        functools.partial(_ffn_kernel, sub=sub, g_row=g_row),
        out_shape=jax.ShapeDtypeStruct((n, D_MODEL), F32),
        grid=(n // tm + last,),
        in_specs=[
            pl.BlockSpec((tm, D_MODEL), lambda s: (tile(s), 0)),
            pl.BlockSpec((None, None, N_MOD, D_MODEL), lambda s: (layer, mod_row_of_tile(tile(s)), 0, 0)),
            pl.BlockSpec((None, 6, D_MODEL), lambda s: (layer, 0, 0)),
            pl.BlockSpec((None, None, D_MODEL, FF_CHUNK), lambda s: (layer, half, 0, chunk(s))),
            pl.BlockSpec((None, None, D_MODEL, FF_CHUNK), lambda s: (layer, half, 0, N_FF_CHUNKS + chunk(s))),
            pl.BlockSpec((None, None, FF_CHUNK, D_MODEL), lambda s: (layer, half, chunk(s), 0)),
        ],
        out_specs=pl.BlockSpec((tm, D_MODEL), lambda s: (tile(s), 0)),
        scratch_shapes=[
            pltpu.VMEM((tm, D_MODEL), F32),
            pltpu.VMEM((tm, D_MODEL), BF16),
            pltpu.VMEM((N_FF_CHUNKS, D_MODEL, 2 * FF_CHUNK), BF16),
            pltpu.VMEM((N_FF_CHUNKS, FF_CHUNK, D_MODEL), BF16),
        ],
        compiler_params=_params(("arbitrary",)),
        name="ffn",
    )(x, mods, norm_g, w_gu, w_gu, w_down)


def _rope(x, cos, sin_signed, lo_half):
    partner = jnp.where(lo_half, pltpu.roll(x, LANES - ROT_FREQS, 1), pltpu.roll(x, ROT_FREQS, 1))
    return x * cos + partner * sin_signed


def _mixer_in_kernel(*refs, rotary, seq, cache_slot):
    x_ref, mod_ref, g_ref, win_ref, cdft_ref = refs[:5]
    if rotary:
        cos_ref, sin_ref = refs[5:7]
        q_ref, kt_ref, v_ref, gb_ref, u_ref, xcs_ref = refs[7:]
    else:
        q_ref, kt_ref, v_ref, gb_ref, u_ref, xcs_ref, kc_ref, vc_ref = refs[-8:]
    x = x_ref[...]
    shift = mod_ref[3:4, :]
    scale = mod_ref[4:5, :]
    h = _rms(x, g_ref[2:3, :] * (1.0 + scale)) + shift
    hb = h.astype(BF16)

    q = _dot(hb, win_ref[:, 0:ATT_DIM]) * (QK_DIM ** -0.5 * LOG2_E)
    k = _dot(hb, win_ref[:, ATT_DIM:2 * ATT_DIM])
    v = _dot(hb, win_ref[:, 2 * ATT_DIM:3 * ATT_DIM])
    v_ref[...] = v.astype(v_ref.dtype)
    if rotary:
        cos = cos_ref[...]
        sin = sin_ref[...]
        lane = lax.broadcasted_iota(jnp.int32, (1, LANES), 1)
        lo_half = (lane % (2 * ROT_FREQS)) < ROT_FREQS
        for hd in range(N_HEADS):
            sl = slice(hd * V_DIM, (hd + 1) * V_DIM)
            q_ref[:, sl] = _rope(q[:, sl], cos, sin, lo_half).astype(q_ref.dtype)
            kt_ref[sl, :] = _rope(k[:, sl], cos, sin, lo_half).T.astype(kt_ref.dtype)
    else:
        q_ref[...] = q.astype(q_ref.dtype)
        for s in range(x.shape[0] // seq):
            rows = slice(s * seq, (s + 1) * seq)
            kt = k[rows, :].T
            kt_ref[s] = kt.astype(kt_ref.dtype)
            if cache_slot is None:
                kc_out, vc_out = kc_ref.at[s], vc_ref.at[s]
            else:
                kc_out, vc_out = kc_ref.at[s, cache_slot], vc_ref.at[s, cache_slot]
                for other in range(DEPTH):
                    if other != cache_slot:
                        kc_ref[s, other] = jnp.zeros(kc_ref.shape[2:], F32)
                        vc_ref[s, other] = jnp.zeros(vc_ref.shape[2:], F32)
            kc_out[...] = kt
            for hd in range(N_HEADS):
                vc_out[pl.ds(hd, seq, stride=N_HEADS), :] = v[rows, hd * V_DIM:(hd + 1) * V_DIM]

    c0 = 3 * ATT_DIM
    gates = _dot(hb, win_ref[:, c0:c0 + 3 * CONV_DIM])
    gb_ref[...] = gates[:, 0:CONV_DIM]
    u_ref[...] = gates[:, CONV_DIM:2 * CONV_DIM] * gates[:, 2 * CONV_DIM:3 * CONV_DIM]
    f = _dot(hb, win_ref[:, c0 + 3 * CONV_DIM:IN_DIM])
    xcs_ref[...] = _dot(f.astype(BF16), cdft_ref[...]).astype(BF16)


def _mixer_in(x, mods, norm_g, win, cdft, rope_tabs, caches, *, layer, mod_row_of_tile, tm, seq):
    n = x.shape[0]
    rotary = rope_tabs is not None
    in_specs = [
        pl.BlockSpec((tm, D_MODEL), lambda i: (i, 0)),
        pl.BlockSpec((None, None, N_MOD, D_MODEL), lambda i: (layer, mod_row_of_tile(i), 0, 0)),
        pl.BlockSpec((None, 6, D_MODEL), lambda i: (layer, 0, 0)),
        _resident((None, D_MODEL, IN_DIM), lambda i: (layer, 0, 0)),
        _resident((FOURIER_DIM, 2 * FOURIER_DIM), lambda i: (0, 0)),
    ]
    args = [x, mods, norm_g, win, cdft]
    if rotary:
        tiles_per_seq = rope_tabs[0].shape[0] // tm
        for tab in rope_tabs:
            in_specs.append(pl.BlockSpec((tm, LANES), lambda i: (i % tiles_per_seq, 0)))
            args.append(tab)

    def rows(width):
        return pl.BlockSpec((tm, width), lambda i: (i, 0))

    batch = n // seq
    if tm <= seq:
        tiles_per_seq = seq // tm
        kt_spec = pl.BlockSpec((None, ATT_DIM, tm), lambda i: (i // tiles_per_seq, 0, i % tiles_per_seq))
    else:
        kt_spec = pl.BlockSpec((tm // seq, ATT_DIM, seq), lambda i: (i, 0, 0))
    out_shape = [
        jax.ShapeDtypeStruct((n, ATT_DIM), BF16),
        jax.ShapeDtypeStruct((batch, ATT_DIM, seq), BF16),
        jax.ShapeDtypeStruct((n, ATT_DIM), BF16),
        jax.ShapeDtypeStruct((n, CONV_DIM), F32),
        jax.ShapeDtypeStruct((n, CONV_DIM), F32),
        jax.ShapeDtypeStruct((n, 2 * FOURIER_DIM), BF16),
    ]
    out_specs = [rows(ATT_DIM), kt_spec, rows(ATT_DIM), rows(CONV_DIM), rows(CONV_DIM),
                 rows(2 * FOURIER_DIM)]
    aliases = {}
    if not rotary:
        seqs = tm // seq
        out_shape += [jax.ShapeDtypeStruct((batch, DEPTH, ATT_DIM, seq), F32),
                      jax.ShapeDtypeStruct((batch, DEPTH, N_HEADS * seq, V_DIM), F32)]
        if caches is None:
            out_specs += [pl.BlockSpec((seqs, DEPTH, ATT_DIM, seq), lambda i: (i, 0, 0, 0)),
                          pl.BlockSpec((seqs, DEPTH, N_HEADS * seq, V_DIM), lambda i: (i, 0, 0, 0))]
        else:
            out_specs += [pl.BlockSpec((seqs, None, ATT_DIM, seq), lambda i: (i, layer, 0, 0)),
                          pl.BlockSpec((seqs, None, N_HEADS * seq, V_DIM), lambda i: (i, layer, 0, 0))]
            first_cache_out = len(out_shape) - 2
            for offset, arr in enumerate(caches):
                aliases[len(args)] = first_cache_out + offset
                in_specs.append(pl.BlockSpec(memory_space=pl.ANY))
                args.append(arr)

    cache_slot = layer if (not rotary and caches is None) else None
    return pl.pallas_call(
        functools.partial(_mixer_in_kernel, rotary=rotary, seq=seq, cache_slot=cache_slot),
        out_shape=out_shape,
        grid=(n // tm,),
        in_specs=in_specs,
        out_specs=out_specs,
        input_output_aliases=aliases,
        compiler_params=_params(("arbitrary",)),
        name="mixer_in",
    )(*args)


def _lambda(lam_ref, lambda_init):
    lq = lam_ref[...]
    s01 = jnp.sum(lq[0:1, :] * lq[1:2, :], axis=-1, keepdims=True)
    s23 = jnp.sum(lq[2:3, :] * lq[3:4, :], axis=-1, keepdims=True)
    return jnp.exp(s01) - jnp.exp(s23) + lambda_init


def _stage_values(vext_sc, v_ref, cv_ref, nb, n):
    for b in range(nb):
        for hd in range(N_HEADS):
            sl = slice(hd * V_DIM, (hd + 1) * V_DIM)
            vext_sc[b, hd, 0:n, 0:V_DIM] = v_ref[b, :, sl]
            if cv_ref is not None:
                past = vext_sc.shape[2] - n
                vext_sc[b, hd, n:, 0:V_DIM] = cv_ref[pl.ds(hd, past, stride=N_HEADS), :].astype(BF16)
            vext_sc[b, hd, :, V_DIM:] = jnp.ones((vext_sc.shape[2], V_DIM), BF16)


def _attend(q_ref, kt_sc, vext_sc, o_ref, lam, sub_g, nb):
    lane = lax.broadcasted_iota(jnp.int32, (1, V_DIM), 1)
    first_map = lane < QK_DIM

    def exponentials(b, hd):
        sl = slice(hd * V_DIM, (hd + 1) * V_DIM)
        q = q_ref[b, :, sl]
        zero = jnp.zeros_like(q)
        out = []
        for keep in (first_map, jnp.logical_not(first_map)):
            qm = jnp.where(keep, q, zero)
            s = _dot(qm, kt_sc[b, sl, :])
            m = jnp.max(s, axis=-1, keepdims=True)
            out.append(jnp.exp2((s - m).astype(BF16)))
        return out

    def finish(b, hd, e_maps):
        sl = slice(hd * V_DIM, (hd + 1) * V_DIM)
        vext = vext_sc[b, hd]
        r0, r1 = [_dot(e, vext) for e in e_maps]
        o = r0[:, :V_DIM] / r0[:, V_DIM:] - lam * (r1[:, :V_DIM] / r1[:, V_DIM:])
        o_ref[b, :, sl] = (_rms(o, sub_g)).astype(o_ref.dtype)

    items = [(b, hd) for b in range(nb) for hd in range(N_HEADS)]
    pending = exponentials(*items[0])
    for idx, item in enumerate(items):
        upcoming = exponentials(*items[idx + 1]) if idx + 1 < len(items) else None
        finish(*item, pending)
        pending = upcoming


def _mixer_out_kernel(*refs, seqs, tr, n, parts, has_ctx, lambda_init, dft_resident):
    if has_ctx:
        (x_ref, q_ref, kt_ref, v_ref, ckt_ref, cv_ref, lam_ref, sg_ref, gb_ref, u_ref, xcs_ref, dft_ref, cw_ref,
         mod_ref, g_ref, wout_ref, o_ref, att_ref, vext_sc, kt_sc) = refs
    else:
        (x_ref, q_ref, kt_ref, v_ref, lam_ref, sg_ref, gb_ref, u_ref, xcs_ref, dft_ref, cw_ref,
         mod_ref, g_ref, wout_ref, o_ref, att_ref, vext_sc) = refs
        kt_sc, cv_ref = kt_ref, None

    @pl.when(pl.program_id(1) == 0)
    def _():
        if has_ctx:
            for b in range(seqs):
                kt_sc[b, :, 0:n] = kt_ref[b]
                kt_sc[b, :, n:] = ckt_ref[...].astype(BF16)
        _stage_values(vext_sc, v_ref, cv_ref, seqs, n)

    sub = tr // parts
    gate_g = g_ref[3:4, :] * mod_ref[5:6, :]
    cw = cw_ref[...]
    row = lax.broadcasted_iota(jnp.int32, (sub, 1), 0)
    four_scale = (n * FOURIER_GROUP_DIM) ** -0.5

    def mix(s, t):
        rows = slice(t * sub, (t + 1) * sub)
        r0 = pl.multiple_of(pl.program_id(1) * tr + t * sub, sub)
        dft_rows = pl.ds(r0, sub) if dft_resident else rows
        four = (_dot(dft_ref[dft_rows, 0:n], xcs_ref[s, :, 0:FOURIER_DIM])
                + _dot(dft_ref[dft_rows, n:2 * n], xcs_ref[s, :, FOURIER_DIM:2 * FOURIER_DIM])) * four_scale
        u = u_ref[s, pl.ds(r0, sub), :]
        before_start = pl.multiple_of(jnp.maximum(r0 - SUBLANES, 0), SUBLANES)
        after_start = pl.multiple_of(jnp.minimum(r0 + sub, n - SUBLANES), SUBLANES)
        before = u_ref[s, pl.ds(before_start, SUBLANES), :][SUBLANES - 1:SUBLANES, :]
        after = u_ref[s, pl.ds(after_start, SUBLANES), :][0:1, :]
        before = jnp.where(r0 > 0, before, 0.0)
        after = jnp.where(r0 + sub < n, after, 0.0)
        prev = jnp.where(row == 0, before, pltpu.roll(u, 1, 0))
        nxt = jnp.where(row == sub - 1, after, pltpu.roll(u, sub - 1, 0))
        conv = gb_ref[s, rows, :] * (prev * cw[0:1, :] + u * cw[1:2, :] + nxt * cw[2:3, :])
        return conv.astype(BF16), four.astype(BF16)

    def project(s, t, conv, four):
        rows = slice(t * sub, (t + 1) * sub)
        y = (_dot(att_ref[s, rows, :], wout_ref[0:ATT_DIM, :])
             + _dot(conv, wout_ref[ATT_DIM:ATT_DIM + CONV_DIM, :])
             + _dot(four, wout_ref[ATT_DIM + CONV_DIM:D_MODEL, :]))
        o_ref[s, rows, :] = x_ref[s, rows, :] + _rms(y, gate_g)

    items = [(s, t) for s in range(seqs) for t in range(parts)]
    attend = functools.partial(_attend, q_ref, kt_sc, vext_sc, att_ref, _lambda(lam_ref, lambda_init),
                               sg_ref[...] * (1.0 - lambda_init), seqs)
    if len(items) == 1:
        pending = mix(*items[0])
        attend()
    else:
        attend()
        pending = mix(*items[0])
    for idx, item in enumerate(items):
        upcoming = mix(*items[idx + 1]) if idx + 1 < len(items) else None
        project(*item, *pending)
        pending = upcoming


def _mixer_out(x, q, kt, v, ctx_kt, ctx_v, lam_qk, subln_g, gb, u, xcs, dft, conv_w, mods, norm_g, wout, *,
               layer, mod_row_of_batch, seqs, tr, parts):
    b, n, _ = x.shape
    has_ctx = ctx_kt is not None
    assert not has_ctx or seqs == 1
    dft_resident = tr == n

    def tile(width):
        return pl.BlockSpec((seqs, tr, width), lambda i, j: (i, j, 0))

    def whole(width):
        return pl.BlockSpec((seqs, n, width), lambda i, j: (i, 0, 0))

    in_specs = [tile(D_MODEL), tile(ATT_DIM), pl.BlockSpec((seqs, ATT_DIM, n), lambda i, j: (i, 0, 0)),
                whole(ATT_DIM)]
    args = [x, q, kt, v]
    if has_ctx:
        for arr in (ctx_kt, ctx_v):
            in_specs.append(pl.BlockSpec((None, None) + arr.shape[2:], lambda i, j: (i, layer, 0, 0)))
            args.append(arr)
    in_specs += [
        pl.BlockSpec((None, 4, QK_DIM), lambda i, j: (layer, 0, 0)),
        pl.BlockSpec((None, 1, V_DIM), lambda i, j: (layer, 0, 0)),
        tile(CONV_DIM), whole(CONV_DIM), whole(2 * FOURIER_DIM),
        _resident((n, 2 * n), lambda i, j: (0, 0)) if dft_resident
        else pl.BlockSpec((tr, 2 * n), lambda i, j: (j, 0)),
        pl.BlockSpec((None, 3, CONV_DIM), lambda i, j: (layer, 0, 0)),
        pl.BlockSpec((None, None, N_MOD, D_MODEL), lambda i, j: (layer, mod_row_of_batch(i), 0, 0)),
        pl.BlockSpec((None, 6, D_MODEL), lambda i, j: (layer, 0, 0)),
        _resident((None, D_MODEL, D_MODEL), lambda i, j: (layer, 0, 0)),
    ]
    args += [lam_qk, subln_g.reshape(DEPTH, 1, V_DIM), gb, u, xcs, dft, conv_w, mods, norm_g, wout]
    n_keys = n + (ctx_kt.shape[3] if has_ctx else 0)
    scratch = [pltpu.VMEM((seqs, tr, ATT_DIM), BF16), pltpu.VMEM((seqs, N_HEADS, n_keys, 2 * V_DIM), BF16)]
    if has_ctx:
        scratch.append(pltpu.VMEM((seqs, ATT_DIM, n_keys), BF16))
    return pl.pallas_call(
        functools.partial(_mixer_out_kernel, seqs=seqs, tr=tr, n=n, parts=parts, has_ctx=has_ctx,
                          lambda_init=_lambda_init(layer), dft_resident=dft_resident),
        out_shape=jax.ShapeDtypeStruct((b, n, D_MODEL), F32),
        grid=(b // seqs, n // tr),
        in_specs=in_specs,
        out_specs=tile(D_MODEL),
        scratch_shapes=scratch,
        compiler_params=_params(("arbitrary", "arbitrary")),
        name="mixer_out",
    )(*args)


def _position_dft(n):
    jk = (np.arange(n, dtype=np.int64)[:, None] * np.arange(n, dtype=np.int64)[None, :]) % n
    ang = 2.0 * np.pi * jk.astype(np.float64) / n
    return np.concatenate([np.cos(ang), -np.sin(ang)], axis=1).astype(np.float32)


def _channel_dft():
    g = FOURIER_GROUP_DIM
    jk = (np.arange(g)[:, None] * np.arange(g)[None, :]) % g
    ang = 2.0 * np.pi * jk.astype(np.float64) / g
    eye = np.eye(FOURIER_DIM // g)
    return np.concatenate([np.kron(eye, np.cos(ang)), np.kron(eye, np.sin(ang))], axis=1).astype(np.float32)


def _rope_tables(n):
    t = jnp.arange(n)
    row = (t // GRID_W).astype(F32)
    col = (t % GRID_W).astype(F32)
    inv = 1.0 / (ROPE_THETA ** (jnp.arange(ROT_FREQS, dtype=F32) / ROT_FREQS))
    ang_r = row[:, None] * inv
    ang_c = col[:, None] * inv
    cos = jnp.concatenate([jnp.cos(ang_r)] * 2 + [jnp.cos(ang_c)] * 2, axis=-1)
    sin = jnp.concatenate([-jnp.sin(ang_r), jnp.sin(ang_r), -jnp.sin(ang_c), jnp.sin(ang_c)], axis=-1)
    return jnp.tile(cos, (1, 2)), jnp.tile(sin, (1, 2))


class _Tiling(NamedTuple):
    token_rows: int
    seqs: int
    seq_rows: int
    parts: int


def _tiling(seq_len):
    if seq_len <= MIXER_ROWS:
        return _Tiling(TOKEN_ROWS, TOKEN_ROWS // seq_len, seq_len, 1)
    return _Tiling(TOKEN_ROWS, 1, MIXER_ROWS, 1)


def _run_path(x, mods, mod_row, weights, ctx, rope_tabs, dft):
    norm_g, wgu, wd, win, wout, cdft, conv_w, lam_qk, subln_g = weights
    b, n, _ = x.shape
    tm, seqs, tr, parts = _tiling(n)
    tiles_per_seq = max(n // tm, 1)
    seqs_per_tile = max(tm // n, 1)

    def mod_row_of_tile(i):
        return mod_row(i * seqs_per_tile // tiles_per_seq)

    def shape3(a):
        return a.reshape(b, n, a.shape[-1])

    ctx_k, ctx_v = ctx if ctx is not None else (None, None)
    caches = None
    for l in range(DEPTH):
        flat = x.reshape(b * n, D_MODEL)
        flat = _ffn(flat, mods, norm_g, wgu, wd, layer=l, half=0, mod_row_of_tile=mod_row_of_tile, tm=tm)
        q, kt, v, gb, u, xcs, *caches = _mixer_in(flat, mods, norm_g, win, cdft, rope_tabs, caches, layer=l,
                                                  mod_row_of_tile=mod_row_of_tile, tm=tm, seq=n)
        caches = caches or None
        x = _mixer_out(shape3(flat), shape3(q), kt, shape3(v), ctx_k, ctx_v, lam_qk, subln_g, shape3(gb),
                       shape3(u), shape3(xcs), dft, conv_w, mods, norm_g, wout, layer=l,
                       mod_row_of_batch=lambda i: mod_row(i * seqs), seqs=seqs, tr=tr, parts=parts)
        flat = _ffn(x.reshape(b * n, D_MODEL), mods, norm_g, wgu, wd, layer=l, half=1,
                    mod_row_of_tile=mod_row_of_tile, tm=tm)
        x = flat.reshape(b, n, D_MODEL)
    return x, caches


def kernel(x_prompt, x_sample, cache_k, cache_v, c, c_ctx, w_mod, b_mod, norm_g, w_ffn_gu, w_ffn_down,
           w_in, w_out, conv_w, lam_qk, subln_g):
    batch, seq, _ = x_prompt.shape
    dec_batch, dec_seq, _ = x_sample.shape
    past = cache_k.shape[2]

    cvecs = jnp.zeros((MOD_ROWS, D_MODEL), F32).at[0].set(c_ctx).at[1:1 + dec_batch].set(c)
    mods = _modulation(cvecs, w_mod, b_mod)

    wgu, wd = w_ffn_gu, w_ffn_down
    win = _cast_bf16(w_in.reshape(DEPTH * D_MODEL, IN_DIM), D_MODEL // 2).reshape(DEPTH, D_MODEL, IN_DIM)
    wout = _cast_bf16(w_out.reshape(DEPTH * D_MODEL, D_MODEL), D_MODEL).reshape(DEPTH, D_MODEL, D_MODEL)
    cdft = jnp.asarray(_channel_dft()).astype(BF16)
    weights = (norm_g, wgu, wd, win, wout, cdft, conv_w, lam_qk, subln_g)

    dft_ctx = jnp.asarray(_position_dft(seq)).astype(BF16)
    y_prompt, (kc, vc) = _run_path(x_prompt, mods, lambda i: 0, weights, None, None, dft_ctx)
    new_cache_k = kc.reshape(batch, DEPTH, N_HEADS, 2, QK_DIM, seq).transpose(0, 1, 5, 2, 3, 4)
    new_cache_v = vc.reshape(batch, DEPTH, seq, N_HEADS, V_DIM)

    dft_lat = jnp.asarray(_position_dft(dec_seq)).astype(BF16)
    ctx = (cache_k.transpose(0, 1, 3, 4, 5, 2).reshape(dec_batch, DEPTH, ATT_DIM, past),
           cache_v.reshape(dec_batch, DEPTH, past * N_HEADS, V_DIM))
    y_sample, _ = _run_path(x_sample, mods, lambda i: 1 + i, weights, ctx, _rope_tables(dec_seq), dft_lat)
    return (y_prompt, y_sample, new_cache_k, new_cache_v)
```

```python
import functools
import math
from typing import NamedTuple

import jax
import jax.numpy as jnp
import numpy as np
from jax import lax
from jax.experimental import pallas as pl
from jax.experimental.pallas import tpu as pltpu

D_MODEL = 1024
DEPTH = 2
GRID_W = 64
V_DIM = 128
QK_DIM = V_DIM // 2
N_HEADS = 4
ATT_DIM = N_HEADS * V_DIM
CONV_DIM = 256
FOURIER_DIM = 256
FOURIER_GROUP_DIM = 64
IN_DIM = 3 * ATT_DIM + 3 * CONV_DIM + FOURIER_DIM
D_FF = 2816
N_MOD = 9
ROT_FREQS = 16
ROPE_THETA = 10000.0
EPS = 1e-6
FFN_RESIDUAL_WEIGHT = 0.5
LOG2_E = math.log2(math.e)

FF_CHUNK = 256
N_FF_CHUNKS = D_FF // FF_CHUNK
FF_GROUP = 3
assert (N_FF_CHUNKS - 2) % FF_GROUP == 0
TOKEN_ROWS = 1024
MIXER_ROWS = 256
MOD_ROWS = 8
MOD_COLS = N_MOD * D_MODEL // 4
SUBLANES = 8
LANES = 128
VMEM_LIMIT = 56 * 1024 * 1024

F32 = jnp.float32
BF16 = jnp.bfloat16


def _lambda_init(layer_idx):
    return 0.8 - 0.6 * math.exp(-0.3 * layer_idx)


def _rms(x, g):
    return x * lax.rsqrt(jnp.mean(x * x, axis=-1, keepdims=True) + EPS) * g


def _dot(a, b):
    return jnp.dot(a, b, preferred_element_type=F32)


def _params(semantics):
    return pltpu.CompilerParams(dimension_semantics=semantics, vmem_limit_bytes=VMEM_LIMIT)


def _resident(block_shape, index_map):
    return pl.BlockSpec(block_shape, index_map, pipeline_mode=pl.Buffered(1))


def _cast_kernel(x_ref, o_ref):
    o_ref[...] = x_ref[...].astype(o_ref.dtype)


def _cast_bf16(x, block_rows):
    rows, cols = x.shape
    return pl.pallas_call(
        _cast_kernel,
        out_shape=jax.ShapeDtypeStruct((rows, cols), BF16),
        grid=(rows // block_rows,),
        in_specs=[pl.BlockSpec((block_rows, cols), lambda i: (i, 0))],
        out_specs=pl.BlockSpec((block_rows, cols), lambda i: (i, 0)),
        compiler_params=_params(("arbitrary",)),
        name="cast_bf16",
    )(x)


def _mod_kernel(c_ref, w_ref, b_ref, o_ref):
    c = c_ref[...]
    s = (c / (1.0 + jnp.exp(-c))).astype(BF16)
    o_ref[...] = _dot(s, w_ref[...].astype(BF16)) + b_ref[...]


def _modulation(cvecs, w_mod, b_mod):
    col = MOD_COLS
    out = pl.pallas_call(
        _mod_kernel,
        out_shape=jax.ShapeDtypeStruct((DEPTH, MOD_ROWS, N_MOD * D_MODEL), F32),
        grid=(DEPTH, N_MOD * D_MODEL // col),
        in_specs=[
            pl.BlockSpec((MOD_ROWS, D_MODEL), lambda l, j: (0, 0)),
            pl.BlockSpec((None, D_MODEL, col), lambda l, j: (l, 0, j)),
            pl.BlockSpec((None, 1, col), lambda l, j: (l, 0, j)),
        ],
        out_specs=pl.BlockSpec((None, MOD_ROWS, col), lambda l, j: (l, 0, j)),
        compiler_params=_params(("arbitrary", "arbitrary")),
        name="modulation",
    )(cvecs, w_mod, b_mod.reshape(DEPTH, 1, N_MOD * D_MODEL))
    return out.reshape(DEPTH, MOD_ROWS, N_MOD, D_MODEL)


def _ffn_kernel(x_ref, mod_ref, g_ref, wg_ref, wu_ref, wdn_ref, o_ref, acc_ref, hb_ref, wgu_sc, wd_sc,
                *, sub, g_row):
    step = pl.program_id(0)
    last = N_FF_CHUNKS - 1
    shift = mod_ref[3 * sub:3 * sub + 1, :]
    scale = mod_ref[3 * sub + 1:3 * sub + 2, :]
    gate = mod_ref[3 * sub + 2:3 * sub + 3, :]

    def pre_norm(rows=slice(None)):
        h = _rms(x_ref[rows, :], g_ref[g_row:g_row + 1, :] * (1.0 + scale)) + shift
        return h.astype(BF16)

    def hidden(hb, c):
        gu = _dot(hb, wgu_sc[c])
        g = gu[:, :FF_CHUNK]
        u = gu[:, FF_CHUNK:]
        return (g / (1.0 + jnp.exp(-g)) * u).astype(BF16)

    def finish(y):
        o_ref[...] = x_ref[...] + _rms(y, g_ref[g_row + 1:g_row + 2, :] * (FFN_RESIDUAL_WEIGHT * gate))

    @pl.when(step <= last)
    def _():
        @pl.when(step == 0)
        def _():
            hb_ref[...] = pre_norm()
            acc_ref[...] = jnp.zeros_like(acc_ref)

        wgu_sc[step, :, 0:FF_CHUNK] = wg_ref[...].astype(BF16)
        wgu_sc[step, :, FF_CHUNK:2 * FF_CHUNK] = wu_ref[...].astype(BF16)
        wd_sc[step] = wdn_ref[...].astype(BF16)
        acc_ref[...] += _dot(hidden(hb_ref[...], step), wd_sc[step])

        @pl.when(step == last)
        def _():
            finish(acc_ref[...])

    @pl.when(step > last)
    def _():
        rows = x_ref.shape[0] // 2
        halves = [pre_norm(slice(r * rows, (r + 1) * rows)) for r in range(2)]
        for r, hb_half in enumerate(halves):
            acc_ref[r * rows:(r + 1) * rows, :] = _dot(hidden(hb_half, 0), wd_sc[0])
        hb = jnp.concatenate(halves, axis=0)
        for c0 in range(1, last, FF_GROUP):
            a = jnp.concatenate([hidden(hb, c0 + k) for k in range(FF_GROUP)], axis=-1)
            wd = wd_sc[c0:c0 + FF_GROUP].reshape(FF_GROUP * FF_CHUNK, D_MODEL)
            acc_ref[...] += _dot(a, wd)
        finish(acc_ref[...] + _dot(hidden(hb, last), wd_sc[last]))


def _ffn(x, mods, norm_g, w_gu, w_down, *, layer, half, mod_row_of_tile, tm):
    n = x.shape[0]
    sub = 2 * half
    g_row = 4 * half
    last = N_FF_CHUNKS - 1

    def tile(s):
        return jnp.maximum(s - last, 0)

    def chunk(s):
        return jnp.minimum(s, last)

    return pl.pallas_call(
        functools.partial(_ffn_kernel, sub=sub, g_row=g_row),
        out_shape=jax.ShapeDtypeStruct((n, D_MODEL), F32),
        grid=(n // tm + last,),
        in_specs=[
            pl.BlockSpec((tm, D_MODEL), lambda s: (tile(s), 0)),
            pl.BlockSpec((None, None, N_MOD, D_MODEL), lambda s: (layer, mod_row_of_tile(tile(s)), 0, 0)),
            pl.BlockSpec((None, 6, D_MODEL), lambda s: (layer, 0, 0)),
            pl.BlockSpec((None, None, D_MODEL, FF_CHUNK), lambda s: (layer, half, 0, chunk(s))),
            pl.BlockSpec((None, None, D_MODEL, FF_CHUNK), lambda s: (layer, half, 0, N_FF_CHUNKS + chunk(s))),
            pl.BlockSpec((None, None, FF_CHUNK, D_MODEL), lambda s: (layer, half, chunk(s), 0)),
        ],
        out_specs=pl.BlockSpec((tm, D_MODEL), lambda s: (tile(s), 0)),
        scratch_shapes=[
            pltpu.VMEM((tm, D_MODEL), F32),
            pltpu.VMEM((tm, D_MODEL), BF16),
            pltpu.VMEM((N_FF_CHUNKS, D_MODEL, 2 * FF_CHUNK), BF16),
            pltpu.VMEM((N_FF_CHUNKS, FF_CHUNK, D_MODEL), BF16),
        ],
        compiler_params=_params(("arbitrary",)),
        name="ffn",
    )(x, mods, norm_g, w_gu, w_gu, w_down)


def _rope(x, cos, sin_signed, lo_half):
    partner = jnp.where(lo_half, pltpu.roll(x, LANES - ROT_FREQS, 1), pltpu.roll(x, ROT_FREQS, 1))
    return x * cos + partner * sin_signed


def _mixer_in_kernel(*refs, rotary, seq, cache_slot):
    x_ref, mod_ref, g_ref, win_ref, cdft_ref = refs[:5]
    if rotary:
        cos_ref, sin_ref = refs[5:7]
        q_ref, kt_ref, v_ref, gb_ref, u_ref, xcs_ref = refs[7:]
    else:
        q_ref, kt_ref, v_ref, gb_ref, u_ref, xcs_ref, kc_ref, vc_ref = refs[-8:]
    x = x_ref[...]
    shift = mod_ref[3:4, :]
    scale = mod_ref[4:5, :]
    gain = g_ref[2:3, :] * (1.0 + scale)
    rows = x.shape[0] // 2
    halves = [(_rms(x[r * rows:(r + 1) * rows, :], gain) + shift).astype(BF16) for r in range(2)]
    hb = jnp.concatenate(halves, axis=0)

    q = jnp.concatenate([_dot(hb_half, win_ref[:, 0:ATT_DIM]) for hb_half in halves], axis=0)
    q = q * (QK_DIM ** -0.5 * LOG2_E)
    k = _dot(hb, win_ref[:, ATT_DIM:2 * ATT_DIM])
    v = _dot(hb, win_ref[:, 2 * ATT_DIM:3 * ATT_DIM])
    v_ref[...] = v.astype(v_ref.dtype)
    if rotary:
        cos = cos_ref[...]
        sin = sin_ref[...]
        lane = lax.broadcasted_iota(jnp.int32, (1, LANES), 1)
        lo_half = (lane % (2 * ROT_FREQS)) < ROT_FREQS
        for hd in range(N_HEADS):
            sl = slice(hd * V_DIM, (hd + 1) * V_DIM)
            q_ref[:, sl] = _rope(q[:, sl], cos, sin, lo_half).astype(q_ref.dtype)
            kt_ref[sl, :] = _rope(k[:, sl], cos, sin, lo_half).T.astype(kt_ref.dtype)
    else:
        q_ref[...] = q.astype(q_ref.dtype)
        for s in range(x.shape[0] // seq):
            rows = slice(s * seq, (s + 1) * seq)
            kt = k[rows, :].T
            kt_ref[s] = kt.astype(kt_ref.dtype)
            if cache_slot is None:
                kc_out, vc_out = kc_ref.at[s], vc_ref.at[s]
            else:
                kc_out, vc_out = kc_ref.at[s, cache_slot], vc_ref.at[s, cache_slot]
                for other in range(DEPTH):
                    if other != cache_slot:
                        kc_ref[s, other] = jnp.zeros(kc_ref.shape[2:], F32)
                        vc_ref[s, other] = jnp.zeros(vc_ref.shape[2:], F32)
            kc_out[...] = kt
            for hd in range(N_HEADS):
                vc_out[pl.ds(hd, seq, stride=N_HEADS), :] = v[rows, hd * V_DIM:(hd + 1) * V_DIM]

    c0 = 3 * ATT_DIM
    gates = _dot(hb, win_ref[:, c0:c0 + 3 * CONV_DIM])
    gb_ref[...] = gates[:, 0:CONV_DIM]
    u_ref[...] = gates[:, CONV_DIM:2 * CONV_DIM] * gates[:, 2 * CONV_DIM:3 * CONV_DIM]
    f = _dot(hb, win_ref[:, c0 + 3 * CONV_DIM:IN_DIM])
    xcs_ref[...] = _dot(f.astype(BF16), cdft_ref[...]).astype(BF16)


def _mixer_in(x, mods, norm_g, win, cdft, rope_tabs, caches, *, layer, mod_row_of_tile, tm, seq):
    n = x.shape[0]
    rotary = rope_tabs is not None
    in_specs = [
        pl.BlockSpec((tm, D_MODEL), lambda i: (i, 0)),
        pl.BlockSpec((None, None, N_MOD, D_MODEL), lambda i: (layer, mod_row_of_tile(i), 0, 0)),
        pl.BlockSpec((None, 6, D_MODEL), lambda i: (layer, 0, 0)),
        _resident((None, D_MODEL, IN_DIM), lambda i: (layer, 0, 0)),
        _resident((FOURIER_DIM, 2 * FOURIER_DIM), lambda i: (0, 0)),
    ]
    args = [x, mods, norm_g, win, cdft]
    if rotary:
        tiles_per_seq = rope_tabs[0].shape[0] // tm
        for tab in rope_tabs:
            in_specs.append(pl.BlockSpec((tm, LANES), lambda i: (i % tiles_per_seq, 0)))
            args.append(tab)

    def rows(width):
        return pl.BlockSpec((tm, width), lambda i: (i, 0))

    batch = n // seq
    if tm <= seq:
        tiles_per_seq = seq // tm
        kt_spec = pl.BlockSpec((None, ATT_DIM, tm), lambda i: (i // tiles_per_seq, 0, i % tiles_per_seq))
    else:
        kt_spec = pl.BlockSpec((tm // seq, ATT_DIM, seq), lambda i: (i, 0, 0))
    out_shape = [
        jax.ShapeDtypeStruct((n, ATT_DIM), BF16),
        jax.ShapeDtypeStruct((batch, ATT_DIM, seq), BF16),
        jax.ShapeDtypeStruct((n, ATT_DIM), BF16),
        jax.ShapeDtypeStruct((n, CONV_DIM), F32),
        jax.ShapeDtypeStruct((n, CONV_DIM), F32),
        jax.ShapeDtypeStruct((n, 2 * FOURIER_DIM), BF16),
    ]
    out_specs = [rows(ATT_DIM), kt_spec, rows(ATT_DIM), rows(CONV_DIM), rows(CONV_DIM),
                 rows(2 * FOURIER_DIM)]
    aliases = {}
    if not rotary:
        seqs = tm // seq
        out_shape += [jax.ShapeDtypeStruct((batch, DEPTH, ATT_DIM, seq), F32),
                      jax.ShapeDtypeStruct((batch, DEPTH, N_HEADS * seq, V_DIM), F32)]
        if caches is None:
            out_specs += [pl.BlockSpec((seqs, DEPTH, ATT_DIM, seq), lambda i: (i, 0, 0, 0)),
                          pl.BlockSpec((seqs, DEPTH, N_HEADS * seq, V_DIM), lambda i: (i, 0, 0, 0))]
        else:
            out_specs += [pl.BlockSpec((seqs, None, ATT_DIM, seq), lambda i: (i, layer, 0, 0)),
                          pl.BlockSpec((seqs, None, N_HEADS * seq, V_DIM), lambda i: (i, layer, 0, 0))]
            first_cache_out = len(out_shape) - 2
            for offset, arr in enumerate(caches):
                aliases[len(args)] = first_cache_out + offset
                in_specs.append(pl.BlockSpec(memory_space=pl.ANY))
                args.append(arr)

    cache_slot = layer if (not rotary and caches is None) else None
    return pl.pallas_call(
        functools.partial(_mixer_in_kernel, rotary=rotary, seq=seq, cache_slot=cache_slot),
        out_shape=out_shape,
        grid=(n // tm,),
        in_specs=in_specs,
        out_specs=out_specs,
        input_output_aliases=aliases,
        compiler_params=_params(("arbitrary",)),
        name="mixer_in",
    )(*args)


def _lambda(lam_ref, lambda_init):
    lq = lam_ref[...]
    s01 = jnp.sum(lq[0:1, :] * lq[1:2, :], axis=-1, keepdims=True)
    s23 = jnp.sum(lq[2:3, :] * lq[3:4, :], axis=-1, keepdims=True)
    return jnp.exp(s01) - jnp.exp(s23) + lambda_init


def _stage_values(vext_sc, v_ref, cv_ref, nb, n):
    for b in range(nb):
        for hd in range(N_HEADS):
            sl = slice(hd * V_DIM, (hd + 1) * V_DIM)
            vext_sc[b, hd, 0:n, 0:V_DIM] = v_ref[b, :, sl]
            if cv_ref is not None:
                past = vext_sc.shape[2] - n
                vext_sc[b, hd, n:, 0:V_DIM] = cv_ref[pl.ds(hd, past, stride=N_HEADS), :].astype(BF16)
            vext_sc[b, hd, :, V_DIM:] = jnp.ones((vext_sc.shape[2], V_DIM), BF16)


def _attend(q_ref, kt_sc, vext_sc, o_ref, lam, sub_g, nb):
    lane = lax.broadcasted_iota(jnp.int32, (1, V_DIM), 1)
    first_map = lane < QK_DIM

    def exponentials(b, hd):
        sl = slice(hd * V_DIM, (hd + 1) * V_DIM)
        q = q_ref[b, :, sl]
        zero = jnp.zeros_like(q)
        out = []
        for keep in (first_map, jnp.logical_not(first_map)):
            qm = jnp.where(keep, q, zero)
            s = _dot(qm, kt_sc[b, sl, :])
            m = jnp.max(s, axis=-1, keepdims=True)
            out.append(jnp.exp2((s - m).astype(BF16)))
        return out

    def finish(b, hd, e_maps):
        sl = slice(hd * V_DIM, (hd + 1) * V_DIM)
        vext = vext_sc[b, hd]
        r0, r1 = [_dot(e, vext) for e in e_maps]
        o = r0[:, :V_DIM] / r0[:, V_DIM:] - lam * (r1[:, :V_DIM] / r1[:, V_DIM:])
        o_ref[b, :, sl] = (_rms(o, sub_g)).astype(o_ref.dtype)

    items = [(b, hd) for b in range(nb) for hd in range(N_HEADS)]
    pending = exponentials(*items[0])
    for idx, item in enumerate(items):
        upcoming = exponentials(*items[idx + 1]) if idx + 1 < len(items) else None
        finish(*item, pending)
        pending = upcoming


def _mixer_out_kernel(*refs, seqs, tr, n, parts, has_ctx, lambda_init, dft_resident):
    if has_ctx:
        (x_ref, q_ref, kt_ref, v_ref, ckt_ref, cv_ref, lam_ref, sg_ref, gb_ref, u_ref, xcs_ref, dft_ref, cw_ref,
         mod_ref, g_ref, wout_ref, o_ref, att_ref, vext_sc, kt_sc) = refs
    else:
        (x_ref, q_ref, kt_ref, v_ref, lam_ref, sg_ref, gb_ref, u_ref, xcs_ref, dft_ref, cw_ref,
         mod_ref, g_ref, wout_ref, o_ref, att_ref, vext_sc) = refs
        kt_sc, cv_ref = kt_ref, None

    @pl.when(pl.program_id(1) == 0)
    def _():
        if has_ctx:
            for b in range(seqs):
                kt_sc[b, :, 0:n] = kt_ref[b]
                kt_sc[b, :, n:] = ckt_ref[...].astype(BF16)
        _stage_values(vext_sc, v_ref, cv_ref, seqs, n)

    sub = tr // parts
    gate_g = g_ref[3:4, :] * mod_ref[5:6, :]
    cw = cw_ref[...]
    row = lax.broadcasted_iota(jnp.int32, (sub, 1), 0)
    four_scale = (n * FOURIER_GROUP_DIM) ** -0.5

    def mix(s, t):
        rows = slice(t * sub, (t + 1) * sub)
        r0 = pl.multiple_of(pl.program_id(1) * tr + t * sub, sub)
        dft_rows = pl.ds(r0, sub) if dft_resident else rows
        four = (_dot(dft_ref[dft_rows, 0:n], xcs_ref[s, :, 0:FOURIER_DIM])
                + _dot(dft_ref[dft_rows, n:2 * n], xcs_ref[s, :, FOURIER_DIM:2 * FOURIER_DIM])) * four_scale
        u = u_ref[s, pl.ds(r0, sub), :]
        before_start = pl.multiple_of(jnp.maximum(r0 - SUBLANES, 0), SUBLANES)
        after_start = pl.multiple_of(jnp.minimum(r0 + sub, n - SUBLANES), SUBLANES)
        before = u_ref[s, pl.ds(before_start, SUBLANES), :][SUBLANES - 1:SUBLANES, :]
        after = u_ref[s, pl.ds(after_start, SUBLANES), :][0:1, :]
        before = jnp.where(r0 > 0, before, 0.0)
        after = jnp.where(r0 + sub < n, after, 0.0)
        prev = jnp.where(row == 0, before, pltpu.roll(u, 1, 0))
        nxt = jnp.where(row == sub - 1, after, pltpu.roll(u, sub - 1, 0))
        conv = gb_ref[s, rows, :] * (prev * cw[0:1, :] + u * cw[1:2, :] + nxt * cw[2:3, :])
        return conv.astype(BF16), four.astype(BF16)

    def project(s, t, conv, four):
        rows = slice(t * sub, (t + 1) * sub)
        y = (_dot(att_ref[s, rows, :], wout_ref[0:ATT_DIM, :])
             + _dot(conv, wout_ref[ATT_DIM:ATT_DIM + CONV_DIM, :])
             + _dot(four, wout_ref[ATT_DIM + CONV_DIM:D_MODEL, :]))
        o_ref[s, rows, :] = x_ref[s, rows, :] + _rms(y, gate_g)

    items = [(s, t) for s in range(seqs) for t in range(parts)]
    attend = functools.partial(_attend, q_ref, kt_sc, vext_sc, att_ref, _lambda(lam_ref, lambda_init),
                               sg_ref[...] * (1.0 - lambda_init), seqs)
    if len(items) == 1:
        pending = mix(*items[0])
        attend()
    else:
        attend()
        pending = mix(*items[0])
    for idx, item in enumerate(items):
        upcoming = mix(*items[idx + 1]) if idx + 1 < len(items) else None
        project(*item, *pending)
        pending = upcoming


def _mixer_out(x, q, kt, v, ctx_kt, ctx_v, lam_qk, subln_g, gb, u, xcs, dft, conv_w, mods, norm_g, wout, *,
               layer, mod_row_of_batch, seqs, tr, parts):
    b, n, _ = x.shape
    has_ctx = ctx_kt is not None
    assert not has_ctx or seqs == 1
    dft_resident = tr == n

    def tile(width):
        return pl.BlockSpec((seqs, tr, width), lambda i, j: (i, j, 0))

    def whole(width):
        return pl.BlockSpec((seqs, n, width), lambda i, j: (i, 0, 0))

    in_specs = [tile(D_MODEL), tile(ATT_DIM), pl.BlockSpec((seqs, ATT_DIM, n), lambda i, j: (i, 0, 0)),
                whole(ATT_DIM)]
    args = [x, q, kt, v]
    if has_ctx:
        for arr in (ctx_kt, ctx_v):
            in_specs.append(pl.BlockSpec((None, None) + arr.shape[2:], lambda i, j: (i, layer, 0, 0)))
            args.append(arr)
    in_specs += [
        pl.BlockSpec((None, 4, QK_DIM), lambda i, j: (layer, 0, 0)),
        pl.BlockSpec((None, 1, V_DIM), lambda i, j: (layer, 0, 0)),
        tile(CONV_DIM), whole(CONV_DIM), whole(2 * FOURIER_DIM),
        _resident((n, 2 * n), lambda i, j: (0, 0)) if dft_resident
        else pl.BlockSpec((tr, 2 * n), lambda i, j: (j, 0)),
        pl.BlockSpec((None, 3, CONV_DIM), lambda i, j: (layer, 0, 0)),
        pl.BlockSpec((None, None, N_MOD, D_MODEL), lambda i, j: (layer, mod_row_of_batch(i), 0, 0)),
        pl.BlockSpec((None, 6, D_MODEL), lambda i, j: (layer, 0, 0)),
        _resident((None, D_MODEL, D_MODEL), lambda i, j: (layer, 0, 0)),
    ]
    args += [lam_qk, subln_g.reshape(DEPTH, 1, V_DIM), gb, u, xcs, dft, conv_w, mods, norm_g, wout]
    n_keys = n + (ctx_kt.shape[3] if has_ctx else 0)
    scratch = [pltpu.VMEM((seqs, tr, ATT_DIM), BF16), pltpu.VMEM((seqs, N_HEADS, n_keys, 2 * V_DIM), BF16)]
    if has_ctx:
        scratch.append(pltpu.VMEM((seqs, ATT_DIM, n_keys), BF16))
    return pl.pallas_call(
        functools.partial(_mixer_out_kernel, seqs=seqs, tr=tr, n=n, parts=parts, has_ctx=has_ctx,
                          lambda_init=_lambda_init(layer), dft_resident=dft_resident),
        out_shape=jax.ShapeDtypeStruct((b, n, D_MODEL), F32),
        grid=(b // seqs, n // tr),
        in_specs=in_specs,
        out_specs=tile(D_MODEL),
        scratch_shapes=scratch,
        compiler_params=_params(("arbitrary", "arbitrary")),
        name="mixer_out",
    )(*args)


def _position_dft(n):
    jk = (np.arange(n, dtype=np.int64)[:, None] * np.arange(n, dtype=np.int64)[None, :]) % n
    ang = 2.0 * np.pi * jk.astype(np.float64) / n
    return np.concatenate([np.cos(ang), -np.sin(ang)], axis=1).astype(np.float32)


def _channel_dft():
    g = FOURIER_GROUP_DIM
    jk = (np.arange(g)[:, None] * np.arange(g)[None, :]) % g
    ang = 2.0 * np.pi * jk.astype(np.float64) / g
    eye = np.eye(FOURIER_DIM // g)
    return np.concatenate([np.kron(eye, np.cos(ang)), np.kron(eye, np.sin(ang))], axis=1).astype(np.float32)


def _rope_tables(n):
    t = jnp.arange(n)
    row = (t // GRID_W).astype(F32)
    col = (t % GRID_W).astype(F32)
    inv = 1.0 / (ROPE_THETA ** (jnp.arange(ROT_FREQS, dtype=F32) / ROT_FREQS))
    ang_r = row[:, None] * inv
    ang_c = col[:, None] * inv
    cos = jnp.concatenate([jnp.cos(ang_r)] * 2 + [jnp.cos(ang_c)] * 2, axis=-1)
    sin = jnp.concatenate([-jnp.sin(ang_r), jnp.sin(ang_r), -jnp.sin(ang_c), jnp.sin(ang_c)], axis=-1)
    return jnp.tile(cos, (1, 2)), jnp.tile(sin, (1, 2))


class _Tiling(NamedTuple):
    token_rows: int
    seqs: int
    seq_rows: int
    parts: int


def _tiling(seq_len):
    if seq_len <= MIXER_ROWS:
        return _Tiling(TOKEN_ROWS, TOKEN_ROWS // seq_len, seq_len, 1)
    return _Tiling(TOKEN_ROWS, 1, MIXER_ROWS, 1)


def _run_path(x, mods, mod_row, weights, ctx, rope_tabs, dft):
    norm_g, wgu, wd, win, wout, cdft, conv_w, lam_qk, subln_g = weights
    b, n, _ = x.shape
    tm, seqs, tr, parts = _tiling(n)
    tiles_per_seq = max(n // tm, 1)
    seqs_per_tile = max(tm // n, 1)

    def mod_row_of_tile(i):
        return mod_row(i * seqs_per_tile // tiles_per_seq)

    def shape3(a):
        return a.reshape(b, n, a.shape[-1])

    ctx_k, ctx_v = ctx if ctx is not None else (None, None)
    caches = None
    for l in range(DEPTH):
        flat = x.reshape(b * n, D_MODEL)
        flat = _ffn(flat, mods, norm_g, wgu, wd, layer=l, half=0, mod_row_of_tile=mod_row_of_tile, tm=tm)
        q, kt, v, gb, u, xcs, *caches = _mixer_in(flat, mods, norm_g, win, cdft, rope_tabs, caches, layer=l,
                                                  mod_row_of_tile=mod_row_of_tile, tm=tm, seq=n)
        caches = caches or None
        x = _mixer_out(shape3(flat), shape3(q), kt, shape3(v), ctx_k, ctx_v, lam_qk, subln_g, shape3(gb),
                       shape3(u), shape3(xcs), dft, conv_w, mods, norm_g, wout, layer=l,
                       mod_row_of_batch=lambda i: mod_row(i * seqs), seqs=seqs, tr=tr, parts=parts)
        flat = _ffn(x.reshape(b * n, D_MODEL), mods, norm_g, wgu, wd, layer=l, half=1,
                    mod_row_of_tile=mod_row_of_tile, tm=tm)
        x = flat.reshape(b, n, D_MODEL)
    return x, caches


def kernel(x_prompt, x_sample, cache_k, cache_v, c, c_ctx, w_mod, b_mod, norm_g, w_ffn_gu, w_ffn_down,
           w_in, w_out, conv_w, lam_qk, subln_g):
    batch, seq, _ = x_prompt.shape
    dec_batch, dec_seq, _ = x_sample.shape
    past = cache_k.shape[2]

    cvecs = jnp.zeros((MOD_ROWS, D_MODEL), F32).at[0].set(c_ctx).at[1:1 + dec_batch].set(c)
    mods = _modulation(cvecs, w_mod, b_mod)

    wgu, wd = w_ffn_gu, w_ffn_down
    win = _cast_bf16(w_in.reshape(DEPTH * D_MODEL, IN_DIM), D_MODEL // 2).reshape(DEPTH, D_MODEL, IN_DIM)
    wout = _cast_bf16(w_out.reshape(DEPTH * D_MODEL, D_MODEL), D_MODEL).reshape(DEPTH, D_MODEL, D_MODEL)
    cdft = jnp.asarray(_channel_dft()).astype(BF16)
    weights = (norm_g, wgu, wd, win, wout, cdft, conv_w, lam_qk, subln_g)

    dft_ctx = jnp.asarray(_position_dft(seq)).astype(BF16)
    y_prompt, (kc, vc) = _run_path(x_prompt, mods, lambda i: 0, weights, None, None, dft_ctx)
    new_cache_k = kc.reshape(batch, DEPTH, N_HEADS, 2, QK_DIM, seq).transpose(0, 1, 5, 2, 3, 4)
    new_cache_v = vc.reshape(batch, DEPTH, seq, N_HEADS, V_DIM)

    dft_lat = jnp.asarray(_position_dft(dec_seq)).astype(BF16)
    ctx = (cache_k.transpose(0, 1, 3, 4, 5, 2).reshape(dec_batch, DEPTH, ATT_DIM, past),
           cache_v.reshape(dec_batch, DEPTH, past * N_HEADS, V_DIM))
    y_sample, _ = _run_path(x_sample, mods, lambda i: 1 + i, weights, ctx, _rope_tables(dec_seq), dft_lat)
    return (y_prompt, y_sample, new_cache_k, new_cache_v)
```

```python
import functools
import math
from typing import NamedTuple

import jax
import jax.numpy as jnp
import numpy as np
from jax import lax
from jax.experimental import pallas as pl
from jax.experimental.pallas import tpu as pltpu

D_MODEL = 1024
DEPTH = 2
GRID_W = 64
V_DIM = 128
QK_DIM = V_DIM // 2
N_HEADS = 4
ATT_DIM = N_HEADS * V_DIM
CONV_DIM = 256
FOURIER_DIM = 256
FOURIER_GROUP_DIM = 64
IN_DIM = 3 * ATT_DIM + 3 * CONV_DIM + FOURIER_DIM
D_FF = 2816
N_MOD = 9
ROT_FREQS = 16
ROPE_THETA = 10000.0
EPS = 1e-6
FFN_RESIDUAL_WEIGHT = 0.5
LOG2_E = math.log2(math.e)

FF_CHUNK = 256
N_FF_CHUNKS = D_FF // FF_CHUNK
FF_GROUP = 3
assert (N_FF_CHUNKS - 2) % FF_GROUP == 0
TOKEN_ROWS = 1024
MIXER_ROWS = 256
MOD_ROWS = 8
MOD_COLS = N_MOD * D_MODEL // 4
SUBLANES = 8
LANES = 128
VMEM_LIMIT = 56 * 1024 * 1024

F32 = jnp.float32
BF16 = jnp.bfloat16


def _lambda_init(layer_idx):
    return 0.8 - 0.6 * math.exp(-0.3 * layer_idx)


def _rms(x, g):
    return x * lax.rsqrt(jnp.mean(x * x, axis=-1, keepdims=True) + EPS) * g


def _dot(a, b):
    return jnp.dot(a, b, preferred_element_type=F32)


def _params(semantics):
    return pltpu.CompilerParams(dimension_semantics=semantics, vmem_limit_bytes=VMEM_LIMIT)


def _resident(block_shape, index_map):
    return pl.BlockSpec(block_shape, index_map, pipeline_mode=pl.Buffered(1))


def _cast_kernel(x_ref, o_ref):
    o_ref[...] = x_ref[...].astype(o_ref.dtype)


def _cast_bf16(x, block_rows):
    rows, cols = x.shape
    return pl.pallas_call(
        _cast_kernel,
        out_shape=jax.ShapeDtypeStruct((rows, cols), BF16),
        grid=(rows // block_rows,),
        in_specs=[pl.BlockSpec((block_rows, cols), lambda i: (i, 0))],
        out_specs=pl.BlockSpec((block_rows, cols), lambda i: (i, 0)),
        compiler_params=_params(("arbitrary",)),
        name="cast_bf16",
    )(x)


def _mod_kernel(c_ref, w_ref, b_ref, o_ref):
    c = c_ref[...]
    s = (c / (1.0 + jnp.exp(-c))).astype(BF16)
    o_ref[...] = _dot(s, w_ref[...].astype(BF16)) + b_ref[...]


def _modulation(cvecs, w_mod, b_mod):
    col = MOD_COLS
    out = pl.pallas_call(
        _mod_kernel,
        out_shape=jax.ShapeDtypeStruct((DEPTH, MOD_ROWS, N_MOD * D_MODEL), F32),
        grid=(DEPTH, N_MOD * D_MODEL // col),
        in_specs=[
            pl.BlockSpec((MOD_ROWS, D_MODEL), lambda l, j: (0, 0)),
            pl.BlockSpec((None, D_MODEL, col), lambda l, j: (l, 0, j)),
            pl.BlockSpec((None, 1, col), lambda l, j: (l, 0, j)),
        ],
        out_specs=pl.BlockSpec((None, MOD_ROWS, col), lambda l, j: (l, 0, j)),
        compiler_params=_params(("arbitrary", "arbitrary")),
        name="modulation",
    )(cvecs, w_mod, b_mod.reshape(DEPTH, 1, N_MOD * D_MODEL))
    return out.reshape(DEPTH, MOD_ROWS, N_MOD, D_MODEL)


def _ffn_kernel(x_ref, mod_ref, g_ref, wg_ref, wu_ref, wdn_ref, o_ref, acc_ref, hb_ref, wgu_sc, wd_sc,
                *, sub, g_row):
    step = pl.program_id(0)
    last = N_FF_CHUNKS - 1
    shift = mod_ref[3 * sub:3 * sub + 1, :]
    scale = mod_ref[3 * sub + 1:3 * sub + 2, :]
    gate = mod_ref[3 * sub + 2:3 * sub + 3, :]

    def pre_norm(rows=slice(None)):
        h = _rms(x_ref[rows, :], g_ref[g_row:g_row + 1, :] * (1.0 + scale)) + shift
        return h.astype(BF16)

    def hidden(hb, c):
        gu = _dot(hb, wgu_sc[c])
        g = gu[:, :FF_CHUNK]
        u = gu[:, FF_CHUNK:]
        return (g / (1.0 + jnp.exp(-g)) * u).astype(BF16)

    def finish(y):
        o_ref[...] = x_ref[...] + _rms(y, g_ref[g_row + 1:g_row + 2, :] * (FFN_RESIDUAL_WEIGHT * gate))

    @pl.when(step <= last)
    def _():
        @pl.when(step == 0)
        def _():
            hb_ref[...] = pre_norm()
            acc_ref[...] = jnp.zeros_like(acc_ref)

        wgu_sc[step, :, 0:FF_CHUNK] = wg_ref[...].astype(BF16)
        wgu_sc[step, :, FF_CHUNK:2 * FF_CHUNK] = wu_ref[...].astype(BF16)
        wd_sc[step] = wdn_ref[...].astype(BF16)
        acc_ref[...] += _dot(hidden(hb_ref[...], step), wd_sc[step])

        @pl.when(step == last)
        def _():
            finish(acc_ref[...])

    @pl.when(step > last)
    def _():
        rows = x_ref.shape[0] // 2
        halves = [pre_norm(slice(r * rows, (r + 1) * rows)) for r in range(2)]
        for r, hb_half in enumerate(halves):
            acc_ref[r * rows:(r + 1) * rows, :] = _dot(hidden(hb_half, 0), wd_sc[0])
        hb = jnp.concatenate(halves, axis=0)
        for c0 in range(1, last, FF_GROUP):
            a = jnp.concatenate([hidden(hb, c0 + k) for k in range(FF_GROUP)], axis=-1)
            wd = wd_sc[c0:c0 + FF_GROUP].reshape(FF_GROUP * FF_CHUNK, D_MODEL)
            acc_ref[...] += _dot(a, wd)
        finish(acc_ref[...] + _dot(hidden(hb, last), wd_sc[last]))


def _ffn(x, mods, norm_g, w_gu, w_down, *, layer, half, mod_row_of_tile, tm):
    n = x.shape[0]
    sub = 2 * half
    g_row = 4 * half
    last = N_FF_CHUNKS - 1

    def tile(s):
        return jnp.maximum(s - last, 0)

    def chunk(s):
        return jnp.minimum(s, last)

    return pl.pallas_call(
        functools.partial(_ffn_kernel, sub=sub, g_row=g_row),
        out_shape=jax.ShapeDtypeStruct((n, D_MODEL), F32),
        grid=(n // tm + last,),
        in_specs=[
            pl.BlockSpec((tm, D_MODEL), lambda s: (tile(s), 0)),
            pl.BlockSpec((None, None, N_MOD, D_MODEL), lambda s: (layer, mod_row_of_tile(tile(s)), 0, 0)),
            pl.BlockSpec((None, 6, D_MODEL), lambda s: (layer, 0, 0)),
            pl.BlockSpec((None, None, D_MODEL, FF_CHUNK), lambda s: (layer, half, 0, chunk(s))),
            pl.BlockSpec((None, None, D_MODEL, FF_CHUNK), lambda s: (layer, half, 0, N_FF_CHUNKS + chunk(s))),
            pl.BlockSpec((None, None, FF_CHUNK, D_MODEL), lambda s: (layer, half, chunk(s), 0)),
        ],
        out_specs=pl.BlockSpec((tm, D_MODEL), lambda s: (tile(s), 0)),
        scratch_shapes=[
            pltpu.VMEM((tm, D_MODEL), F32),
            pltpu.VMEM((tm, D_MODEL), BF16),
            pltpu.VMEM((N_FF_CHUNKS, D_MODEL, 2 * FF_CHUNK), BF16),
            pltpu.VMEM((N_FF_CHUNKS, FF_CHUNK, D_MODEL), BF16),
        ],
        compiler_params=_params(("arbitrary",)),
        name="ffn",
    )(x, mods, norm_g, w_gu, w_gu, w_down)


def _rope(x, cos, sin_signed, lo_half):
    partner = jnp.where(lo_half, pltpu.roll(x, LANES - ROT_FREQS, 1), pltpu.roll(x, ROT_FREQS, 1))
    return x * cos + partner * sin_signed


def _mixer_in_kernel(*refs, rotary, seq, cache_slot):
    x_ref, mod_ref, g_ref, win_ref, cdft_ref = refs[:5]
    if rotary:
        cos_ref, sin_ref = refs[5:7]
        q_ref, kt_ref, v_ref, gb_ref, u_ref, xcs_ref = refs[7:]
    else:
        q_ref, kt_ref, v_ref, gb_ref, u_ref, xcs_ref, kc_ref, vc_ref = refs[-8:]
    x = x_ref[...]
    shift = mod_ref[3:4, :]
    scale = mod_ref[4:5, :]
    gain = g_ref[2:3, :] * (1.0 + scale)
    rows = x.shape[0] // 2
    halves = [(_rms(x[r * rows:(r + 1) * rows, :], gain) + shift).astype(BF16) for r in range(2)]
    hb = jnp.concatenate(halves, axis=0)

    q = jnp.concatenate([_dot(hb_half, win_ref[:, 0:ATT_DIM]) for hb_half in halves], axis=0)
    q = q * (QK_DIM ** -0.5 * LOG2_E)
    k = _dot(hb, win_ref[:, ATT_DIM:2 * ATT_DIM])
    v = _dot(hb, win_ref[:, 2 * ATT_DIM:3 * ATT_DIM])
    v_ref[...] = v.astype(v_ref.dtype)
    if rotary:
        cos = cos_ref[...]
        sin = sin_ref[...]
        lane = lax.broadcasted_iota(jnp.int32, (1, LANES), 1)
        lo_half = (lane % (2 * ROT_FREQS)) < ROT_FREQS
        for hd in range(N_HEADS):
            sl = slice(hd * V_DIM, (hd + 1) * V_DIM)
            q_ref[:, sl] = _rope(q[:, sl], cos, sin, lo_half).astype(q_ref.dtype)
            kt_ref[sl, :] = _rope(k[:, sl], cos, sin, lo_half).T.astype(kt_ref.dtype)
    else:
        q_ref[...] = q.astype(q_ref.dtype)
        for s in range(x.shape[0] // seq):
            rows = slice(s * seq, (s + 1) * seq)
            kt = k[rows, :].T
            kt_ref[s] = kt.astype(kt_ref.dtype)
            if cache_slot is None:
                kc_out, vc_out = kc_ref.at[s], vc_ref.at[s]
            else:
                kc_out, vc_out = kc_ref.at[s, cache_slot], vc_ref.at[s, cache_slot]
                for other in range(DEPTH):
                    if other != cache_slot:
                        kc_ref[s, other] = jnp.zeros(kc_ref.shape[2:], F32)
                        vc_ref[s, other] = jnp.zeros(vc_ref.shape[2:], F32)
            kc_out[...] = kt
            for hd in range(N_HEADS):
                vc_out[pl.ds(hd, seq, stride=N_HEADS), :] = v[rows, hd * V_DIM:(hd + 1) * V_DIM]

    c0 = 3 * ATT_DIM
    gates = _dot(hb, win_ref[:, c0:c0 + 3 * CONV_DIM])
    gb_ref[...] = gates[:, 0:CONV_DIM]
    u_ref[...] = gates[:, CONV_DIM:2 * CONV_DIM] * gates[:, 2 * CONV_DIM:3 * CONV_DIM]
    f = _dot(hb, win_ref[:, c0 + 3 * CONV_DIM:IN_DIM])
    xcs_ref[...] = _dot(f.astype(BF16), cdft_ref[...]).astype(BF16)


def _mixer_in(x, mods, norm_g, win, cdft, rope_tabs, caches, *, layer, mod_row_of_tile, tm, seq):
    n = x.shape[0]
    rotary = rope_tabs is not None
    in_specs = [
        pl.BlockSpec((tm, D_MODEL), lambda i: (i, 0)),
        pl.BlockSpec((None, None, N_MOD, D_MODEL), lambda i: (layer, mod_row_of_tile(i), 0, 0)),
        pl.BlockSpec((None, 6, D_MODEL), lambda i: (layer, 0, 0)),
        _resident((None, D_MODEL, IN_DIM), lambda i: (layer, 0, 0)),
        _resident((FOURIER_DIM, 2 * FOURIER_DIM), lambda i: (0, 0)),
    ]
    args = [x, mods, norm_g, win, cdft]
    if rotary:
        tiles_per_seq = rope_tabs[0].shape[0] // tm
        for tab in rope_tabs:
            in_specs.append(pl.BlockSpec((tm, LANES), lambda i: (i % tiles_per_seq, 0)))
            args.append(tab)

    def rows(width):
        return pl.BlockSpec((tm, width), lambda i: (i, 0))

    batch = n // seq
    if tm <= seq:
        tiles_per_seq = seq // tm
        kt_spec = pl.BlockSpec((None, ATT_DIM, tm), lambda i: (i // tiles_per_seq, 0, i % tiles_per_seq))
    else:
        kt_spec = pl.BlockSpec((tm // seq, ATT_DIM, seq), lambda i: (i, 0, 0))
    out_shape = [
        jax.ShapeDtypeStruct((n, ATT_DIM), BF16),
        jax.ShapeDtypeStruct((batch, ATT_DIM, seq), BF16),
        jax.ShapeDtypeStruct((n, ATT_DIM), BF16),
        jax.ShapeDtypeStruct((n, CONV_DIM), F32),
        jax.ShapeDtypeStruct((n, CONV_DIM), F32),
        jax.ShapeDtypeStruct((n, 2 * FOURIER_DIM), BF16),
    ]
    out_specs = [rows(ATT_DIM), kt_spec, rows(ATT_DIM), rows(CONV_DIM), rows(CONV_DIM),
                 rows(2 * FOURIER_DIM)]
    aliases = {}
    if not rotary:
        seqs = tm // seq
        out_shape += [jax.ShapeDtypeStruct((batch, DEPTH, ATT_DIM, seq), F32),
                      jax.ShapeDtypeStruct((batch, DEPTH, N_HEADS * seq, V_DIM), F32)]
        if caches is None:
            out_specs += [pl.BlockSpec((seqs, DEPTH, ATT_DIM, seq), lambda i: (i, 0, 0, 0)),
                          pl.BlockSpec((seqs, DEPTH, N_HEADS * seq, V_DIM), lambda i: (i, 0, 0, 0))]
        else:
            out_specs += [pl.BlockSpec((seqs, None, ATT_DIM, seq), lambda i: (i, layer, 0, 0)),
                          pl.BlockSpec((seqs, None, N_HEADS * seq, V_DIM), lambda i: (i, layer, 0, 0))]
            first_cache_out = len(out_shape) - 2
            for offset, arr in enumerate(caches):
                aliases[len(args)] = first_cache_out + offset
                in_specs.append(pl.BlockSpec(memory_space=pl.ANY))
                args.append(arr)

    cache_slot = layer if (not rotary and caches is None) else None
    return pl.pallas_call(
        functools.partial(_mixer_in_kernel, rotary=rotary, seq=seq, cache_slot=cache_slot),
        out_shape=out_shape,
        grid=(n // tm,),
        in_specs=in_specs,
        out_specs=out_specs,
        input_output_aliases=aliases,
        compiler_params=_params(("arbitrary",)),
        name="mixer_in",
    )(*args)


def _lambda(lam_ref, lambda_init):
    lq = lam_ref[...]
    s01 = jnp.sum(lq[0:1, :] * lq[1:2, :], axis=-1, keepdims=True)
    s23 = jnp.sum(lq[2:3, :] * lq[3:4, :], axis=-1, keepdims=True)
    return jnp.exp(s01) - jnp.exp(s23) + lambda_init


def _stage_values(vext_sc, v_ref, cv_ref, nb, n):
    for b in range(nb):
        for hd in range(N_HEADS):
            sl = slice(hd * V_DIM, (hd + 1) * V_DIM)
            vext_sc[b, hd, 0:n, 0:V_DIM] = v_ref[b, :, sl]
            if cv_ref is not None:
                past = vext_sc.shape[2] - n
                vext_sc[b, hd, n:, 0:V_DIM] = cv_ref[pl.ds(hd, past, stride=N_HEADS), :].astype(BF16)
            vext_sc[b, hd, :, V_DIM:] = jnp.ones((vext_sc.shape[2], V_DIM), BF16)


def _attend(q_ref, kt_sc, vext_sc, o_ref, lam, sub_g, nb):
    lane = lax.broadcasted_iota(jnp.int32, (1, V_DIM), 1)
    first_map = lane < QK_DIM

    def exponentials(b, hd):
        sl = slice(hd * V_DIM, (hd + 1) * V_DIM)
        q = q_ref[b, :, sl]
        zero = jnp.zeros_like(q)
        out = []
        for keep in (first_map, jnp.logical_not(first_map)):
            qm = jnp.where(keep, q, zero)
            s = _dot(qm, kt_sc[b, sl, :])
            m = jnp.max(s, axis=-1, keepdims=True)
            out.append(jnp.exp2(s - m).astype(BF16))
        return out

    def finish(b, hd, e_maps):
        sl = slice(hd * V_DIM, (hd + 1) * V_DIM)
        vext = vext_sc[b, hd]
        r0, r1 = [_dot(e, vext) for e in e_maps]
        o = r0[:, :V_DIM] / r0[:, V_DIM:] - lam * (r1[:, :V_DIM] / r1[:, V_DIM:])
        o_ref[b, :, sl] = (_rms(o, sub_g)).astype(o_ref.dtype)

    items = [(b, hd) for b in range(nb) for hd in range(N_HEADS)]
    pending = exponentials(*items[0])
    for idx, item in enumerate(items):
        upcoming = exponentials(*items[idx + 1]) if idx + 1 < len(items) else None
        finish(*item, pending)
        pending = upcoming


def _mixer_out_kernel(*refs, seqs, tr, n, parts, has_ctx, lambda_init, dft_resident):
    if has_ctx:
        (x_ref, q_ref, kt_ref, v_ref, ckt_ref, cv_ref, lam_ref, sg_ref, gb_ref, u_ref, xcs_ref, dft_ref, cw_ref,
         mod_ref, g_ref, wout_ref, o_ref, att_ref, vext_sc, kt_sc) = refs
    else:
        (x_ref, q_ref, kt_ref, v_ref, lam_ref, sg_ref, gb_ref, u_ref, xcs_ref, dft_ref, cw_ref,
         mod_ref, g_ref, wout_ref, o_ref, att_ref, vext_sc) = refs
        kt_sc, cv_ref = kt_ref, None

    @pl.when(pl.program_id(1) == 0)
    def _():
        if has_ctx:
            for b in range(seqs):
                kt_sc[b, :, 0:n] = kt_ref[b]
                kt_sc[b, :, n:] = ckt_ref[...].astype(BF16)
        _stage_values(vext_sc, v_ref, cv_ref, seqs, n)

    sub = tr // parts
    gate_g = g_ref[3:4, :] * mod_ref[5:6, :]
    cw = cw_ref[...]
    row = lax.broadcasted_iota(jnp.int32, (sub, 1), 0)
    four_scale = (n * FOURIER_GROUP_DIM) ** -0.5

    def mix(s, t):
        rows = slice(t * sub, (t + 1) * sub)
        r0 = pl.multiple_of(pl.program_id(1) * tr + t * sub, sub)
        dft_rows = pl.ds(r0, sub) if dft_resident else rows
        four = (_dot(dft_ref[dft_rows, 0:n], xcs_ref[s, :, 0:FOURIER_DIM])
                + _dot(dft_ref[dft_rows, n:2 * n], xcs_ref[s, :, FOURIER_DIM:2 * FOURIER_DIM])) * four_scale
        u = u_ref[s, pl.ds(r0, sub), :]
        before_start = pl.multiple_of(jnp.maximum(r0 - SUBLANES, 0), SUBLANES)
        after_start = pl.multiple_of(jnp.minimum(r0 + sub, n - SUBLANES), SUBLANES)
        before = u_ref[s, pl.ds(before_start, SUBLANES), :][SUBLANES - 1:SUBLANES, :]
        after = u_ref[s, pl.ds(after_start, SUBLANES), :][0:1, :]
        before = jnp.where(r0 > 0, before, 0.0)
        after = jnp.where(r0 + sub < n, after, 0.0)
        prev = jnp.where(row == 0, before, pltpu.roll(u, 1, 0))
        nxt = jnp.where(row == sub - 1, after, pltpu.roll(u, sub - 1, 0))
        conv = gb_ref[s, rows, :] * (prev * cw[0:1, :] + u * cw[1:2, :] + nxt * cw[2:3, :])
        return conv.astype(BF16), four.astype(BF16)

    def project(s, t, conv, four):
        rows = slice(t * sub, (t + 1) * sub)
        y = (_dot(att_ref[s, rows, :], wout_ref[0:ATT_DIM, :])
             + _dot(conv, wout_ref[ATT_DIM:ATT_DIM + CONV_DIM, :])
             + _dot(four, wout_ref[ATT_DIM + CONV_DIM:D_MODEL, :]))
        o_ref[s, rows, :] = x_ref[s, rows, :] + _rms(y, gate_g)

    items = [(s, t) for s in range(seqs) for t in range(parts)]
    attend = functools.partial(_attend, q_ref, kt_sc, vext_sc, att_ref, _lambda(lam_ref, lambda_init),
                               sg_ref[...] * (1.0 - lambda_init), seqs)
    if len(items) == 1:
        pending = mix(*items[0])
        attend()
    else:
        attend()
        pending = mix(*items[0])
    for idx, item in enumerate(items):
        upcoming = mix(*items[idx + 1]) if idx + 1 < len(items) else None
        project(*item, *pending)
        pending = upcoming


def _mixer_out(x, q, kt, v, ctx_kt, ctx_v, lam_qk, subln_g, gb, u, xcs, dft, conv_w, mods, norm_g, wout, *,
               layer, mod_row_of_batch, seqs, tr, parts):
    b, n, _ = x.shape
    has_ctx = ctx_kt is not None
    assert not has_ctx or seqs == 1
    dft_resident = tr == n

    def tile(width):
        return pl.BlockSpec((seqs, tr, width), lambda i, j: (i, j, 0))

    def whole(width):
        return pl.BlockSpec((seqs, n, width), lambda i, j: (i, 0, 0))

    in_specs = [tile(D_MODEL), tile(ATT_DIM), pl.BlockSpec((seqs, ATT_DIM, n), lambda i, j: (i, 0, 0)),
                whole(ATT_DIM)]
    args = [x, q, kt, v]
    if has_ctx:
        for arr in (ctx_kt, ctx_v):
            in_specs.append(pl.BlockSpec((None, None) + arr.shape[2:], lambda i, j: (i, layer, 0, 0)))
            args.append(arr)
    in_specs += [
        pl.BlockSpec((None, 4, QK_DIM), lambda i, j: (layer, 0, 0)),
        pl.BlockSpec((None, 1, V_DIM), lambda i, j: (layer, 0, 0)),
        tile(CONV_DIM), whole(CONV_DIM), whole(2 * FOURIER_DIM),
        _resident((n, 2 * n), lambda i, j: (0, 0)) if dft_resident
        else pl.BlockSpec((tr, 2 * n), lambda i, j: (j, 0)),
        pl.BlockSpec((None, 3, CONV_DIM), lambda i, j: (layer, 0, 0)),
        pl.BlockSpec((None, None, N_MOD, D_MODEL), lambda i, j: (layer, mod_row_of_batch(i), 0, 0)),
        pl.BlockSpec((None, 6, D_MODEL), lambda i, j: (layer, 0, 0)),
        _resident((None, D_MODEL, D_MODEL), lambda i, j: (layer, 0, 0)),
    ]
    args += [lam_qk, subln_g.reshape(DEPTH, 1, V_DIM), gb, u, xcs, dft, conv_w, mods, norm_g, wout]
    n_keys = n + (ctx_kt.shape[3] if has_ctx else 0)
    scratch = [pltpu.VMEM((seqs, tr, ATT_DIM), BF16), pltpu.VMEM((seqs, N_HEADS, n_keys, 2 * V_DIM), BF16)]
    if has_ctx:
        scratch.append(pltpu.VMEM((seqs, ATT_DIM, n_keys), BF16))
    return pl.pallas_call(
        functools.partial(_mixer_out_kernel, seqs=seqs, tr=tr, n=n, parts=parts, has_ctx=has_ctx,
                          lambda_init=_lambda_init(layer), dft_resident=dft_resident),
        out_shape=jax.ShapeDtypeStruct((b, n, D_MODEL), F32),
        grid=(b // seqs, n // tr),
        in_specs=in_specs,
        out_specs=tile(D_MODEL),
        scratch_shapes=scratch,
        compiler_params=_params(("arbitrary", "arbitrary")),
        name="mixer_out",
    )(*args)


def _position_dft(n):
    jk = (np.arange(n, dtype=np.int64)[:, None] * np.arange(n, dtype=np.int64)[None, :]) % n
    ang = 2.0 * np.pi * jk.astype(np.float64) / n
    return np.concatenate([np.cos(ang), -np.sin(ang)], axis=1).astype(np.float32)


def _channel_dft():
    g = FOURIER_GROUP_DIM
    jk = (np.arange(g)[:, None] * np.arange(g)[None, :]) % g
    ang = 2.0 * np.pi * jk.astype(np.float64) / g
    eye = np.eye(FOURIER_DIM // g)
    return np.concatenate([np.kron(eye, np.cos(ang)), np.kron(eye, np.sin(ang))], axis=1).astype(np.float32)


def _rope_tables(n):
    t = jnp.arange(n)
    row = (t // GRID_W).astype(F32)
    col = (t % GRID_W).astype(F32)
    inv = 1.0 / (ROPE_THETA ** (jnp.arange(ROT_FREQS, dtype=F32) / ROT_FREQS))
    ang_r = row[:, None] * inv
    ang_c = col[:, None] * inv
    cos = jnp.concatenate([jnp.cos(ang_r)] * 2 + [jnp.cos(ang_c)] * 2, axis=-1)
    sin = jnp.concatenate([-jnp.sin(ang_r), jnp.sin(ang_r), -jnp.sin(ang_c), jnp.sin(ang_c)], axis=-1)
    return jnp.tile(cos, (1, 2)), jnp.tile(sin, (1, 2))


class _Tiling(NamedTuple):
    token_rows: int
    seqs: int
    seq_rows: int
    parts: int


def _tiling(seq_len):
    if seq_len <= MIXER_ROWS:
        return _Tiling(TOKEN_ROWS, TOKEN_ROWS // seq_len, seq_len, 1)
    return _Tiling(TOKEN_ROWS, 1, MIXER_ROWS, 1)


def _run_path(x, mods, mod_row, weights, ctx, rope_tabs, dft):
    norm_g, wgu, wd, win, wout, cdft, conv_w, lam_qk, subln_g = weights
    b, n, _ = x.shape
    tm, seqs, tr, parts = _tiling(n)
    tiles_per_seq = max(n // tm, 1)
    seqs_per_tile = max(tm // n, 1)

    def mod_row_of_tile(i):
        return mod_row(i * seqs_per_tile // tiles_per_seq)

    def shape3(a):
        return a.reshape(b, n, a.shape[-1])

    ctx_k, ctx_v = ctx if ctx is not None else (None, None)
    caches = None
    for l in range(DEPTH):
        flat = x.reshape(b * n, D_MODEL)
        flat = _ffn(flat, mods, norm_g, wgu, wd, layer=l, half=0, mod_row_of_tile=mod_row_of_tile, tm=tm)
        q, kt, v, gb, u, xcs, *caches = _mixer_in(flat, mods, norm_g, win, cdft, rope_tabs, caches, layer=l,
                                                  mod_row_of_tile=mod_row_of_tile, tm=tm, seq=n)
        caches = caches or None
        x = _mixer_out(shape3(flat), shape3(q), kt, shape3(v), ctx_k, ctx_v, lam_qk, subln_g, shape3(gb),
                       shape3(u), shape3(xcs), dft, conv_w, mods, norm_g, wout, layer=l,
                       mod_row_of_batch=lambda i: mod_row(i * seqs), seqs=seqs, tr=tr, parts=parts)
        flat = _ffn(x.reshape(b * n, D_MODEL), mods, norm_g, wgu, wd, layer=l, half=1,
                    mod_row_of_tile=mod_row_of_tile, tm=tm)
        x = flat.reshape(b, n, D_MODEL)
    return x, caches


def kernel(x_prompt, x_sample, cache_k, cache_v, c, c_ctx, w_mod, b_mod, norm_g, w_ffn_gu, w_ffn_down,
           w_in, w_out, conv_w, lam_qk, subln_g):
    batch, seq, _ = x_prompt.shape
    dec_batch, dec_seq, _ = x_sample.shape
    past = cache_k.shape[2]

    cvecs = jnp.zeros((MOD_ROWS, D_MODEL), F32).at[0].set(c_ctx).at[1:1 + dec_batch].set(c)
    mods = _modulation(cvecs, w_mod, b_mod)

    wgu, wd = w_ffn_gu, w_ffn_down
    win = _cast_bf16(w_in.reshape(DEPTH * D_MODEL, IN_DIM), D_MODEL // 2).reshape(DEPTH, D_MODEL, IN_DIM)
    wout = _cast_bf16(w_out.reshape(DEPTH * D_MODEL, D_MODEL), D_MODEL).reshape(DEPTH, D_MODEL, D_MODEL)
    cdft = jnp.asarray(_channel_dft()).astype(BF16)
    weights = (norm_g, wgu, wd, win, wout, cdft, conv_w, lam_qk, subln_g)

    dft_ctx = jnp.asarray(_position_dft(seq)).astype(BF16)
    y_prompt, (kc, vc) = _run_path(x_prompt, mods, lambda i: 0, weights, None, None, dft_ctx)
    new_cache_k = kc.reshape(batch, DEPTH, N_HEADS, 2, QK_DIM, seq).transpose(0, 1, 5, 2, 3, 4)
    new_cache_v = vc.reshape(batch, DEPTH, seq, N_HEADS, V_DIM)

    dft_lat = jnp.asarray(_position_dft(dec_seq)).astype(BF16)
    ctx = (cache_k.transpose(0, 1, 3, 4, 5, 2).reshape(dec_batch, DEPTH, ATT_DIM, past),
           cache_v.reshape(dec_batch, DEPTH, past * N_HEADS, V_DIM))
    y_sample, _ = _run_path(x_sample, mods, lambda i: 1 + i, weights, ctx, _rope_tables(dec_seq), dft_lat)
    return (y_prompt, y_sample, new_cache_k, new_cache_v)
```

```python
import functools
import math
from typing import NamedTuple

import jax
import jax.numpy as jnp
import numpy as np
from jax import lax
from jax.experimental import pallas as pl
from jax.experimental.pallas import tpu as pltpu

D_MODEL = 1024
DEPTH = 2
GRID_W = 64
V_DIM = 128
QK_DIM = V_DIM // 2
N_HEADS = 4
ATT_DIM = N_HEADS * V_DIM
CONV_DIM = 256
FOURIER_DIM = 256
FOURIER_GROUP_DIM = 64
IN_DIM = 3 * ATT_DIM + 3 * CONV_DIM + FOURIER_DIM
D_FF = 2816
N_MOD = 9
ROT_FREQS = 16
ROPE_THETA = 10000.0
EPS = 1e-6
FFN_RESIDUAL_WEIGHT = 0.5
LOG2_E = math.log2(math.e)

FF_CHUNK = 256
N_FF_CHUNKS = D_FF // FF_CHUNK
FF_GROUP = 3
assert (N_FF_CHUNKS - 2) % FF_GROUP == 0
TOKEN_ROWS = 1024
MIXER_ROWS = 256
X_RING = 3
MOD_ROWS = 8
MOD_COLS = N_MOD * D_MODEL // 4
SUBLANES = 8
LANES = 128
VMEM_LIMIT = 56 * 1024 * 1024

F32 = jnp.float32
BF16 = jnp.bfloat16


def _lambda_init(layer_idx):
    return 0.8 - 0.6 * math.exp(-0.3 * layer_idx)


def _rms(x, g):
    return x * lax.rsqrt(jnp.mean(x * x, axis=-1, keepdims=True) + EPS) * g


def _dot(a, b):
    return jnp.dot(a, b, preferred_element_type=F32)


def _params(semantics):
    return pltpu.CompilerParams(dimension_semantics=semantics, vmem_limit_bytes=VMEM_LIMIT)


def _resident(block_shape, index_map):
    return pl.BlockSpec(block_shape, index_map, pipeline_mode=pl.Buffered(1))


def _cast_kernel(x_ref, o_ref):
    o_ref[...] = x_ref[...].astype(o_ref.dtype)


def _cast_bf16(x, block_rows):
    rows, cols = x.shape
    return pl.pallas_call(
        _cast_kernel,
        out_shape=jax.ShapeDtypeStruct((rows, cols), BF16),
        grid=(rows // block_rows,),
        in_specs=[pl.BlockSpec((block_rows, cols), lambda i: (i, 0))],
        out_specs=pl.BlockSpec((block_rows, cols), lambda i: (i, 0)),
        compiler_params=_params(("arbitrary",)),
        name="cast_bf16",
    )(x)


def _mod_kernel(c_ref, w_ref, b_ref, o_ref):
    c = c_ref[...]
    s = (c / (1.0 + jnp.exp(-c))).astype(BF16)
    o_ref[...] = _dot(s, w_ref[...].astype(BF16)) + b_ref[...]


def _modulation(cvecs, w_mod, b_mod):
    col = MOD_COLS
    out = pl.pallas_call(
        _mod_kernel,
        out_shape=jax.ShapeDtypeStruct((DEPTH, MOD_ROWS, N_MOD * D_MODEL), F32),
        grid=(DEPTH, N_MOD * D_MODEL // col),
        in_specs=[
            pl.BlockSpec((MOD_ROWS, D_MODEL), lambda l, j: (0, 0)),
            pl.BlockSpec((None, D_MODEL, col), lambda l, j: (l, 0, j)),
            pl.BlockSpec((None, 1, col), lambda l, j: (l, 0, j)),
        ],
        out_specs=pl.BlockSpec((None, MOD_ROWS, col), lambda l, j: (l, 0, j)),
        compiler_params=_params(("arbitrary", "arbitrary")),
        name="modulation",
    )(cvecs, w_mod, b_mod.reshape(DEPTH, 1, N_MOD * D_MODEL))
    return out.reshape(DEPTH, MOD_ROWS, N_MOD, D_MODEL)


def _ffn_kernel(x_ref, mod_ref, g_ref, wg_ref, wu_ref, wdn_ref, o_ref, acc_ref, hb_ref, wgu_sc, wd_sc,
                *, sub, g_row):
    step = pl.program_id(0)
    last = N_FF_CHUNKS - 1
    shift = mod_ref[3 * sub:3 * sub + 1, :]
    scale = mod_ref[3 * sub + 1:3 * sub + 2, :]
    gate = mod_ref[3 * sub + 2:3 * sub + 3, :]

    def pre_norm(rows=slice(None)):
        h = _rms(x_ref[rows, :], g_ref[g_row:g_row + 1, :] * (1.0 + scale)) + shift
        return h.astype(BF16)

    def hidden(hb, c):
        gu = _dot(hb, wgu_sc[c])
        g = gu[:, :FF_CHUNK]
        u = gu[:, FF_CHUNK:]
        return (g / (1.0 + jnp.exp(-g)) * u).astype(BF16)

    def finish(y):
        o_ref[...] = x_ref[...] + _rms(y, g_ref[g_row + 1:g_row + 2, :] * (FFN_RESIDUAL_WEIGHT * gate))

    @pl.when(step <= last)
    def _():
        @pl.when(step == 0)
        def _():
            hb_ref[...] = pre_norm()
            acc_ref[...] = jnp.zeros_like(acc_ref)

        wgu_sc[step, :, 0:FF_CHUNK] = wg_ref[...].astype(BF16)
        wgu_sc[step, :, FF_CHUNK:2 * FF_CHUNK] = wu_ref[...].astype(BF16)
        wd_sc[step] = wdn_ref[...].astype(BF16)
        acc_ref[...] += _dot(hidden(hb_ref[...], step), wd_sc[step])

        @pl.when(step == last)
        def _():
            finish(acc_ref[...])

    @pl.when(step > last)
    def _():
        rows = x_ref.shape[0] // 2
        halves = [pre_norm(slice(r * rows, (r + 1) * rows)) for r in range(2)]
        for r, hb_half in enumerate(halves):
            acc_ref[r * rows:(r + 1) * rows, :] = _dot(hidden(hb_half, 0), wd_sc[0])
        hb = jnp.concatenate(halves, axis=0)
        for c0 in range(1, last, FF_GROUP):
            a = jnp.concatenate([hidden(hb, c0 + k) for k in range(FF_GROUP)], axis=-1)
            wd = wd_sc[c0:c0 + FF_GROUP].reshape(FF_GROUP * FF_CHUNK, D_MODEL)
            acc_ref[...] += _dot(a, wd)
        finish(acc_ref[...] + _dot(hidden(hb, last), wd_sc[last]))


def _ffn(x, mods, norm_g, w_gu, w_down, *, layer, half, mod_row_of_tile, tm):
    n = x.shape[0]
    sub = 2 * half
    g_row = 4 * half
    last = N_FF_CHUNKS - 1

    def tile(s):
        return jnp.maximum(s - last, 0)

    def chunk(s):
        return jnp.minimum(s, last)

    return pl.pallas_call(
        functools.partial(_ffn_kernel, sub=sub, g_row=g_row),
        out_shape=jax.ShapeDtypeStruct((n, D_MODEL), F32),
        grid=(n // tm + last,),
        in_specs=[
            pl.BlockSpec((tm, D_MODEL), lambda s: (tile(s), 0)),
            pl.BlockSpec((None, None, N_MOD, D_MODEL), lambda s: (layer, mod_row_of_tile(tile(s)), 0, 0)),
            pl.BlockSpec((None, 6, D_MODEL), lambda s: (layer, 0, 0)),
            pl.BlockSpec((None, None, D_MODEL, FF_CHUNK), lambda s: (layer, half, 0, chunk(s))),
            pl.BlockSpec((None, None, D_MODEL, FF_CHUNK), lambda s: (layer, half, 0, N_FF_CHUNKS + chunk(s))),
            pl.BlockSpec((None, None, FF_CHUNK, D_MODEL), lambda s: (layer, half, chunk(s), 0)),
        ],
        out_specs=pl.BlockSpec((tm, D_MODEL), lambda s: (tile(s), 0)),
        scratch_shapes=[
            pltpu.VMEM((tm, D_MODEL), F32),
            pltpu.VMEM((tm, D_MODEL), BF16),
            pltpu.VMEM((N_FF_CHUNKS, D_MODEL, 2 * FF_CHUNK), BF16),
            pltpu.VMEM((N_FF_CHUNKS, FF_CHUNK, D_MODEL), BF16),
        ],
        compiler_params=_params(("arbitrary",)),
        name="ffn",
    )(x, mods, norm_g, w_gu, w_gu, w_down)


def _rope(x, cos, sin_signed, lo_half):
    partner = jnp.where(lo_half, pltpu.roll(x, LANES - ROT_FREQS, 1), pltpu.roll(x, ROT_FREQS, 1))
    return x * cos + partner * sin_signed


def _mixer_in_kernel(*refs, rotary, seq, cache_slot):
    x_ref, mod_ref, g_ref, win_ref, cdft_ref = refs[:5]
    if rotary:
        cos_ref, sin_ref = refs[5:7]
        q_ref, kt_ref, v_ref, gb_ref, u_ref, xcs_ref = refs[7:]
    else:
        q_ref, kt_ref, v_ref, gb_ref, u_ref, xcs_ref, kc_ref, vc_ref = refs[-8:]
    x = x_ref[...]
    shift = mod_ref[3:4, :]
    scale = mod_ref[4:5, :]
    gain = g_ref[2:3, :] * (1.0 + scale)
    rows = x.shape[0] // 2
    halves = [(_rms(x[r * rows:(r + 1) * rows, :], gain) + shift).astype(BF16) for r in range(2)]
    hb = jnp.concatenate(halves, axis=0)

    q = jnp.concatenate([_dot(hb_half, win_ref[:, 0:ATT_DIM]) for hb_half in halves], axis=0)
    q = q * (QK_DIM ** -0.5 * LOG2_E)
    k = _dot(hb, win_ref[:, ATT_DIM:2 * ATT_DIM])
    v = _dot(hb, win_ref[:, 2 * ATT_DIM:3 * ATT_DIM])
    v_ref[...] = v.astype(v_ref.dtype)
    if rotary:
        cos = cos_ref[...]
        sin = sin_ref[...]
        lane = lax.broadcasted_iota(jnp.int32, (1, LANES), 1)
        lo_half = (lane % (2 * ROT_FREQS)) < ROT_FREQS
        for hd in range(N_HEADS):
            sl = slice(hd * V_DIM, (hd + 1) * V_DIM)
            q_ref[:, sl] = _rope(q[:, sl], cos, sin, lo_half).astype(q_ref.dtype)
            kt_ref[sl, :] = _rope(k[:, sl], cos, sin, lo_half).T.astype(kt_ref.dtype)
    else:
        q_ref[...] = q.astype(q_ref.dtype)
        for s in range(x.shape[0] // seq):
            rows = slice(s * seq, (s + 1) * seq)
            kt = k[rows, :].T
            kt_ref[s] = kt.astype(kt_ref.dtype)
            if cache_slot is None:
                kc_out, vc_out = kc_ref.at[s], vc_ref.at[s]
            else:
                kc_out, vc_out = kc_ref.at[s, cache_slot], vc_ref.at[s, cache_slot]
                for other in range(DEPTH):
                    if other != cache_slot:
                        kc_ref[s, other] = jnp.zeros(kc_ref.shape[2:], F32)
                        vc_ref[s, other] = jnp.zeros(vc_ref.shape[2:], F32)
            kc_out[...] = kt
            for hd in range(N_HEADS):
                vc_out[pl.ds(hd, seq, stride=N_HEADS), :] = v[rows, hd * V_DIM:(hd + 1) * V_DIM]

    c0 = 3 * ATT_DIM
    gates = _dot(hb, win_ref[:, c0:c0 + 3 * CONV_DIM])
    gb_ref[...] = gates[:, 0:CONV_DIM]
    u_ref[...] = gates[:, CONV_DIM:2 * CONV_DIM] * gates[:, 2 * CONV_DIM:3 * CONV_DIM]
    f = _dot(hb, win_ref[:, c0 + 3 * CONV_DIM:IN_DIM])
    xcs_ref[...] = _dot(f.astype(BF16), cdft_ref[...]).astype(BF16)


def _mixer_in(x, mods, norm_g, win, cdft, rope_tabs, caches, *, layer, mod_row_of_tile, tm, seq):
    n = x.shape[0]
    rotary = rope_tabs is not None
    in_specs = [
        pl.BlockSpec((tm, D_MODEL), lambda i: (i, 0)),
        pl.BlockSpec((None, None, N_MOD, D_MODEL), lambda i: (layer, mod_row_of_tile(i), 0, 0)),
        pl.BlockSpec((None, 6, D_MODEL), lambda i: (layer, 0, 0)),
        _resident((None, D_MODEL, IN_DIM), lambda i: (layer, 0, 0)),
        _resident((FOURIER_DIM, 2 * FOURIER_DIM), lambda i: (0, 0)),
    ]
    args = [x, mods, norm_g, win, cdft]
    if rotary:
        tiles_per_seq = rope_tabs[0].shape[0] // tm
        for tab in rope_tabs:
            in_specs.append(pl.BlockSpec((tm, LANES), lambda i: (i % tiles_per_seq, 0)))
            args.append(tab)

    def rows(width):
        return pl.BlockSpec((tm, width), lambda i: (i, 0))

    batch = n // seq
    if tm <= seq:
        tiles_per_seq = seq // tm
        kt_spec = pl.BlockSpec((None, ATT_DIM, tm), lambda i: (i // tiles_per_seq, 0, i % tiles_per_seq))
    else:
        kt_spec = pl.BlockSpec((tm // seq, ATT_DIM, seq), lambda i: (i, 0, 0))
    out_shape = [
        jax.ShapeDtypeStruct((n, ATT_DIM), BF16),
        jax.ShapeDtypeStruct((batch, ATT_DIM, seq), BF16),
        jax.ShapeDtypeStruct((n, ATT_DIM), BF16),
        jax.ShapeDtypeStruct((n, CONV_DIM), F32),
        jax.ShapeDtypeStruct((n, CONV_DIM), F32),
        jax.ShapeDtypeStruct((n, 2 * FOURIER_DIM), BF16),
    ]
    out_specs = [rows(ATT_DIM), kt_spec, rows(ATT_DIM), rows(CONV_DIM), rows(CONV_DIM),
                 rows(2 * FOURIER_DIM)]
    aliases = {}
    if not rotary:
        seqs = tm // seq
        out_shape += [jax.ShapeDtypeStruct((batch, DEPTH, ATT_DIM, seq), F32),
                      jax.ShapeDtypeStruct((batch, DEPTH, N_HEADS * seq, V_DIM), F32)]
        if caches is None:
            out_specs += [pl.BlockSpec((seqs, DEPTH, ATT_DIM, seq), lambda i: (i, 0, 0, 0)),
                          pl.BlockSpec((seqs, DEPTH, N_HEADS * seq, V_DIM), lambda i: (i, 0, 0, 0))]
        else:
            out_specs += [pl.BlockSpec((seqs, None, ATT_DIM, seq), lambda i: (i, layer, 0, 0)),
                          pl.BlockSpec((seqs, None, N_HEADS * seq, V_DIM), lambda i: (i, layer, 0, 0))]
            first_cache_out = len(out_shape) - 2
            for offset, arr in enumerate(caches):
                aliases[len(args)] = first_cache_out + offset
                in_specs.append(pl.BlockSpec(memory_space=pl.ANY))
                args.append(arr)

    cache_slot = layer if (not rotary and caches is None) else None
    return pl.pallas_call(
        functools.partial(_mixer_in_kernel, rotary=rotary, seq=seq, cache_slot=cache_slot),
        out_shape=out_shape,
        grid=(n // tm,),
        in_specs=in_specs,
        out_specs=out_specs,
        input_output_aliases=aliases,
        compiler_params=_params(("arbitrary",)),
        name="mixer_in",
    )(*args)


def _lambda(lam_ref, lambda_init):
    lq = lam_ref[...]
    s01 = jnp.sum(lq[0:1, :] * lq[1:2, :], axis=-1, keepdims=True)
    s23 = jnp.sum(lq[2:3, :] * lq[3:4, :], axis=-1, keepdims=True)
    return jnp.exp(s01) - jnp.exp(s23) + lambda_init


def _stage_values(vext_sc, v_ref, cv_ref, nb, n):
    for b in range(nb):
        for hd in range(N_HEADS):
            sl = slice(hd * V_DIM, (hd + 1) * V_DIM)
            vext_sc[b, hd, 0:n, 0:V_DIM] = v_ref[b, :, sl]
            if cv_ref is not None:
                past = vext_sc.shape[2] - n
                vext_sc[b, hd, n:, 0:V_DIM] = cv_ref[pl.ds(hd, past, stride=N_HEADS), :].astype(BF16)
            vext_sc[b, hd, :, V_DIM:] = jnp.ones((vext_sc.shape[2], V_DIM), BF16)


def _attend(q_ref, kt_sc, vext_sc, o_ref, lam, sub_g, nb):
    lane = lax.broadcasted_iota(jnp.int32, (1, V_DIM), 1)
    first_map = lane < QK_DIM

    def exponentials(b, hd):
        sl = slice(hd * V_DIM, (hd + 1) * V_DIM)
        q = q_ref[b, :, sl]
        zero = jnp.zeros_like(q)
        out = []
        for keep in (first_map, jnp.logical_not(first_map)):
            qm = jnp.where(keep, q, zero)
            s = _dot(qm, kt_sc[b, sl, :])
            m = jnp.max(s, axis=-1, keepdims=True)
            out.append(jnp.exp2(s - m).astype(BF16))
        return out

    def finish(b, hd, e_maps):
        sl = slice(hd * V_DIM, (hd + 1) * V_DIM)
        vext = vext_sc[b, hd]
        r0, r1 = [_dot(e, vext) for e in e_maps]
        o = r0[:, :V_DIM] / r0[:, V_DIM:] - lam * (r1[:, :V_DIM] / r1[:, V_DIM:])
        o_ref[b, :, sl] = (_rms(o, sub_g)).astype(o_ref.dtype)

    items = [(b, hd) for b in range(nb) for hd in range(N_HEADS)]
    pending = exponentials(*items[0])
    for idx, item in enumerate(items):
        upcoming = exponentials(*items[idx + 1]) if idx + 1 < len(items) else None
        finish(*item, pending)
        pending = upcoming


def _mixer_out_kernel(*refs, seqs, tr, n, parts, has_ctx, lambda_init, dft_resident, x_ring):
    if x_ring:
        *refs, ring, sems = refs
        x_hbm = refs[0]
        step, n_steps = pl.program_id(0), pl.num_programs(0)

        def x_copy(block, slot):
            return pltpu.make_async_copy(x_hbm.at[pl.ds(block * seqs, seqs)], ring.at[slot], sems.at[slot])

        @pl.when(step == 0)
        def _():
            for first in range(min(X_RING - 1, pl.cdiv(x_hbm.shape[0], seqs))):
                x_copy(first, first).start()

        ahead = step + (X_RING - 1)

        @pl.when(ahead < n_steps)
        def _():
            x_copy(ahead, ahead % X_RING).start()

        slot = step % X_RING
        x_copy(step, slot).wait()
        refs = [ring.at[slot]] + refs[1:]
    if has_ctx:
        (x_ref, q_ref, kt_ref, v_ref, ckt_ref, cv_ref, lam_ref, sg_ref, gb_ref, u_ref, xcs_ref, dft_ref, cw_ref,
         mod_ref, g_ref, wout_ref, o_ref, att_ref, vext_sc, kt_sc) = refs
    else:
        (x_ref, q_ref, kt_ref, v_ref, lam_ref, sg_ref, gb_ref, u_ref, xcs_ref, dft_ref, cw_ref,
         mod_ref, g_ref, wout_ref, o_ref, att_ref, vext_sc) = refs
        kt_sc, cv_ref = kt_ref, None

    @pl.when(pl.program_id(1) == 0)
    def _():
        if has_ctx:
            for b in range(seqs):
                kt_sc[b, :, 0:n] = kt_ref[b]
                kt_sc[b, :, n:] = ckt_ref[...].astype(BF16)
        _stage_values(vext_sc, v_ref, cv_ref, seqs, n)

    sub = tr // parts
    gate_g = g_ref[3:4, :] * mod_ref[5:6, :]
    cw = cw_ref[...]
    row = lax.broadcasted_iota(jnp.int32, (sub, 1), 0)
    four_scale = (n * FOURIER_GROUP_DIM) ** -0.5

    def mix(s, t):
        rows = slice(t * sub, (t + 1) * sub)
        r0 = pl.multiple_of(pl.program_id(1) * tr + t * sub, sub)
        dft_rows = pl.ds(r0, sub) if dft_resident else rows
        four = (_dot(dft_ref[dft_rows, 0:n], xcs_ref[s, :, 0:FOURIER_DIM])
                + _dot(dft_ref[dft_rows, n:2 * n], xcs_ref[s, :, FOURIER_DIM:2 * FOURIER_DIM])) * four_scale
        u = u_ref[s, pl.ds(r0, sub), :]
        before_start = pl.multiple_of(jnp.maximum(r0 - SUBLANES, 0), SUBLANES)
        after_start = pl.multiple_of(jnp.minimum(r0 + sub, n - SUBLANES), SUBLANES)
        before = u_ref[s, pl.ds(before_start, SUBLANES), :][SUBLANES - 1:SUBLANES, :]
        after = u_ref[s, pl.ds(after_start, SUBLANES), :][0:1, :]
        before = jnp.where(r0 > 0, before, 0.0)
        after = jnp.where(r0 + sub < n, after, 0.0)
        prev = jnp.where(row == 0, before, pltpu.roll(u, 1, 0))
        nxt = jnp.where(row == sub - 1, after, pltpu.roll(u, sub - 1, 0))
        conv = gb_ref[s, rows, :] * (prev * cw[0:1, :] + u * cw[1:2, :] + nxt * cw[2:3, :])
        return conv.astype(BF16), four.astype(BF16)

    def project(s, t, conv, four):
        rows = slice(t * sub, (t + 1) * sub)
        y = (_dot(att_ref[s, rows, :], wout_ref[0:ATT_DIM, :])
             + _dot(conv, wout_ref[ATT_DIM:ATT_DIM + CONV_DIM, :])
             + _dot(four, wout_ref[ATT_DIM + CONV_DIM:D_MODEL, :]))
        o_ref[s, rows, :] = x_ref[s, rows, :] + _rms(y, gate_g)

    items = [(s, t) for s in range(seqs) for t in range(parts)]
    attend = functools.partial(_attend, q_ref, kt_sc, vext_sc, att_ref, _lambda(lam_ref, lambda_init),
                               sg_ref[...] * (1.0 - lambda_init), seqs)
    if len(items) == 1:
        pending = mix(*items[0])
        attend()
    else:
        attend()
        pending = mix(*items[0])
    for idx, item in enumerate(items):
        upcoming = mix(*items[idx + 1]) if idx + 1 < len(items) else None
        project(*item, *pending)
        pending = upcoming


def _mixer_out(x, q, kt, v, ctx_kt, ctx_v, lam_qk, subln_g, gb, u, xcs, dft, conv_w, mods, norm_g, wout, *,
               layer, mod_row_of_batch, seqs, tr, parts):
    b, n, _ = x.shape
    has_ctx = ctx_kt is not None
    assert not has_ctx or seqs == 1
    dft_resident = tr == n
    x_ring = tr == n

    def tile(width):
        return pl.BlockSpec((seqs, tr, width), lambda i, j: (i, j, 0))

    def whole(width):
        return pl.BlockSpec((seqs, n, width), lambda i, j: (i, 0, 0))

    x_spec = pl.BlockSpec(memory_space=pl.ANY) if x_ring else tile(D_MODEL)
    in_specs = [x_spec, tile(ATT_DIM), pl.BlockSpec((seqs, ATT_DIM, n), lambda i, j: (i, 0, 0)), whole(ATT_DIM)]
    args = [x, q, kt, v]
    if has_ctx:
        for arr in (ctx_kt, ctx_v):
            in_specs.append(pl.BlockSpec((None, None) + arr.shape[2:], lambda i, j: (i, layer, 0, 0)))
            args.append(arr)
    in_specs += [
        pl.BlockSpec((None, 4, QK_DIM), lambda i, j: (layer, 0, 0)),
        pl.BlockSpec((None, 1, V_DIM), lambda i, j: (layer, 0, 0)),
        tile(CONV_DIM), whole(CONV_DIM), whole(2 * FOURIER_DIM),
        _resident((n, 2 * n), lambda i, j: (0, 0)) if dft_resident
        else pl.BlockSpec((tr, 2 * n), lambda i, j: (j, 0)),
        pl.BlockSpec((None, 3, CONV_DIM), lambda i, j: (layer, 0, 0)),
        pl.BlockSpec((None, None, N_MOD, D_MODEL), lambda i, j: (layer, mod_row_of_batch(i), 0, 0)),
        pl.BlockSpec((None, 6, D_MODEL), lambda i, j: (layer, 0, 0)),
        _resident((None, D_MODEL, D_MODEL), lambda i, j: (layer, 0, 0)),
    ]
    args += [lam_qk, subln_g.reshape(DEPTH, 1, V_DIM), gb, u, xcs, dft, conv_w, mods, norm_g, wout]
    n_keys = n + (ctx_kt.shape[3] if has_ctx else 0)
    scratch = [pltpu.VMEM((seqs, tr, ATT_DIM), BF16), pltpu.VMEM((seqs, N_HEADS, n_keys, 2 * V_DIM), BF16)]
    if has_ctx:
        scratch.append(pltpu.VMEM((seqs, ATT_DIM, n_keys), BF16))
    if x_ring:
        scratch += [pltpu.VMEM((X_RING, seqs, tr, D_MODEL), F32), pltpu.SemaphoreType.DMA((X_RING,))]
    return pl.pallas_call(
        functools.partial(_mixer_out_kernel, seqs=seqs, tr=tr, n=n, parts=parts, has_ctx=has_ctx,
                          lambda_init=_lambda_init(layer), dft_resident=dft_resident, x_ring=x_ring),
        out_shape=jax.ShapeDtypeStruct((b, n, D_MODEL), F32),
        grid=(b // seqs, n // tr),
        in_specs=in_specs,
        out_specs=tile(D_MODEL),
        scratch_shapes=scratch,
        compiler_params=_params(("arbitrary", "arbitrary")),
        name="mixer_out",
    )(*args)


def _position_dft(n):
    jk = (np.arange(n, dtype=np.int64)[:, None] * np.arange(n, dtype=np.int64)[None, :]) % n
    ang = 2.0 * np.pi * jk.astype(np.float64) / n
    return np.concatenate([np.cos(ang), -np.sin(ang)], axis=1).astype(np.float32)


def _channel_dft():
    g = FOURIER_GROUP_DIM
    jk = (np.arange(g)[:, None] * np.arange(g)[None, :]) % g
    ang = 2.0 * np.pi * jk.astype(np.float64) / g
    eye = np.eye(FOURIER_DIM // g)
    return np.concatenate([np.kron(eye, np.cos(ang)), np.kron(eye, np.sin(ang))], axis=1).astype(np.float32)


def _rope_tables(n):
    t = jnp.arange(n)
    row = (t // GRID_W).astype(F32)
    col = (t % GRID_W).astype(F32)
    inv = 1.0 / (ROPE_THETA ** (jnp.arange(ROT_FREQS, dtype=F32) / ROT_FREQS))
    ang_r = row[:, None] * inv
    ang_c = col[:, None] * inv
    cos = jnp.concatenate([jnp.cos(ang_r)] * 2 + [jnp.cos(ang_c)] * 2, axis=-1)
    sin = jnp.concatenate([-jnp.sin(ang_r), jnp.sin(ang_r), -jnp.sin(ang_c), jnp.sin(ang_c)], axis=-1)
    return jnp.tile(cos, (1, 2)), jnp.tile(sin, (1, 2))


class _Tiling(NamedTuple):
    token_rows: int
    seqs: int
    seq_rows: int
    parts: int


def _tiling(seq_len):
    if seq_len <= MIXER_ROWS:
        return _Tiling(TOKEN_ROWS, TOKEN_ROWS // seq_len, seq_len, 1)
    return _Tiling(TOKEN_ROWS, 1, MIXER_ROWS, 1)


def _run_path(x, mods, mod_row, weights, ctx, rope_tabs, dft):
    norm_g, wgu, wd, win, wout, cdft, conv_w, lam_qk, subln_g = weights
    b, n, _ = x.shape
    tm, seqs, tr, parts = _tiling(n)
    tiles_per_seq = max(n // tm, 1)
    seqs_per_tile = max(tm // n, 1)

    def mod_row_of_tile(i):
        return mod_row(i * seqs_per_tile // tiles_per_seq)

    def shape3(a):
        return a.reshape(b, n, a.shape[-1])

    ctx_k, ctx_v = ctx if ctx is not None else (None, None)
    caches = None
    for l in range(DEPTH):
        flat = x.reshape(b * n, D_MODEL)
        flat = _ffn(flat, mods, norm_g, wgu, wd, layer=l, half=0, mod_row_of_tile=mod_row_of_tile, tm=tm)
        q, kt, v, gb, u, xcs, *caches = _mixer_in(flat, mods, norm_g, win, cdft, rope_tabs, caches, layer=l,
                                                  mod_row_of_tile=mod_row_of_tile, tm=tm, seq=n)
        caches = caches or None
        x = _mixer_out(shape3(flat), shape3(q), kt, shape3(v), ctx_k, ctx_v, lam_qk, subln_g, shape3(gb),
                       shape3(u), shape3(xcs), dft, conv_w, mods, norm_g, wout, layer=l,
                       mod_row_of_batch=lambda i: mod_row(i * seqs), seqs=seqs, tr=tr, parts=parts)
        flat = _ffn(x.reshape(b * n, D_MODEL), mods, norm_g, wgu, wd, layer=l, half=1,
                    mod_row_of_tile=mod_row_of_tile, tm=tm)
        x = flat.reshape(b, n, D_MODEL)
    return x, caches


def kernel(x_prompt, x_sample, cache_k, cache_v, c, c_ctx, w_mod, b_mod, norm_g, w_ffn_gu, w_ffn_down,
           w_in, w_out, conv_w, lam_qk, subln_g):
    batch, seq, _ = x_prompt.shape
    dec_batch, dec_seq, _ = x_sample.shape
    past = cache_k.shape[2]

    cvecs = jnp.zeros((MOD_ROWS, D_MODEL), F32).at[0].set(c_ctx).at[1:1 + dec_batch].set(c)
    mods = _modulation(cvecs, w_mod, b_mod)

    wgu, wd = w_ffn_gu, w_ffn_down
    win = _cast_bf16(w_in.reshape(DEPTH * D_MODEL, IN_DIM), D_MODEL // 2).reshape(DEPTH, D_MODEL, IN_DIM)
    wout = _cast_bf16(w_out.reshape(DEPTH * D_MODEL, D_MODEL), D_MODEL).reshape(DEPTH, D_MODEL, D_MODEL)
    cdft = jnp.asarray(_channel_dft()).astype(BF16)
    weights = (norm_g, wgu, wd, win, wout, cdft, conv_w, lam_qk, subln_g)

    dft_ctx = jnp.asarray(_position_dft(seq)).astype(BF16)
    y_prompt, (kc, vc) = _run_path(x_prompt, mods, lambda i: 0, weights, None, None, dft_ctx)
    new_cache_k = kc.reshape(batch, DEPTH, N_HEADS, 2, QK_DIM, seq).transpose(0, 1, 5, 2, 3, 4)
    new_cache_v = vc.reshape(batch, DEPTH, seq, N_HEADS, V_DIM)

    dft_lat = jnp.asarray(_position_dft(dec_seq)).astype(BF16)
    ctx = (cache_k.transpose(0, 1, 3, 4, 5, 2).reshape(dec_batch, DEPTH, ATT_DIM, past),
           cache_v.reshape(dec_batch, DEPTH, past * N_HEADS, V_DIM))
    y_sample, _ = _run_path(x_sample, mods, lambda i: 1 + i, weights, ctx, _rope_tables(dec_seq), dft_lat)
    return (y_prompt, y_sample, new_cache_k, new_cache_v)
```

```python
import functools
import math
from typing import NamedTuple

import jax
import jax.numpy as jnp
import numpy as np
from jax import lax
from jax.experimental import pallas as pl
from jax.experimental.pallas import tpu as pltpu

D_MODEL = 1024
DEPTH = 2
GRID_W = 64
V_DIM = 128
QK_DIM = V_DIM // 2
N_HEADS = 4
ATT_DIM = N_HEADS * V_DIM
CONV_DIM = 256
FOURIER_DIM = 256
FOURIER_GROUP_DIM = 64
IN_DIM = 3 * ATT_DIM + 3 * CONV_DIM + FOURIER_DIM
D_FF = 2816
N_MOD = 9
ROT_FREQS = 16
ROPE_THETA = 10000.0
EPS = 1e-6
FFN_RESIDUAL_WEIGHT = 0.5
LOG2_E = math.log2(math.e)

FF_CHUNK = 256
N_FF_CHUNKS = D_FF // FF_CHUNK
FF_GROUP = 3
assert (N_FF_CHUNKS - 2) % FF_GROUP == 0
TOKEN_ROWS = 1024
MIXER_ROWS = 256
MOD_ROWS = 8
MOD_COLS = N_MOD * D_MODEL // 4
SUBLANES = 8
LANES = 128
VMEM_LIMIT = 56 * 1024 * 1024

F32 = jnp.float32
BF16 = jnp.bfloat16


def _lambda_init(layer_idx):
    return 0.8 - 0.6 * math.exp(-0.3 * layer_idx)


def _rms(x, g):
    return x * lax.rsqrt(jnp.mean(x * x, axis=-1, keepdims=True) + EPS) * g


def _dot(a, b):
    return jnp.dot(a, b, preferred_element_type=F32)


def _params(semantics):
    return pltpu.CompilerParams(dimension_semantics=semantics, vmem_limit_bytes=VMEM_LIMIT)


def _resident(block_shape, index_map):
    return pl.BlockSpec(block_shape, index_map, pipeline_mode=pl.Buffered(1))


def _cast_kernel(x_ref, o_ref):
    o_ref[...] = x_ref[...].astype(o_ref.dtype)


def _cast_bf16(x, block_rows):
    rows, cols = x.shape
    return pl.pallas_call(
        _cast_kernel,
        out_shape=jax.ShapeDtypeStruct((rows, cols), BF16),
        grid=(rows // block_rows,),
        in_specs=[pl.BlockSpec((block_rows, cols), lambda i: (i, 0))],
        out_specs=pl.BlockSpec((block_rows, cols), lambda i: (i, 0)),
        compiler_params=_params(("arbitrary",)),
        name="cast_bf16",
    )(x)


def _mod_kernel(c_ref, w_ref, b_ref, o_ref):
    c = c_ref[...]
    s = (c / (1.0 + jnp.exp(-c))).astype(BF16)
    o_ref[...] = _dot(s, w_ref[...].astype(BF16)) + b_ref[...]


def _modulation(cvecs, w_mod, b_mod):
    col = MOD_COLS
    out = pl.pallas_call(
        _mod_kernel,
        out_shape=jax.ShapeDtypeStruct((DEPTH, MOD_ROWS, N_MOD * D_MODEL), F32),
        grid=(DEPTH, N_MOD * D_MODEL // col),
        in_specs=[
            pl.BlockSpec((MOD_ROWS, D_MODEL), lambda l, j: (0, 0)),
            pl.BlockSpec((None, D_MODEL, col), lambda l, j: (l, 0, j)),
            pl.BlockSpec((None, 1, col), lambda l, j: (l, 0, j)),
        ],
        out_specs=pl.BlockSpec((None, MOD_ROWS, col), lambda l, j: (l, 0, j)),
        compiler_params=_params(("arbitrary", "arbitrary")),
        name="modulation",
    )(cvecs, w_mod, b_mod.reshape(DEPTH, 1, N_MOD * D_MODEL))
    return out.reshape(DEPTH, MOD_ROWS, N_MOD, D_MODEL)


def _ffn_kernel(x_ref, mod_ref, g_ref, wg_ref, wu_ref, wdn_ref, o_ref, acc_ref, hb_ref, wgu_sc, wd_sc,
                *, sub, g_row):
    step = pl.program_id(0)
    last = N_FF_CHUNKS - 1
    shift = mod_ref[3 * sub:3 * sub + 1, :]
    scale = mod_ref[3 * sub + 1:3 * sub + 2, :]
    gate = mod_ref[3 * sub + 2:3 * sub + 3, :]

    def pre_norm(rows=slice(None)):
        h = _rms(x_ref[rows, :], g_ref[g_row:g_row + 1, :] * (1.0 + scale)) + shift
        return h.astype(BF16)

    def hidden(hb, c):
        gu = _dot(hb, wgu_sc[c])
        g = gu[:, :FF_CHUNK]
        u = gu[:, FF_CHUNK:]
        return (g / (1.0 + jnp.exp(-g)) * u).astype(BF16)

    def finish(y):
        o_ref[...] = x_ref[...] + _rms(y, g_ref[g_row + 1:g_row + 2, :] * (FFN_RESIDUAL_WEIGHT * gate))

    @pl.when(step <= last)
    def _():
        @pl.when(step == 0)
        def _():
            hb_ref[...] = pre_norm()
            acc_ref[...] = jnp.zeros_like(acc_ref)

        wgu_sc[step, :, 0:FF_CHUNK] = wg_ref[...].astype(BF16)
        wgu_sc[step, :, FF_CHUNK:2 * FF_CHUNK] = wu_ref[...].astype(BF16)
        wd_sc[step] = wdn_ref[...].astype(BF16)
        acc_ref[...] += _dot(hidden(hb_ref[...], step), wd_sc[step])

        @pl.when(step == last)
        def _():
            finish(acc_ref[...])

    @pl.when(step > last)
    def _():
        rows = x_ref.shape[0] // 2
        halves = [pre_norm(slice(r * rows, (r + 1) * rows)) for r in range(2)]
        for r, hb_half in enumerate(halves):
            acc_ref[r * rows:(r + 1) * rows, :] = _dot(hidden(hb_half, 0), wd_sc[0])
        hb = jnp.concatenate(halves, axis=0)
        for c0 in range(1, last, FF_GROUP):
            a = jnp.concatenate([hidden(hb, c0 + k) for k in range(FF_GROUP)], axis=-1)
            wd = wd_sc[c0:c0 + FF_GROUP].reshape(FF_GROUP * FF_CHUNK, D_MODEL)
            acc_ref[...] += _dot(a, wd)
        a_last = hidden(hb, last)
        post_gain = g_ref[g_row + 1:g_row + 2, :] * (FFN_RESIDUAL_WEIGHT * gate)
        for r in range(2):
            half = slice(r * rows, (r + 1) * rows)
            y = acc_ref[half, :] + _dot(a_last[half, :], wd_sc[last])
            o_ref[half, :] = x_ref[half, :] + _rms(y, post_gain)


def _ffn(x, mods, norm_g, w_gu, w_down, *, layer, half, mod_row_of_tile, tm):
    n = x.shape[0]
    sub = 2 * half
    g_row = 4 * half
    last = N_FF_CHUNKS - 1

    def tile(s):
        return jnp.maximum(s - last, 0)

    def chunk(s):
        return jnp.minimum(s, last)

    return pl.pallas_call(
        functools.partial(_ffn_kernel, sub=sub, g_row=g_row),
        out_shape=jax.ShapeDtypeStruct((n, D_MODEL), F32),
        grid=(n // tm + last,),
        in_specs=[
            pl.BlockSpec((tm, D_MODEL), lambda s: (tile(s), 0)),
            pl.BlockSpec((None, None, N_MOD, D_MODEL), lambda s: (layer, mod_row_of_tile(tile(s)), 0, 0)),
            pl.BlockSpec((None, 6, D_MODEL), lambda s: (layer, 0, 0)),
            pl.BlockSpec((None, None, D_MODEL, FF_CHUNK), lambda s: (layer, half, 0, chunk(s))),
            pl.BlockSpec((None, None, D_MODEL, FF_CHUNK), lambda s: (layer, half, 0, N_FF_CHUNKS + chunk(s))),
            pl.BlockSpec((None, None, FF_CHUNK, D_MODEL), lambda s: (layer, half, chunk(s), 0)),
        ],
        out_specs=pl.BlockSpec((tm, D_MODEL), lambda s: (tile(s), 0)),
        scratch_shapes=[
            pltpu.VMEM((tm, D_MODEL), F32),
            pltpu.VMEM((tm, D_MODEL), BF16),
            pltpu.VMEM((N_FF_CHUNKS, D_MODEL, 2 * FF_CHUNK), BF16),
            pltpu.VMEM((N_FF_CHUNKS, FF_CHUNK, D_MODEL), BF16),
        ],
        compiler_params=_params(("arbitrary",)),
        name="ffn",
    )(x, mods, norm_g, w_gu, w_gu, w_down)


def _rope(x, cos, sin_signed, lo_half):
    partner = jnp.where(lo_half, pltpu.roll(x, LANES - ROT_FREQS, 1), pltpu.roll(x, ROT_FREQS, 1))
    return x * cos + partner * sin_signed


def _mixer_in_kernel(*refs, rotary, seq, cache_slot):
    x_ref, mod_ref, g_ref, win_ref, cdft_ref = refs[:5]
    if rotary:
        cos_ref, sin_ref = refs[5:7]
        q_ref, kt_ref, v_ref, gb_ref, u_ref, xcs_ref = refs[7:]
    else:
        q_ref, kt_ref, v_ref, gb_ref, u_ref, xcs_ref, kc_ref, vc_ref = refs[-8:]
    x = x_ref[...]
    shift = mod_ref[3:4, :]
    scale = mod_ref[4:5, :]
    gain = g_ref[2:3, :] * (1.0 + scale)
    rows = x.shape[0] // 2
    halves = [(_rms(x[r * rows:(r + 1) * rows, :], gain) + shift).astype(BF16) for r in range(2)]
    hb = jnp.concatenate(halves, axis=0)

    q = jnp.concatenate([_dot(hb_half, win_ref[:, 0:ATT_DIM]) for hb_half in halves], axis=0)
    q = q * (QK_DIM ** -0.5 * LOG2_E)
    k = _dot(hb, win_ref[:, ATT_DIM:2 * ATT_DIM])
    v = _dot(hb, win_ref[:, 2 * ATT_DIM:3 * ATT_DIM])
    v_ref[...] = v.astype(v_ref.dtype)
    if rotary:
        cos = cos_ref[...]
        sin = sin_ref[...]
        lane = lax.broadcasted_iota(jnp.int32, (1, LANES), 1)
        lo_half = (lane % (2 * ROT_FREQS)) < ROT_FREQS
        for hd in range(N_HEADS):
            sl = slice(hd * V_DIM, (hd + 1) * V_DIM)
            q_ref[:, sl] = _rope(q[:, sl], cos, sin, lo_half).astype(q_ref.dtype)
            kt_ref[sl, :] = _rope(k[:, sl], cos, sin, lo_half).T.astype(kt_ref.dtype)
    else:
        q_ref[...] = q.astype(q_ref.dtype)
        for s in range(x.shape[0] // seq):
            rows = slice(s * seq, (s + 1) * seq)
            kt = k[rows, :].T
            kt_ref[s] = kt.astype(kt_ref.dtype)
            if cache_slot is None:
                kc_out, vc_out = kc_ref.at[s], vc_ref.at[s]
            else:
                kc_out, vc_out = kc_ref.at[s, cache_slot], vc_ref.at[s, cache_slot]
                for other in range(DEPTH):
                    if other != cache_slot:
                        kc_ref[s, other] = jnp.zeros(kc_ref.shape[2:], F32)
                        vc_ref[s, other] = jnp.zeros(vc_ref.shape[2:], F32)
            kc_out[...] = kt
            for hd in range(N_HEADS):
                vc_out[pl.ds(hd, seq, stride=N_HEADS), :] = v[rows, hd * V_DIM:(hd + 1) * V_DIM]

    c0 = 3 * ATT_DIM
    gates = _dot(hb, win_ref[:, c0:c0 + 3 * CONV_DIM])
    gb_ref[...] = gates[:, 0:CONV_DIM]
    u_ref[...] = gates[:, CONV_DIM:2 * CONV_DIM] * gates[:, 2 * CONV_DIM:3 * CONV_DIM]
    f = _dot(hb, win_ref[:, c0 + 3 * CONV_DIM:IN_DIM])
    xcs_ref[...] = _dot(f.astype(BF16), cdft_ref[...]).astype(BF16)


def _mixer_in(x, mods, norm_g, win, cdft, rope_tabs, caches, *, layer, mod_row_of_tile, tm, seq):
    n = x.shape[0]
    rotary = rope_tabs is not None
    in_specs = [
        pl.BlockSpec((tm, D_MODEL), lambda i: (i, 0)),
        pl.BlockSpec((None, None, N_MOD, D_MODEL), lambda i: (layer, mod_row_of_tile(i), 0, 0)),
        pl.BlockSpec((None, 6, D_MODEL), lambda i: (layer, 0, 0)),
        _resident((None, D_MODEL, IN_DIM), lambda i: (layer, 0, 0)),
        _resident((FOURIER_DIM, 2 * FOURIER_DIM), lambda i: (0, 0)),
    ]
    args = [x, mods, norm_g, win, cdft]
    if rotary:
        tiles_per_seq = rope_tabs[0].shape[0] // tm
        for tab in rope_tabs:
            in_specs.append(pl.BlockSpec((tm, LANES), lambda i: (i % tiles_per_seq, 0)))
            args.append(tab)

    def rows(width):
        return pl.BlockSpec((tm, width), lambda i: (i, 0))

    batch = n // seq
    if tm <= seq:
        tiles_per_seq = seq // tm
        kt_spec = pl.BlockSpec((None, ATT_DIM, tm), lambda i: (i // tiles_per_seq, 0, i % tiles_per_seq))
    else:
        kt_spec = pl.BlockSpec((tm // seq, ATT_DIM, seq), lambda i: (i, 0, 0))
    out_shape = [
        jax.ShapeDtypeStruct((n, ATT_DIM), BF16),
        jax.ShapeDtypeStruct((batch, ATT_DIM, seq), BF16),
        jax.ShapeDtypeStruct((n, ATT_DIM), BF16),
        jax.ShapeDtypeStruct((n, CONV_DIM), F32),
        jax.ShapeDtypeStruct((n, CONV_DIM), F32),
        jax.ShapeDtypeStruct((n, 2 * FOURIER_DIM), BF16),
    ]
    out_specs = [rows(ATT_DIM), kt_spec, rows(ATT_DIM), rows(CONV_DIM), rows(CONV_DIM),
                 rows(2 * FOURIER_DIM)]
    aliases = {}
    if not rotary:
        seqs = tm // seq
        out_shape += [jax.ShapeDtypeStruct((batch, DEPTH, ATT_DIM, seq), F32),
                      jax.ShapeDtypeStruct((batch, DEPTH, N_HEADS * seq, V_DIM), F32)]
        if caches is None:
            out_specs += [pl.BlockSpec((seqs, DEPTH, ATT_DIM, seq), lambda i: (i, 0, 0, 0)),
                          pl.BlockSpec((seqs, DEPTH, N_HEADS * seq, V_DIM), lambda i: (i, 0, 0, 0))]
        else:
            out_specs += [pl.BlockSpec((seqs, None, ATT_DIM, seq), lambda i: (i, layer, 0, 0)),
                          pl.BlockSpec((seqs, None, N_HEADS * seq, V_DIM), lambda i: (i, layer, 0, 0))]
            first_cache_out = len(out_shape) - 2
            for offset, arr in enumerate(caches):
                aliases[len(args)] = first_cache_out + offset
                in_specs.append(pl.BlockSpec(memory_space=pl.ANY))
                args.append(arr)

    cache_slot = layer if (not rotary and caches is None) else None
    return pl.pallas_call(
        functools.partial(_mixer_in_kernel, rotary=rotary, seq=seq, cache_slot=cache_slot),
        out_shape=out_shape,
        grid=(n // tm,),
        in_specs=in_specs,
        out_specs=out_specs,
        input_output_aliases=aliases,
        compiler_params=_params(("arbitrary",)),
        name="mixer_in",
    )(*args)


def _lambda(lam_ref, lambda_init):
    lq = lam_ref[...]
    s01 = jnp.sum(lq[0:1, :] * lq[1:2, :], axis=-1, keepdims=True)
    s23 = jnp.sum(lq[2:3, :] * lq[3:4, :], axis=-1, keepdims=True)
    return jnp.exp(s01) - jnp.exp(s23) + lambda_init


def _stage_values(vext_sc, v_ref, cv_ref, nb, n):
    for b in range(nb):
        for hd in range(N_HEADS):
            sl = slice(hd * V_DIM, (hd + 1) * V_DIM)
            vext_sc[b, hd, 0:n, 0:V_DIM] = v_ref[b, :, sl]
            if cv_ref is not None:
                past = vext_sc.shape[2] - n
                vext_sc[b, hd, n:, 0:V_DIM] = cv_ref[pl.ds(hd, past, stride=N_HEADS), :].astype(BF16)
            vext_sc[b, hd, :, V_DIM:] = jnp.ones((vext_sc.shape[2], V_DIM), BF16)


def _attend(q_ref, kt_sc, vext_sc, o_ref, lam, sub_g, nb):
    lane = lax.broadcasted_iota(jnp.int32, (1, V_DIM), 1)
    first_map = lane < QK_DIM

    def exponentials(b, hd):
        sl = slice(hd * V_DIM, (hd + 1) * V_DIM)
        q = q_ref[b, :, sl]
        zero = jnp.zeros_like(q)
        out = []
        for keep in (first_map, jnp.logical_not(first_map)):
            qm = jnp.where(keep, q, zero)
            s = _dot(qm, kt_sc[b, sl, :])
            m = jnp.max(s, axis=-1, keepdims=True)
            out.append(jnp.exp2(s - m).astype(BF16))
        return out

    def finish(b, hd, e_maps):
        sl = slice(hd * V_DIM, (hd + 1) * V_DIM)
        vext = vext_sc[b, hd]
        r0, r1 = [_dot(e, vext) for e in e_maps]
        o = r0[:, :V_DIM] / r0[:, V_DIM:] - lam * (r1[:, :V_DIM] / r1[:, V_DIM:])
        o_ref[b, :, sl] = (_rms(o, sub_g)).astype(o_ref.dtype)

    items = [(b, hd) for b in range(nb) for hd in range(N_HEADS)]
    pending = exponentials(*items[0])
    for idx, item in enumerate(items):
        upcoming = exponentials(*items[idx + 1]) if idx + 1 < len(items) else None
        finish(*item, pending)
        pending = upcoming


def _mixer_out_kernel(*refs, seqs, tr, n, parts, has_ctx, lambda_init, dft_resident):
    if has_ctx:
        (x_ref, q_ref, kt_ref, v_ref, ckt_ref, cv_ref, lam_ref, sg_ref, gb_ref, u_ref, xcs_ref, dft_ref, cw_ref,
         mod_ref, g_ref, wout_ref, o_ref, att_ref, vext_sc, kt_sc) = refs
    else:
        (x_ref, q_ref, kt_ref, v_ref, lam_ref, sg_ref, gb_ref, u_ref, xcs_ref, dft_ref, cw_ref,
         mod_ref, g_ref, wout_ref, o_ref, att_ref, vext_sc) = refs
        kt_sc, cv_ref = kt_ref, None

    @pl.when(pl.program_id(1) == 0)
    def _():
        if has_ctx:
            for b in range(seqs):
                kt_sc[b, :, 0:n] = kt_ref[b]
                kt_sc[b, :, n:] = ckt_ref[...].astype(BF16)
        _stage_values(vext_sc, v_ref, cv_ref, seqs, n)

    sub = tr // parts
    gate_g = g_ref[3:4, :] * mod_ref[5:6, :]
    cw = cw_ref[...]
    row = lax.broadcasted_iota(jnp.int32, (sub, 1), 0)
    four_scale = (n * FOURIER_GROUP_DIM) ** -0.5

    def mix(s, t):
        rows = slice(t * sub, (t + 1) * sub)
        r0 = pl.multiple_of(pl.program_id(1) * tr + t * sub, sub)
        dft_rows = pl.ds(r0, sub) if dft_resident else rows
        four = (_dot(dft_ref[dft_rows, 0:n], xcs_ref[s, :, 0:FOURIER_DIM])
                + _dot(dft_ref[dft_rows, n:2 * n], xcs_ref[s, :, FOURIER_DIM:2 * FOURIER_DIM])) * four_scale
        u = u_ref[s, pl.ds(r0, sub), :]
        before_start = pl.multiple_of(jnp.maximum(r0 - SUBLANES, 0), SUBLANES)
        after_start = pl.multiple_of(jnp.minimum(r0 + sub, n - SUBLANES), SUBLANES)
        before = u_ref[s, pl.ds(before_start, SUBLANES), :][SUBLANES - 1:SUBLANES, :]
        after = u_ref[s, pl.ds(after_start, SUBLANES), :][0:1, :]
        before = jnp.where(r0 > 0, before, 0.0)
        after = jnp.where(r0 + sub < n, after, 0.0)
        prev = jnp.where(row == 0, before, pltpu.roll(u, 1, 0))
        nxt = jnp.where(row == sub - 1, after, pltpu.roll(u, sub - 1, 0))
        conv = gb_ref[s, rows, :] * (prev * cw[0:1, :] + u * cw[1:2, :] + nxt * cw[2:3, :])
        return conv.astype(BF16), four.astype(BF16)

    def project(s, t, conv, four):
        rows = slice(t * sub, (t + 1) * sub)
        y = (_dot(att_ref[s, rows, :], wout_ref[0:ATT_DIM, :])
             + _dot(conv, wout_ref[ATT_DIM:ATT_DIM + CONV_DIM, :])
             + _dot(four, wout_ref[ATT_DIM + CONV_DIM:D_MODEL, :]))
        o_ref[s, rows, :] = x_ref[s, rows, :] + _rms(y, gate_g)

    items = [(s, t) for s in range(seqs) for t in range(parts)]
    attend = functools.partial(_attend, q_ref, kt_sc, vext_sc, att_ref, _lambda(lam_ref, lambda_init),
                               sg_ref[...] * (1.0 - lambda_init), seqs)
    if len(items) == 1:
        pending = mix(*items[0])
        attend()
    else:
        attend()
        pending = mix(*items[0])
    for idx, item in enumerate(items):
        upcoming = mix(*items[idx + 1]) if idx + 1 < len(items) else None
        project(*item, *pending)
        pending = upcoming


def _mixer_out(x, q, kt, v, ctx_kt, ctx_v, lam_qk, subln_g, gb, u, xcs, dft, conv_w, mods, norm_g, wout, *,
               layer, mod_row_of_batch, seqs, tr, parts):
    b, n, _ = x.shape
    has_ctx = ctx_kt is not None
    assert not has_ctx or seqs == 1
    dft_resident = tr == n

    def tile(width):
        return pl.BlockSpec((seqs, tr, width), lambda i, j: (i, j, 0))

    def whole(width):
        return pl.BlockSpec((seqs, n, width), lambda i, j: (i, 0, 0))

    in_specs = [tile(D_MODEL), tile(ATT_DIM), pl.BlockSpec((seqs, ATT_DIM, n), lambda i, j: (i, 0, 0)),
                whole(ATT_DIM)]
    args = [x, q, kt, v]
    if has_ctx:
        for arr in (ctx_kt, ctx_v):
            in_specs.append(pl.BlockSpec((None, None) + arr.shape[2:], lambda i, j: (i, layer, 0, 0)))
            args.append(arr)
    in_specs += [
        pl.BlockSpec((None, 4, QK_DIM), lambda i, j: (layer, 0, 0)),
        pl.BlockSpec((None, 1, V_DIM), lambda i, j: (layer, 0, 0)),
        tile(CONV_DIM), whole(CONV_DIM), whole(2 * FOURIER_DIM),
        _resident((n, 2 * n), lambda i, j: (0, 0)) if dft_resident
        else pl.BlockSpec((tr, 2 * n), lambda i, j: (j, 0)),
        pl.BlockSpec((None, 3, CONV_DIM), lambda i, j: (layer, 0, 0)),
        pl.BlockSpec((None, None, N_MOD, D_MODEL), lambda i, j: (layer, mod_row_of_batch(i), 0, 0)),
        pl.BlockSpec((None, 6, D_MODEL), lambda i, j: (layer, 0, 0)),
        _resident((None, D_MODEL, D_MODEL), lambda i, j: (layer, 0, 0)),
    ]
    args += [lam_qk, subln_g.reshape(DEPTH, 1, V_DIM), gb, u, xcs, dft, conv_w, mods, norm_g, wout]
    n_keys = n + (ctx_kt.shape[3] if has_ctx else 0)
    scratch = [pltpu.VMEM((seqs, tr, ATT_DIM), BF16), pltpu.VMEM((seqs, N_HEADS, n_keys, 2 * V_DIM), BF16)]
    if has_ctx:
        scratch.append(pltpu.VMEM((seqs, ATT_DIM, n_keys), BF16))
    return pl.pallas_call(
        functools.partial(_mixer_out_kernel, seqs=seqs, tr=tr, n=n, parts=parts, has_ctx=has_ctx,
                          lambda_init=_lambda_init(layer), dft_resident=dft_resident),
        out_shape=jax.ShapeDtypeStruct((b, n, D_MODEL), F32),
        grid=(b // seqs, n // tr),
        in_specs=in_specs,
        out_specs=tile(D_MODEL),
        scratch_shapes=scratch,
        compiler_params=_params(("arbitrary", "arbitrary")),
        name="mixer_out",
    )(*args)


def _position_dft(n):
    jk = (np.arange(n, dtype=np.int64)[:, None] * np.arange(n, dtype=np.int64)[None, :]) % n
    ang = 2.0 * np.pi * jk.astype(np.float64) / n
    return np.concatenate([np.cos(ang), -np.sin(ang)], axis=1).astype(np.float32)


def _channel_dft():
    g = FOURIER_GROUP_DIM
    jk = (np.arange(g)[:, None] * np.arange(g)[None, :]) % g
    ang = 2.0 * np.pi * jk.astype(np.float64) / g
    eye = np.eye(FOURIER_DIM // g)
    return np.concatenate([np.kron(eye, np.cos(ang)), np.kron(eye, np.sin(ang))], axis=1).astype(np.float32)


def _rope_tables(n):
    t = jnp.arange(n)
    row = (t // GRID_W).astype(F32)
    col = (t % GRID_W).astype(F32)
    inv = 1.0 / (ROPE_THETA ** (jnp.arange(ROT_FREQS, dtype=F32) / ROT_FREQS))
    ang_r = row[:, None] * inv
    ang_c = col[:, None] * inv
    cos = jnp.concatenate([jnp.cos(ang_r)] * 2 + [jnp.cos(ang_c)] * 2, axis=-1)
    sin = jnp.concatenate([-jnp.sin(ang_r), jnp.sin(ang_r), -jnp.sin(ang_c), jnp.sin(ang_c)], axis=-1)
    return jnp.tile(cos, (1, 2)), jnp.tile(sin, (1, 2))


class _Tiling(NamedTuple):
    token_rows: int
    seqs: int
    seq_rows: int
    parts: int


def _tiling(seq_len):
    if seq_len <= MIXER_ROWS:
        return _Tiling(TOKEN_ROWS, TOKEN_ROWS // seq_len, seq_len, 1)
    return _Tiling(TOKEN_ROWS, 1, MIXER_ROWS, 1)


def _run_path(x, mods, mod_row, weights, ctx, rope_tabs, dft):
    norm_g, wgu, wd, win, wout, cdft, conv_w, lam_qk, subln_g = weights
    b, n, _ = x.shape
    tm, seqs, tr, parts = _tiling(n)
    tiles_per_seq = max(n // tm, 1)
    seqs_per_tile = max(tm // n, 1)

    def mod_row_of_tile(i):
        return mod_row(i * seqs_per_tile // tiles_per_seq)

    def shape3(a):
        return a.reshape(b, n, a.shape[-1])

    ctx_k, ctx_v = ctx if ctx is not None else (None, None)
    caches = None
    for l in range(DEPTH):
        flat = x.reshape(b * n, D_MODEL)
        flat = _ffn(flat, mods, norm_g, wgu, wd, layer=l, half=0, mod_row_of_tile=mod_row_of_tile, tm=tm)
        q, kt, v, gb, u, xcs, *caches = _mixer_in(flat, mods, norm_g, win, cdft, rope_tabs, caches, layer=l,
                                                  mod_row_of_tile=mod_row_of_tile, tm=tm, seq=n)
        caches = caches or None
        x = _mixer_out(shape3(flat), shape3(q), kt, shape3(v), ctx_k, ctx_v, lam_qk, subln_g, shape3(gb),
                       shape3(u), shape3(xcs), dft, conv_w, mods, norm_g, wout, layer=l,
                       mod_row_of_batch=lambda i: mod_row(i * seqs), seqs=seqs, tr=tr, parts=parts)
        flat = _ffn(x.reshape(b * n, D_MODEL), mods, norm_g, wgu, wd, layer=l, half=1,
                    mod_row_of_tile=mod_row_of_tile, tm=tm)
        x = flat.reshape(b, n, D_MODEL)
    return x, caches


def kernel(x_prompt, x_sample, cache_k, cache_v, c, c_ctx, w_mod, b_mod, norm_g, w_ffn_gu, w_ffn_down,
           w_in, w_out, conv_w, lam_qk, subln_g):
    batch, seq, _ = x_prompt.shape
    dec_batch, dec_seq, _ = x_sample.shape
    past = cache_k.shape[2]

    cvecs = jnp.zeros((MOD_ROWS, D_MODEL), F32).at[0].set(c_ctx).at[1:1 + dec_batch].set(c)
    mods = _modulation(cvecs, w_mod, b_mod)

    wgu, wd = w_ffn_gu, w_ffn_down
    win = _cast_bf16(w_in.reshape(DEPTH * D_MODEL, IN_DIM), D_MODEL // 2).reshape(DEPTH, D_MODEL, IN_DIM)
    wout = _cast_bf16(w_out.reshape(DEPTH * D_MODEL, D_MODEL), D_MODEL).reshape(DEPTH, D_MODEL, D_MODEL)
    cdft = jnp.asarray(_channel_dft()).astype(BF16)
    weights = (norm_g, wgu, wd, win, wout, cdft, conv_w, lam_qk, subln_g)

    dft_ctx = jnp.asarray(_position_dft(seq)).astype(BF16)
    y_prompt, (kc, vc) = _run_path(x_prompt, mods, lambda i: 0, weights, None, None, dft_ctx)
    new_cache_k = kc.reshape(batch, DEPTH, N_HEADS, 2, QK_DIM, seq).transpose(0, 1, 5, 2, 3, 4)
    new_cache_v = vc.reshape(batch, DEPTH, seq, N_HEADS, V_DIM)

    dft_lat = jnp.asarray(_position_dft(dec_seq)).astype(BF16)
    ctx = (cache_k.transpose(0, 1, 3, 4, 5, 2).reshape(dec_batch, DEPTH, ATT_DIM, past),
           cache_v.reshape(dec_batch, DEPTH, past * N_HEADS, V_DIM))
    y_sample, _ = _run_path(x_sample, mods, lambda i: 1 + i, weights, ctx, _rope_tables(dec_seq), dft_lat)
    return (y_prompt, y_sample, new_cache_k, new_cache_v)
```
